```python
import math
import jax, jax.numpy as jnp
from jax import lax
import numpy as np

D_MODEL = 2048
BATCH = 4
SEQ = 2048
DEPTH = 4
DEC_BATCH = 128
DEC_SEQ = 4
PAST_LEN = 16384
PAGE_SIZE = 128

N_MIXERS = 2
CONV_WIDTH = 31
CONV_STATE = CONV_WIDTH - 1
GROUP_SIZE = 16
N_GROUPS = D_MODEL // GROUP_SIZE
STATE_DIM = 64
D_FF = ((8 * D_MODEL // 3 + 255) // 256) * 256
FFN_CONV_WIDTH = 3
FFN_STATE = FFN_CONV_WIDTH - 1
N_CONV_LAYERS = (DEPTH + 1) // 2
N_SSM_LAYERS = DEPTH // 2
EPS = 1e-6

kernel_name = "hybrid_conformer_conv_s5_convffn_step"


def rms_norm(x, g):
    xf = x.astype(jnp.float32)
    y = xf * lax.rsqrt(jnp.mean(xf * xf, axis=-1, keepdims=True) + EPS)
    return (y * g.astype(jnp.float32)).astype(x.dtype)


def layer_norm(x, g, b):
    xf = x.astype(jnp.float32)
    mu = jnp.mean(xf, axis=-1, keepdims=True)
    var = jnp.mean(jnp.square(xf - mu), axis=-1, keepdims=True)
    y = (xf - mu) * lax.rsqrt(var + EPS)
    return (y * g.astype(jnp.float32) + b.astype(jnp.float32)).astype(x.dtype)


def causal_depthwise(ext, w):
    c = ext.shape[-1]
    return lax.conv_general_dilated(ext, w[:, None, :].astype(ext.dtype), window_strides=(1,), padding='VALID',
                                    dimension_numbers=('NWC', 'WIO', 'NWC'), feature_group_count=c)


def conformer_conv(h, cache, w_in, dw, dw_b, ln_g, ln_b, w_out):
    z = h @ w_in
    u = z[..., :D_MODEL] * jax.nn.sigmoid(z[..., D_MODEL:])
    ext = jnp.concatenate([cache.astype(u.dtype), u], axis=1)
    c = causal_depthwise(ext, dw) + dw_b
    c = jax.nn.silu(layer_norm(c, ln_g, ln_b))
    return c @ w_out, ext[:, -CONV_STATE:]


def _ssm_combine(e1, e2):
    ar1, ai1, br1, bi1 = e1
    ar2, ai2, br2, bi2 = e2
    return (ar2 * ar1 - ai2 * ai1,
            ar2 * ai1 + ai2 * ar1,
            ar2 * br1 - ai2 * bi1 + br2,
            ar2 * bi1 + ai2 * br1 + bi2)


def s5_layer(h, h0_re, h0_im, a_re, a_im, log_dt, b_re, b_im, c_re, c_im, d_skip, w_glu):
    f32 = jnp.float32
    bsz, seqlen, _ = h.shape
    u = h.astype(f32)
    ug = u.reshape(bsz, seqlen, N_GROUPS, GROUP_SIZE)
    lam_re, lam_im = a_re.astype(f32), a_im.astype(f32)
    dt = jnp.exp(log_dt.astype(f32))[:, None]
    mag = jnp.exp(lam_re * dt)
    ang = lam_im * dt
    abar_re, abar_im = mag * jnp.cos(ang), mag * jnp.sin(ang)
    nr, ni = abar_re - 1.0, abar_im
    den = lam_re * lam_re + lam_im * lam_im
    q_re = (nr * lam_re + ni * lam_im) / den
    q_im = (ni * lam_re - nr * lam_im) / den
    br, bi = b_re.astype(f32), b_im.astype(f32)
    bb_re = q_re[..., None] * br - q_im[..., None] * bi
    bb_im = q_re[..., None] * bi + q_im[..., None] * br
    bu_re = jnp.einsum('blgi,gpi->blgp', ug, bb_re)
    bu_im = jnp.einsum('blgi,gpi->blgp', ug, bb_im)
    s_re, s_im = h0_re.astype(f32), h0_im.astype(f32)
    bu_re = bu_re.at[:, 0].add(abar_re * s_re - abar_im * s_im)
    bu_im = bu_im.at[:, 0].add(abar_re * s_im + abar_im * s_re)
    a_full_re = jnp.broadcast_to(abar_re, bu_re.shape)
    a_full_im = jnp.broadcast_to(abar_im, bu_im.shape)
    _, _, st_re, st_im = lax.associative_scan(_ssm_combine, (a_full_re, a_full_im, bu_re, bu_im), axis=1)
    y = (jnp.einsum('blgp,gip->blgi', st_re, c_re.astype(f32))
         - jnp.einsum('blgp,gip->blgi', st_im, c_im.astype(f32)))
    y = y.reshape(bsz, seqlen, D_MODEL) + d_skip.astype(f32) * u
    v = jax.nn.gelu(y).astype(h.dtype)
    z = v @ w_glu
    out = z[..., :D_MODEL] * jax.nn.sigmoid(z[..., D_MODEL:])
    return out, st_re[:, -1].astype(h0_re.dtype), st_im[:, -1].astype(h0_im.dtype)


def conv_ffn(h, cache, w_gate, w_up, conv_w, w_down):
    g = h @ w_gate
    ext = jnp.concatenate([cache.astype(g.dtype), g], axis=1)
    gc = causal_depthwise(ext, conv_w)
    out = (jax.nn.silu(gc) * (h @ w_up)) @ w_down
    return out, ext[:, -FFN_STATE:]


def setup_inputs(seed: int = 0) -> dict:
    key = jax.random.key(seed)
    ks = jax.random.split(key, 32)
    f32 = jnp.float32
    nrm = lambda k, shape, s: jax.random.normal(k, shape, f32) * s
    D, F, G, P, NC, NS = D_MODEL, D_FF, N_GROUPS, STATE_DIM, N_CONV_LAYERS, N_SSM_LAYERS
    n_idx = jnp.arange(P, dtype=f32)
    return {
        "x_prompt": nrm(ks[0], (BATCH, SEQ, D), 1.0),
        "x_sample": nrm(ks[1], (DEC_BATCH, DEC_SEQ, D), 1.0),
        "state_conv": nrm(ks[2], (NC, DEC_BATCH, CONV_STATE, D), 0.5),
        "state_ssm_re": nrm(ks[3], (NS, DEC_BATCH, G, P), 0.1),
        "state_ssm_im": nrm(ks[4], (NS, DEC_BATCH, G, P), 0.1),
        "state_ffn": nrm(ks[5], (DEPTH, DEC_BATCH, FFN_STATE, F), 1.0),
        "norm_mix": 1.0 + nrm(ks[6], (DEPTH, D), 0.02),
        "norm_ffn": 1.0 + nrm(ks[7], (DEPTH, D), 0.02),
        "norm_final": 1.0 + nrm(ks[8], (D,), 0.02),
        "conv_w_in": nrm(ks[9], (NC, D, 2 * D), D ** -0.5),
        "conv_dw": nrm(ks[10], (NC, CONV_WIDTH, D), CONV_WIDTH ** -0.5),
        "conv_dw_b": nrm(ks[11], (NC, D), 0.02),
        "conv_ln_g": 1.0 + nrm(ks[12], (NC, D), 0.02),
        "conv_ln_b": nrm(ks[13], (NC, D), 0.02),
        "conv_w_out": nrm(ks[14], (NC, D, D), D ** -0.5),
        "ssm_A_re": -0.5 + nrm(ks[15], (NS, G, P), 0.01),
        "ssm_A_im": math.pi * n_idx + nrm(ks[16], (NS, G, P), 0.01),
        "ssm_log_dt": jax.random.uniform(ks[17], (NS, G), f32, math.log(1e-3), math.log(1e-1)),
        "ssm_B_re": nrm(ks[18], (NS, G, P, GROUP_SIZE), GROUP_SIZE ** -0.5),
        "ssm_B_im": nrm(ks[19], (NS, G, P, GROUP_SIZE), GROUP_SIZE ** -0.5),
        "ssm_C_re": nrm(ks[20], (NS, G, GROUP_SIZE, P), P ** -0.5),
        "ssm_C_im": nrm(ks[21], (NS, G, GROUP_SIZE, P), P ** -0.5),
        "ssm_D": nrm(ks[22], (NS, D), 1.0),
        "ssm_w_glu": nrm(ks[23], (NS, D, 2 * D), D ** -0.5),
        "ffn_w_gate": nrm(ks[24], (DEPTH, D, F), D ** -0.5),
        "ffn_w_up": nrm(ks[25], (DEPTH, D, F), D ** -0.5),
        "ffn_conv": nrm(ks[26], (DEPTH, FFN_CONV_WIDTH, F), FFN_CONV_WIDTH ** -0.5),
        "ffn_w_down": nrm(ks[27], (DEPTH, F, D), F ** -0.5),
    }


def reference(x_prompt, x_sample, state_conv, state_ssm_re, state_ssm_im, state_ffn,
              norm_mix, norm_ffn, norm_final,
              conv_w_in, conv_dw, conv_dw_b, conv_ln_g, conv_ln_b, conv_w_out,
              ssm_A_re, ssm_A_im, ssm_log_dt, ssm_B_re, ssm_B_im, ssm_C_re, ssm_C_im, ssm_D, ssm_w_glu,
              ffn_w_gate, ffn_w_up, ffn_conv, ffn_w_down):

    def run(x, conv_st, ssm_re_st, ssm_im_st, ffn_st):
        new_conv, new_re, new_im, new_ffn = [], [], [], []
        for i in range(DEPTH):
            j = i // N_MIXERS
            hn = rms_norm(x, norm_mix[i])
            if i % N_MIXERS == 0:
                mix, c_new = conformer_conv(hn, conv_st[j], conv_w_in[j], conv_dw[j], conv_dw_b[j],
                                            conv_ln_g[j], conv_ln_b[j], conv_w_out[j])
                new_conv.append(c_new)
            else:
                mix, s_re, s_im = s5_layer(hn, ssm_re_st[j], ssm_im_st[j], ssm_A_re[j], ssm_A_im[j],
                                           ssm_log_dt[j], ssm_B_re[j], ssm_B_im[j], ssm_C_re[j],
                                           ssm_C_im[j], ssm_D[j], ssm_w_glu[j])
                new_re.append(s_re)
                new_im.append(s_im)
            x = x + mix
            f_out, f_new = conv_ffn(rms_norm(x, norm_ffn[i]), ffn_st[i], ffn_w_gate[i], ffn_w_up[i],
                                    ffn_conv[i], ffn_w_down[i])
            new_ffn.append(f_new)
            x = x + f_out
        y = rms_norm(x, norm_final)
        return (y, jnp.stack(new_conv), jnp.stack(new_re), jnp.stack(new_im), jnp.stack(new_ffn))

    bp = x_prompt.shape[0]
    zc = jnp.zeros((N_CONV_LAYERS, bp, CONV_STATE, D_MODEL), x_prompt.dtype)
    zs_re = jnp.zeros((N_SSM_LAYERS, bp, N_GROUPS, STATE_DIM), state_ssm_re.dtype)
    zs_im = jnp.zeros((N_SSM_LAYERS, bp, N_GROUPS, STATE_DIM), state_ssm_im.dtype)
    zf = jnp.zeros((DEPTH, bp, FFN_STATE, D_FF), x_prompt.dtype)
    y_prompt, conv_p, re_p, im_p, ffn_p = run(x_prompt, zc, zs_re, zs_im, zf)
    y_sample, conv_s, re_s, im_s, ffn_s = run(x_sample, state_conv, state_ssm_re, state_ssm_im, state_ffn)
    return (y_prompt, y_sample, conv_p, conv_s, re_p, im_p, re_s, im_s, ffn_p, ffn_s)
```

```python
import functools
import math

import jax
import jax.numpy as jnp
from jax import lax
from jax.experimental import pallas as pl
from jax.experimental.pallas import tpu as pltpu

F32 = jnp.float32
BF16 = jnp.bfloat16

EPS = 1e-6
N_MIXERS = 2
GROUP_SIZE = 16
LANES = 128
SUBLANES = 8
VMEM_LIMIT = 56 * 1024 * 1024


def _cparams(n_axes):
    return pltpu.CompilerParams(
        dimension_semantics=("arbitrary",) * n_axes, vmem_limit_bytes=VMEM_LIMIT)


def _rms_scale(ss, n):
    return lax.rsqrt(ss / n + EPS)


def _rmsnorm_body(x_ref, g_ref, o_ref):
    x = x_ref[...]
    ss = jnp.sum(x * x, axis=-1, keepdims=True)
    o_ref[...] = ((x * _rms_scale(ss, x.shape[-1])) * g_ref[...]).astype(o_ref.dtype)


def _rmsnorm(x, g, out_dtype, bm):
    m, d = x.shape
    return pl.pallas_call(
        _rmsnorm_body,
        out_shape=jax.ShapeDtypeStruct((m, d), out_dtype),
        grid=(m // bm,),
        in_specs=[pl.BlockSpec((bm, d), lambda i: (i, 0)),
                  pl.BlockSpec((1, d), lambda i: (0, 0))],
        out_specs=pl.BlockSpec((bm, d), lambda i: (i, 0)),
        compiler_params=_cparams(1),
        name="rmsnorm",
    )(x, g.reshape(1, d))


def _mm_body(*refs, glu, has_res, write_x, nn, bn, n_out):
    it = iter(refs)
    lhs_ref = next(it)
    wa_ref = next(it)
    wb_ref = next(it) if glu else None
    res_ref = next(it) if has_res else None
    g_ref = next(it) if has_res else None
    xo_ref = next(it) if write_x else None
    hn_ref = next(it) if has_res else None
    rowbuf = next(it) if has_res else None

    n = pl.program_id(1)
    x = lhs_ref[...]
    val = jnp.dot(x, wa_ref[...], preferred_element_type=F32)
    if glu:
        gate = jnp.dot(x, wb_ref[...], preferred_element_type=F32)
        val = val * jax.nn.sigmoid(gate)
    if has_res:
        val = res_ref[...] + val
    if write_x:
        xo_ref[...] = val
    if has_res:
        rowbuf[n] = val

        @pl.when(n == nn - 1)
        def _():
            ss = jnp.zeros((val.shape[0], 1), F32)
            for j in range(nn):
                r = rowbuf[j]
                ss = ss + jnp.sum(r * r, axis=-1, keepdims=True)
            scale = _rms_scale(ss, n_out)
            for j in range(nn):
                sl = slice(j * bn, (j + 1) * bn)
                hn_ref[:, sl] = ((rowbuf[j] * scale) * g_ref[:, sl]).astype(hn_ref.dtype)


def _mm(lhs, w, layer, *, glu, bm, bn, resid=None, gamma=None, hn_dtype=None, write_x=True):
    m, k = lhs.shape
    n_out = w.shape[2] // (2 if glu else 1)
    nn = n_out // bn
    has_res = resid is not None
    in_specs = [pl.BlockSpec((bm, k), lambda i, j: (i, 0)),
                pl.BlockSpec((None, k, bn), lambda i, j: (layer, 0, j))]
    args = [lhs, w]
    if glu:
        in_specs.append(pl.BlockSpec((None, k, bn), lambda i, j: (layer, 0, j + nn)))
        args.append(w)
    out_shape, out_specs, scratch = [], [], []
    if has_res:
        in_specs += [pl.BlockSpec((bm, bn), lambda i, j: (i, j)),
                     pl.BlockSpec((1, n_out), lambda i, j: (0, 0))]
        args += [resid, gamma.reshape(1, n_out)]
    if write_x:
        out_shape.append(jax.ShapeDtypeStruct((m, n_out), F32))
        out_specs.append(pl.BlockSpec((bm, bn), lambda i, j: (i, j)))
    if has_res:
        out_shape.append(jax.ShapeDtypeStruct((m, n_out), hn_dtype))
        out_specs.append(pl.BlockSpec((bm, n_out), lambda i, j: (i, 0)))
        scratch.append(pltpu.VMEM((nn, bm, bn), F32))
    outs = pl.pallas_call(
        functools.partial(_mm_body, glu=glu, has_res=has_res, write_x=write_x,
                          nn=nn, bn=bn, n_out=n_out),
        out_shape=out_shape,
        grid=(m // bm, nn),
        in_specs=in_specs,
        out_specs=out_specs,
        scratch_shapes=scratch,
        compiler_params=_cparams(2),
        name="mm_glu" if glu else "mm_lin",
    )(*args)
    return outs


def _ffn1_body(*refs, sample, bm, tiles_per_seq, seq_len):
    if sample:
        h_ref, wg_ref, wu_ref, cw_ref, fix1_ref, fix2_ref, act_ref, g_ref, gbuf = refs
    else:
        h_ref, wg_ref, wu_ref, cw_ref, act_ref, tail_ref, gbuf, carry = refs
    m = pl.program_id(0)
    n = pl.program_id(1)
    h = h_ref[...]
    g = jnp.dot(h, wg_ref[...], preferred_element_type=F32)
    up = jnp.dot(h, wu_ref[...], preferred_element_type=F32)
    gbuf[SUBLANES:SUBLANES + bm, :] = g
    if sample:
        gbuf[0:SUBLANES, :] = jnp.zeros((SUBLANES, g.shape[1]), F32)
        g_ref[...] = g
    else:
        @pl.when(m % tiles_per_seq == 0)
        def _():
            gbuf[0:SUBLANES, :] = jnp.zeros((SUBLANES, g.shape[1]), F32)

        @pl.when(m % tiles_per_seq != 0)
        def _():
            gbuf[0:SUBLANES, :] = carry[n]

        last8 = gbuf[bm:bm + SUBLANES, :]
        carry[n] = last8
        tail_ref[...] = last8
    gp1 = gbuf[SUBLANES - 1:SUBLANES - 1 + bm, :]
    gp2 = gbuf[SUBLANES - 2:SUBLANES - 2 + bm, :]
    if sample:
        t = lax.broadcasted_iota(jnp.int32, g.shape, 0) % seq_len
        gp1 = jnp.where(t >= 1, gp1, fix1_ref[...])
        gp2 = jnp.where(t >= 2, gp2, fix2_ref[...])
    cw = cw_ref[...]
    gc = gp2 * cw[0:1, :] + gp1 * cw[1:2, :] + g * cw[2:3, :]
    act_ref[...] = (jax.nn.silu(gc) * up).astype(act_ref.dtype)


def _ffn1(h, wg, wu, cw, layer, *, bm, bn, seq_len, fix=None):
    m, k = h.shape
    f = wg.shape[2]
    nn = f // bn
    sample = fix is not None
    tiles_per_seq = max(seq_len // bm, 1)
    in_specs = [pl.BlockSpec((bm, k), lambda i, j: (i, 0)),
                pl.BlockSpec((None, k, bn), lambda i, j: (layer, 0, j)),
                pl.BlockSpec((None, k, bn), lambda i, j: (layer, 0, j)),
                pl.BlockSpec((None, cw.shape[1], bn), lambda i, j: (layer, 0, j))]
    args = [h, wg, wu, cw]
    out_shape = [jax.ShapeDtypeStruct((m, f), BF16)]
    out_specs = [pl.BlockSpec((bm, bn), lambda i, j: (i, j))]
    scratch = [pltpu.VMEM((SUBLANES + bm, bn), F32)]
    if sample:
        in_specs += [pl.BlockSpec((bm, bn), lambda i, j: (i, j))] * 2
        args += list(fix)
        out_shape.append(jax.ShapeDtypeStruct((m, f), F32))
        out_specs.append(pl.BlockSpec((bm, bn), lambda i, j: (i, j)))
    else:
        out_shape.append(jax.ShapeDtypeStruct((m // bm, SUBLANES, f), F32))
        out_specs.append(pl.BlockSpec((None, SUBLANES, bn), lambda i, j: (i, 0, j)))
        scratch.append(pltpu.VMEM((nn, SUBLANES, bn), F32))
    outs = pl.pallas_call(
        functools.partial(_ffn1_body, sample=sample, bm=bm, tiles_per_seq=tiles_per_seq,
                          seq_len=seq_len),
        out_shape=out_shape,
        grid=(m // bm, nn),
        in_specs=in_specs,
        out_specs=out_specs,
        scratch_shapes=scratch,
        compiler_params=_cparams(2),
        name="ffn_gate_up",
    )(*args)
    if sample:
        return outs
    act, tails = outs
    return act, tails[tiles_per_seq - 1::tiles_per_seq]


CONV_ROWS = 32
HALO = 32


def _ln_silu(c, g, b):
    mu = jnp.mean(c, axis=-1, keepdims=True)
    d = c - mu
    var = jnp.mean(d * d, axis=-1, keepdims=True)
    y = d * lax.rsqrt(var + EPS)
    return jax.nn.silu(y * g + b)


def _conv_p_body(u_ref, dw_ref, db_ref, lg_ref, lb_ref, o_ref, ubuf, cbuf, *, bm, taps,
                 tiles_per_seq):
    m = pl.program_id(0)
    d = u_ref.shape[1]
    nc = d // LANES

    @pl.when(m % tiles_per_seq == 0)
    def _():
        ubuf[:, 0:HALO, :] = jnp.zeros((nc, HALO, LANES), F32)

    @pl.when(m % tiles_per_seq != 0)
    def _():
        ubuf[:, 0:HALO, :] = ubuf[:, bm:bm + HALO, :]

    for c in range(nc):
        ubuf[c, HALO:HALO + bm, :] = u_ref[:, c * LANES:(c + 1) * LANES]
    off = HALO - (taps - 1)

    def lane_chunk(c, carry):
        wv = dw_ref[c]
        bias = db_ref[c]
        for r0 in range(0, bm, CONV_ROWS):
            acc = jnp.zeros((CONV_ROWS, LANES), F32)
            for k in range(taps):
                acc = acc + ubuf[c, r0 + off + k:r0 + off + k + CONV_ROWS, :] * wv[k:k + 1, :]
            cbuf[c, r0:r0 + CONV_ROWS, :] = acc + bias
        return carry

    lax.fori_loop(0, nc, lane_chunk, 0)
    cfull = jnp.concatenate([cbuf[c] for c in range(nc)], axis=-1)
    o_ref[...] = _ln_silu(cfull, lg_ref[...], lb_ref[...]).astype(o_ref.dtype)


def _lane_chunked(x):
    l, r, d = x.shape
    return x.reshape(l, r, d // LANES, LANES).transpose(0, 2, 1, 3)


def _conv_prompt(u, dw, db, lg, lb, layer, *, bm, seq_len):
    m, d = u.shape
    taps = dw.shape[1]
    nc = d // LANES
    vec = lambda: pl.BlockSpec((None, 1, d), lambda i: (layer, 0, 0))
    return pl.pallas_call(
        functools.partial(_conv_p_body, bm=bm, taps=taps, tiles_per_seq=seq_len // bm),
        out_shape=jax.ShapeDtypeStruct((m, d), BF16),
        grid=(m // bm,),
        in_specs=[pl.BlockSpec((bm, d), lambda i: (i, 0)),
                  pl.BlockSpec((None, nc, taps, LANES), lambda i: (layer, 0, 0, 0)),
                  pl.BlockSpec((None, nc, 1, LANES), lambda i: (layer, 0, 0, 0)),
                  vec(), vec()],
        out_specs=pl.BlockSpec((bm, d), lambda i: (i, 0)),
        scratch_shapes=[pltpu.VMEM((nc, HALO + bm, LANES), F32),
                        pltpu.VMEM((nc, bm, LANES), F32)],
        compiler_params=_cparams(1),
        name="conv_prompt",
    )(u, _lane_chunked(dw), _lane_chunked(db.reshape(db.shape[0], 1, d)),
      lg.reshape(lg.shape[0], 1, d), lb.reshape(lb.shape[0], 1, d))


def _conv_s_body(ext_ref, dw_ref, db_ref, lg_ref, lb_ref, o_ref, cbuf, *, nb, taps, seq):
    d = dw_ref.shape[1]
    for c in range(d // LANES):
        cs = slice(c * LANES, (c + 1) * LANES)
        wv = dw_ref[:, cs]
        bias = db_ref[:, cs]
        for b in range(nb):
            acc = jnp.zeros((seq, LANES), F32)
            for k in range(taps):
                acc = acc + ext_ref[b, k:k + seq, cs] * wv[k:k + 1, :]
            cbuf[b * seq:(b + 1) * seq, cs] = acc + bias
    o_ref[...] = _ln_silu(cbuf[...], lg_ref[...], lb_ref[...]).astype(o_ref.dtype)


def _conv_sample(ext, dw, db, lg, lb, layer, *, nb, seq):
    b, rows, d = ext.shape
    taps = dw.shape[1]
    vec = lambda: pl.BlockSpec((None, 1, d), lambda i: (layer, 0, 0))
    return pl.pallas_call(
        functools.partial(_conv_s_body, nb=nb, taps=taps, seq=seq),
        out_shape=jax.ShapeDtypeStruct((b * seq, d), BF16),
        grid=(b // nb,),
        in_specs=[pl.BlockSpec((nb, rows, d), lambda i: (i, 0, 0)),
                  pl.BlockSpec((None, taps, d), lambda i: (layer, 0, 0)),
                  vec(), vec(), vec()],
        out_specs=pl.BlockSpec((nb * seq, d), lambda i: (i, 0)),
        scratch_shapes=[pltpu.VMEM((nb * seq, d), F32)],
        compiler_params=_cparams(1),
        name="conv_sample",
    )(ext, dw, db.reshape(db.shape[0], 1, d), lg.reshape(lg.shape[0], 1, d),
      lb.reshape(lb.shape[0], 1, d))


def _s5_abar(lam_re, lam_im, dt):
    mag = jnp.exp(lam_re * dt)
    ang = lam_im * dt
    return mag * jnp.cos(ang), mag * jnp.sin(ang)


def _s5_disc_body(are_ref, aim_ref, ldt_ref, arep_ref, airep_ref, bre_ref, bim_ref,
                  abr_ref, abi_ref, bbr_ref, bbi_ref):
    dt = jnp.exp(ldt_ref[...])
    abar_re, abar_im = _s5_abar(are_ref[...], aim_ref[...], dt)
    abr_ref[...] = abar_re
    abi_ref[...] = abar_im
    lam_re = arep_ref[...]
    lam_im = airep_ref[...]
    rep_re, rep_im = _s5_abar(lam_re, lam_im, dt)
    nr = rep_re - 1.0
    ni = rep_im
    den = lam_re * lam_re + lam_im * lam_im
    q_re = (nr * lam_re + ni * lam_im) / den
    q_im = (ni * lam_re - nr * lam_im) / den
    br = bre_ref[...]
    bi = bim_ref[...]
    bbr_ref[...] = q_re * br - q_im * bi
    bbi_ref[...] = q_re * bi + q_im * br


def _s5_discretise(a_re, a_im, log_dt, b_re, b_im):
    ns, g, p = a_re.shape
    rows = ns * g
    two = lambda x: x.reshape(rows, p)
    rep = lambda x: jnp.repeat(x.reshape(rows, p), GROUP_SIZE, axis=1)
    wide = lambda x: x.reshape(rows, p * GROUP_SIZE)
    outs = pl.pallas_call(
        _s5_disc_body,
        out_shape=[jax.ShapeDtypeStruct((rows, p), F32)] * 2
        + [jax.ShapeDtypeStruct((rows, p * GROUP_SIZE), F32)] * 2,
        name="s5_discretise",
    )(two(a_re), two(a_im), log_dt.reshape(rows, 1), rep(a_re), rep(a_im), wide(b_re), wide(b_im))
    abr, abi, bbr, bbi = outs
    return (abr.reshape(ns, g, p), abi.reshape(ns, g, p),
            bbr.reshape(ns, g, p, GROUP_SIZE), bbi.reshape(ns, g, p, GROUP_SIZE))


SLAB_GROUPS = 16
SLAB = SLAB_GROUPS * GROUP_SIZE


def _s5_block_weights(bb_re, bb_im, c_re, c_im):
    g, p, gs = bb_re.shape
    s = g // SLAB_GROUPS
    eye = jnp.eye(SLAB_GROUPS, dtype=F32)

    def bexp(bb):
        x = bb.reshape(s, SLAB_GROUPS, p, gs)
        return jnp.einsum('sgpi,gh->sgihp', x, eye).reshape(s, SLAB, SLAB_GROUPS * p)

    def cexp(cc):
        x = cc.reshape(s, SLAB_GROUPS, gs, p)
        return jnp.einsum('sgop,gh->sgpho', x, eye).reshape(s, SLAB_GROUPS * p, SLAB)

    bmat = jnp.concatenate([bexp(bb_re), bexp(bb_im)], axis=2).astype(BF16)
    cmat = jnp.concatenate([cexp(c_re), cexp(-c_im)], axis=1).astype(BF16)
    return bmat, cmat


def _s5_pitch(rows):
    return rows + SUBLANES


def _s5_project_in(u, bmat_ref, z, *, rows, pitch, n_slab, tiles_half):
    tiles_slab = tiles_half // n_slab
    for s in range(n_slab):
        bu = jnp.dot(u[:, s * SLAB:(s + 1) * SLAB].astype(BF16), bmat_ref[s],
                     preferred_element_type=F32)
        for half in range(2):
            for q in range(tiles_slab):
                j = half * tiles_half + s * tiles_slab + q
                col = (half * tiles_slab + q) * LANES
                z[j * pitch:j * pitch + rows, :] = bu[:, col:col + LANES]


def _s5_project_out(u, z, cmat_ref, d_ref, o_ref, *, rows, pitch, n_slab, tiles_half):
    tiles_slab = tiles_half // n_slab
    for s in range(n_slab):
        parts = []
        for half in range(2):
            for q in range(tiles_slab):
                j = half * tiles_half + s * tiles_slab + q
                parts.append(z[j * pitch:j * pitch + rows, :].astype(BF16))
        st = jnp.concatenate(parts, axis=-1)
        y = jnp.dot(st, cmat_ref[s], preferred_element_type=F32)
        sl = slice(s * SLAB, (s + 1) * SLAB)
        y = y + d_ref[:, sl] * u[:, sl]
        o_ref[:, sl] = jax.nn.gelu(y).astype(o_ref.dtype)


def _s5_step(z, t, state, abar, *, pitch, tiles_half):
    nv = tiles_half // SUBLANES
    new = []
    for k in range(nv):
        idx_re = pl.ds(k * SUBLANES * pitch + t, SUBLANES, stride=pitch)
        idx_im = pl.ds((tiles_half + k * SUBLANES) * pitch + t, SUBLANES, stride=pitch)
        ar, ai = abar[k]
        sr, si = state[k]
        nr = ar * sr - ai * si + z[idx_re, :]
        ni = ar * si + ai * sr + z[idx_im, :]
        z[idx_re, :] = nr
        z[idx_im, :] = ni
        new.append((nr, ni))
    return tuple(new)


def _s5_p_body(u_ref, bmat_ref, cmat_ref, abr_ref, abi_ref, d_ref, o_ref, sre_ref, sim_ref,
               z, st_re, st_im, *, rows, tiles_per_seq, n_slab, tiles_half):
    m = pl.program_id(0)
    pitch = _s5_pitch(rows)
    nv = tiles_half // SUBLANES
    u = u_ref[...]
    _s5_project_in(u, bmat_ref, z, rows=rows, pitch=pitch, n_slab=n_slab, tiles_half=tiles_half)

    @pl.when(m % tiles_per_seq == 0)
    def _():
        st_re[...] = jnp.zeros(st_re.shape, F32)
        st_im[...] = jnp.zeros(st_im.shape, F32)

    vsl = lambda k: slice(k * SUBLANES, (k + 1) * SUBLANES)
    abar = tuple((abr_ref[vsl(k), :], abi_ref[vsl(k), :]) for k in range(nv))
    state0 = tuple((st_re[vsl(k), :], st_im[vsl(k), :]) for k in range(nv))

    def step(t, state):
        return _s5_step(z, t, state, abar, pitch=pitch, tiles_half=tiles_half)

    state = lax.fori_loop(0, rows, step, state0)
    for k in range(nv):
        st_re[vsl(k), :] = state[k][0]
        st_im[vsl(k), :] = state[k][1]
    sre_ref[...] = st_re[...]
    sim_ref[...] = st_im[...]
    _s5_project_out(u, z, cmat_ref, d_ref, o_ref, rows=rows, pitch=pitch, n_slab=n_slab,
                    tiles_half=tiles_half)


def _s5_prompt(u, bmat, cmat, abar_re, abar_im, dskip, *, rows, seq_len):
    m, d = u.shape
    n_slab = bmat.shape[0]
    tiles_half = abar_re.shape[0]
    pitch = _s5_pitch(rows)
    tiles_per_seq = seq_len // rows
    n_seq = m // seq_len
    st_spec = pl.BlockSpec((None, tiles_half, LANES), lambda i: (i // tiles_per_seq, 0, 0))
    full = lambda a: pl.BlockSpec(a.shape, lambda i: (0,) * a.ndim)
    return pl.pallas_call(
        functools.partial(_s5_p_body, rows=rows, tiles_per_seq=tiles_per_seq, n_slab=n_slab,
                          tiles_half=tiles_half),
        out_shape=[jax.ShapeDtypeStruct((m, d), BF16),
                   jax.ShapeDtypeStruct((n_seq, tiles_half, LANES), F32),
                   jax.ShapeDtypeStruct((n_seq, tiles_half, LANES), F32)],
        grid=(m // rows,),
        in_specs=[pl.BlockSpec((rows, d), lambda i: (i, 0)),
                  full(bmat), full(cmat), full(abar_re), full(abar_im), full(dskip)],
        out_specs=[pl.BlockSpec((rows, d), lambda i: (i, 0)), st_spec, st_spec],
        scratch_shapes=[pltpu.VMEM((2 * tiles_half * pitch, LANES), F32),
                        pltpu.VMEM((tiles_half, LANES), F32),
                        pltpu.VMEM((tiles_half, LANES), F32)],
        compiler_params=_cparams(1),
        name="s5_prompt",
    )(u, bmat, cmat, abar_re, abar_im, dskip)


def _s5_s_body(u_ref, h0r_ref, h0i_ref, bmat_ref, cmat_ref, abr_ref, abi_ref, d_ref,
               o_ref, sre_ref, sim_ref, z, zh, *, rows, seq, n_slab, tiles_half):
    pitch = _s5_pitch(rows)
    nb = rows // seq
    hp = _s5_pitch(nb)
    nv = tiles_half // SUBLANES
    u = u_ref[...]
    _s5_project_in(u, bmat_ref, z, rows=rows, pitch=pitch, n_slab=n_slab, tiles_half=tiles_half)
    for j in range(tiles_half):
        cs = slice(j * LANES, (j + 1) * LANES)
        zh[j * hp:j * hp + nb, :] = h0r_ref[:, cs]
        zh[(tiles_half + j) * hp:(tiles_half + j) * hp + nb, :] = h0i_ref[:, cs]

    vsl = lambda k: slice(k * SUBLANES, (k + 1) * SUBLANES)
    abar = tuple((abr_ref[vsl(k), :], abi_ref[vsl(k), :]) for k in range(nv))

    def one_seq(b, carry):
        state = []
        for k in range(nv):
            idx_re = pl.ds(k * SUBLANES * hp + b, SUBLANES, stride=hp)
            idx_im = pl.ds((tiles_half + k * SUBLANES) * hp + b, SUBLANES, stride=hp)
            state.append((zh[idx_re, :], zh[idx_im, :]))
        state = tuple(state)
        for t in range(seq):
            state = _s5_step(z, b * seq + t, state, abar, pitch=pitch, tiles_half=tiles_half)
        for k in range(nv):
            idx_re = pl.ds(k * SUBLANES * hp + b, SUBLANES, stride=hp)
            idx_im = pl.ds((tiles_half + k * SUBLANES) * hp + b, SUBLANES, stride=hp)
            zh[idx_re, :] = state[k][0]
            zh[idx_im, :] = state[k][1]
        return carry

    lax.fori_loop(0, nb, one_seq, 0)
    for j in range(tiles_half):
        cs = slice(j * LANES, (j + 1) * LANES)
        sre_ref[:, cs] = zh[j * hp:j * hp + nb, :]
        sim_ref[:, cs] = zh[(tiles_half + j) * hp:(tiles_half + j) * hp + nb, :]
    _s5_project_out(u, z, cmat_ref, d_ref, o_ref, rows=rows, pitch=pitch, n_slab=n_slab,
                    tiles_half=tiles_half)


def _s5_sample(u, h0_re, h0_im, bmat, cmat, abar_re, abar_im, dskip, *, rows, seq):
    m, d = u.shape
    nstate = h0_re.shape[1]
    n_slab = bmat.shape[0]
    tiles_half = abar_re.shape[0]
    pitch = _s5_pitch(rows)
    nb = rows // seq
    hp = _s5_pitch(nb)
    full = lambda a: pl.BlockSpec(a.shape, lambda i: (0,) * a.ndim)
    st_spec = pl.BlockSpec((nb, nstate), lambda i: (i, 0))
    return pl.pallas_call(
        functools.partial(_s5_s_body, rows=rows, seq=seq, n_slab=n_slab, tiles_half=tiles_half),
        out_shape=[jax.ShapeDtypeStruct((m, d), BF16),
                   jax.ShapeDtypeStruct(h0_re.shape, F32),
                   jax.ShapeDtypeStruct(h0_im.shape, F32)],
        grid=(m // rows,),
        in_specs=[pl.BlockSpec((rows, d), lambda i: (i, 0)), st_spec, st_spec,
                  full(bmat), full(cmat), full(abar_re), full(abar_im), full(dskip)],
        out_specs=[pl.BlockSpec((rows, d), lambda i: (i, 0)), st_spec, st_spec],
        scratch_shapes=[pltpu.VMEM((2 * tiles_half * pitch, LANES), F32),
                        pltpu.VMEM((2 * tiles_half * hp, LANES), F32)],
        compiler_params=_cparams(1),
        name="s5_sample",
    )(u, h0_re, h0_im, bmat, cmat, abar_re, abar_im, dskip)


def _run_chain(x, states, p, *, seq_len, bm, sample):
    m, d = x.shape
    n_seq = m // seq_len
    depth = p["norm_mix"].shape[0]
    conv_st, ssm_re_st, ssm_im_st, ffn_st = states
    new_conv, new_re, new_im, new_ffn = [], [], [], []
    nxt_dtype = lambda i: F32 if (i % N_MIXERS == 1) else BF16
    hn = _rmsnorm(x, p["norm_mix"][0], nxt_dtype(0), bm)
    y = None
    for i in range(depth):
        j = i // N_MIXERS
        if i % N_MIXERS == 0:
            (u,) = _mm(hn, p["conv_w_in"], j, glu=True, bm=bm, bn=512)
            u3 = u.reshape(n_seq, seq_len, d)
            cs = p["conv_dw"].shape[1] - 1
            if sample:
                ext = jnp.concatenate([conv_st[j], u3], axis=1)
                new_conv.append(ext[:, -cs:])
                c = _conv_sample(ext, p["conv_dw"], p["conv_dw_b"], p["conv_ln_g"],
                                 p["conv_ln_b"], j, nb=8, seq=seq_len)
            else:
                new_conv.append(u3[:, -cs:])
                c = _conv_prompt(u, p["conv_dw"], p["conv_dw_b"], p["conv_ln_g"],
                                 p["conv_ln_b"], j, bm=256, seq_len=seq_len)
            x, hn = _mm(c, p["conv_w_out"], j, glu=False, bm=bm, bn=512, resid=x,
                        gamma=p["norm_ffn"][i], hn_dtype=BF16)
        else:
            bmat, cmat = p["s5_bmat"][j], p["s5_cmat"][j]
            abr, abi = p["s5_abar_re"][j], p["s5_abar_im"][j]
            dsk = p["ssm_D"][j].reshape(1, d)
            if sample:
                rows = 128
                v, s_re, s_im = _s5_sample(hn, ssm_re_st[j].reshape(n_seq, -1),
                                           ssm_im_st[j].reshape(n_seq, -1),
                                           bmat, cmat, abr, abi, dsk, rows=rows, seq=seq_len)
            else:
                v, s_re, s_im = _s5_prompt(hn, bmat, cmat, abr, abi, dsk, rows=128,
                                           seq_len=seq_len)
            g, pdim = p["ssm_A_re"].shape[1:]
            new_re.append(s_re.reshape(n_seq, g, pdim))
            new_im.append(s_im.reshape(n_seq, g, pdim))
            x, hn = _mm(v, p["ssm_w_glu"], j, glu=True, bm=bm, bn=512, resid=x,
                        gamma=p["norm_ffn"][i], hn_dtype=BF16)
        f = p["ffn_w_gate"].shape[2]
        if sample:
            c0, c1 = ffn_st[i][:, 0], ffn_st[i][:, 1]
            zero = jnp.zeros_like(c0)
            fix1 = jnp.stack([c1] + [zero] * (seq_len - 1), axis=1).reshape(m, f)
            fix2 = jnp.stack([c0, c1] + [zero] * (seq_len - 2), axis=1).reshape(m, f)
            act, gproj = _ffn1(hn, p["ffn_w_gate"], p["ffn_w_up"], p["ffn_conv"], i, bm=bm,
                               bn=512, seq_len=seq_len, fix=(fix1, fix2))
            fs = p["ffn_conv"].shape[1] - 1
            new_ffn.append(gproj.reshape(n_seq, seq_len, f)[:, -fs:])
        else:
            act, tail = _ffn1(hn, p["ffn_w_gate"], p["ffn_w_up"], p["ffn_conv"], i, bm=bm,
                              bn=512, seq_len=seq_len)
            fs = p["ffn_conv"].shape[1] - 1
            new_ffn.append(tail[:, -fs:])
        last = i == depth - 1
        gamma = p["norm_final"] if last else p["norm_mix"][i + 1]
        outs = _mm(act, p["ffn_w_down"], i, glu=False, bm=min(bm, 512), bn=512, resid=x, gamma=gamma,
                   hn_dtype=F32 if last else nxt_dtype(i + 1), write_x=not last)
        if last:
            (y,) = outs
        else:
            x, hn = outs
    return (y, jnp.stack(new_conv), jnp.stack(new_re), jnp.stack(new_im), jnp.stack(new_ffn))


def kernel(x_prompt, x_sample, state_conv, state_ssm_re, state_ssm_im, state_ffn, norm_mix, norm_ffn, norm_final, conv_w_in, conv_dw, conv_dw_b, conv_ln_g, conv_ln_b, conv_w_out, ssm_A_re, ssm_A_im, ssm_log_dt, ssm_B_re, ssm_B_im, ssm_C_re, ssm_C_im, ssm_D, ssm_w_glu, ffn_w_gate, ffn_w_up, ffn_conv, ffn_w_down):
    bp, sp, d = x_prompt.shape
    bs, ss, _ = x_sample.shape
    abr, abi, bbr, bbi = _s5_discretise(ssm_A_re, ssm_A_im, ssm_log_dt, ssm_B_re, ssm_B_im)
    ns, g, pdim = ssm_A_re.shape
    blocks = [_s5_block_weights(bbr[j], bbi[j], ssm_C_re[j], ssm_C_im[j]) for j in range(ns)]
    tiles_half = g * pdim // LANES
    p = dict(
        norm_mix=norm_mix, norm_ffn=norm_ffn, norm_final=norm_final,
        conv_w_in=conv_w_in.astype(BF16), conv_dw=conv_dw, conv_dw_b=conv_dw_b,
        conv_ln_g=conv_ln_g, conv_ln_b=conv_ln_b, conv_w_out=conv_w_out.astype(BF16),
        ssm_A_re=ssm_A_re, ssm_D=ssm_D, ssm_w_glu=ssm_w_glu.astype(BF16),
        s5_bmat=[b[0] for b in blocks], s5_cmat=[b[1] for b in blocks],
        s5_abar_re=abr.reshape(ns, tiles_half, LANES), s5_abar_im=abi.reshape(ns, tiles_half, LANES),
        ffn_w_gate=ffn_w_gate.astype(BF16), ffn_w_up=ffn_w_up.astype(BF16),
        ffn_conv=ffn_conv, ffn_w_down=ffn_w_down.astype(BF16),
    )
    nc = conv_w_in.shape[0]
    depth = norm_mix.shape[0]
    zc = jnp.zeros((nc, bp, conv_dw.shape[1] - 1, d), x_prompt.dtype)
    zs = jnp.zeros((ns, bp, g, pdim), state_ssm_re.dtype)
    zf = jnp.zeros((depth, bp, ffn_conv.shape[1] - 1, ffn_w_gate.shape[2]), x_prompt.dtype)
    y_p, conv_p, re_p, im_p, ffn_p = _run_chain(
        x_prompt.reshape(bp * sp, d), (zc, zs, zs, zf), p, seq_len=sp, bm=1024, sample=False)
    y_s, conv_s, re_s, im_s, ffn_s = _run_chain(
        x_sample.reshape(bs * ss, d), (state_conv, state_ssm_re, state_ssm_im, state_ffn), p,
        seq_len=ss, bm=bs * ss, sample=True)
    return (y_p.reshape(bp, sp, d), y_s.reshape(bs, ss, d), conv_p, conv_s,
            re_p, im_p, re_s, im_s, ffn_p, ffn_s)
```

```python
import functools
import math

import jax
import jax.numpy as jnp
from jax import lax
from jax.experimental import pallas as pl
from jax.experimental.pallas import tpu as pltpu

F32 = jnp.float32
BF16 = jnp.bfloat16

EPS = 1e-6
N_MIXERS = 2
GROUP_SIZE = 16
LANES = 128
SUBLANES = 8
VMEM_LIMIT = 56 * 1024 * 1024
ROW_CHUNK = 256


def _cparams(n_axes):
    return pltpu.CompilerParams(
        dimension_semantics=("arbitrary",) * n_axes, vmem_limit_bytes=VMEM_LIMIT)


def _rms_scale(ss, n):
    return lax.rsqrt(ss / n + EPS)


def _rmsnorm_body(x_ref, g_ref, o_ref):
    x = x_ref[...]
    ss = jnp.sum(x * x, axis=-1, keepdims=True)
    o_ref[...] = ((x * _rms_scale(ss, x.shape[-1])) * g_ref[...]).astype(o_ref.dtype)


def _rmsnorm(x, g, out_dtype, bm):
    m, d = x.shape
    return pl.pallas_call(
        _rmsnorm_body,
        out_shape=jax.ShapeDtypeStruct((m, d), out_dtype),
        grid=(m // bm,),
        in_specs=[pl.BlockSpec((bm, d), lambda i: (i, 0)),
                  pl.BlockSpec((1, d), lambda i: (0, 0))],
        out_specs=pl.BlockSpec((bm, d), lambda i: (i, 0)),
        compiler_params=_cparams(1),
        name="rmsnorm",
    )(x, g.reshape(1, d))


def _mm_body(*refs, glu, has_res, write_x, nn, bn, n_out):
    it = iter(refs)
    lhs_ref = next(it)
    wa_ref = next(it)
    wb_ref = next(it) if glu else None
    res_ref = next(it) if has_res else None
    g_ref = next(it) if has_res else None
    xo_ref = next(it) if write_x else None
    hn_ref = next(it) if has_res else None
    rowbuf = next(it) if has_res else None

    ss_ref = next(it) if has_res else None

    n = pl.program_id(1)
    bm = lhs_ref.shape[0]
    wa = wa_ref[...]
    wb = wb_ref[...] if glu else None
    if has_res:
        @pl.when(n == 0)
        def _():
            ss_ref[...] = jnp.zeros(ss_ref.shape, F32)
    for r0 in range(0, bm, ROW_CHUNK):
        rs = slice(r0, min(r0 + ROW_CHUNK, bm))
        x = lhs_ref[rs, :]
        val = jnp.dot(x, wa, preferred_element_type=F32)
        if glu:
            gate = jnp.dot(x, wb, preferred_element_type=F32)
            val = val * jax.nn.sigmoid(gate)
        if has_res:
            val = res_ref[rs, :] + val
        if write_x:
            xo_ref[rs, :] = val
        if has_res:
            rowbuf[n, rs, :] = val
            ss_ref[rs, :] = ss_ref[rs, :] + jnp.sum(val * val, axis=-1, keepdims=True)

    if has_res:
        @pl.when(n == nn - 1)
        def _():
            scale = _rms_scale(ss_ref[...], n_out)
            for j in range(nn):
                sl = slice(j * bn, (j + 1) * bn)
                hn_ref[:, sl] = ((rowbuf[j] * scale) * g_ref[:, sl]).astype(hn_ref.dtype)


def _mm(lhs, w, layer, *, glu, bm, bn, resid=None, gamma=None, hn_dtype=None, write_x=True):
    m, k = lhs.shape
    n_out = w.shape[2] // (2 if glu else 1)
    nn = n_out // bn
    has_res = resid is not None
    in_specs = [pl.BlockSpec((bm, k), lambda i, j: (i, 0)),
                pl.BlockSpec((None, k, bn), lambda i, j: (layer, 0, j))]
    args = [lhs, w]
    if glu:
        in_specs.append(pl.BlockSpec((None, k, bn), lambda i, j: (layer, 0, j + nn)))
        args.append(w)
    out_shape, out_specs, scratch = [], [], []
    if has_res:
        in_specs += [pl.BlockSpec((bm, bn), lambda i, j: (i, j)),
                     pl.BlockSpec((1, n_out), lambda i, j: (0, 0))]
        args += [resid, gamma.reshape(1, n_out)]
    if write_x:
        out_shape.append(jax.ShapeDtypeStruct((m, n_out), F32))
        out_specs.append(pl.BlockSpec((bm, bn), lambda i, j: (i, j)))
    if has_res:
        out_shape.append(jax.ShapeDtypeStruct((m, n_out), hn_dtype))
        out_specs.append(pl.BlockSpec((bm, n_out), lambda i, j: (i, 0)))
        scratch += [pltpu.VMEM((nn, bm, bn), F32), pltpu.VMEM((bm, 1), F32)]
    outs = pl.pallas_call(
        functools.partial(_mm_body, glu=glu, has_res=has_res, write_x=write_x,
                          nn=nn, bn=bn, n_out=n_out),
        out_shape=out_shape,
        grid=(m // bm, nn),
        in_specs=in_specs,
        out_specs=out_specs,
        scratch_shapes=scratch,
        compiler_params=_cparams(2),
        name="mm_glu" if glu else "mm_lin",
    )(*args)
    return outs


def _shift_rows(g, tail, k):
    r = pltpu.roll(g, k, 0)
    t = pltpu.roll(tail, k, 0)
    i8 = lax.broadcasted_iota(jnp.int32, tail.shape, 0)
    head = jnp.where(i8 < k, t, r[0:SUBLANES, :])
    return jnp.concatenate([head, r[SUBLANES:, :]], axis=0)


def _ffn1_body(*refs, sample, bm, tiles_per_seq, seq_len):
    if sample:
        _ffn1_sample_body(*refs, bm=bm, seq_len=seq_len)
        return
    h_ref, wg_ref, wu_ref, cw_ref, act_ref, tail_ref, carry = refs
    m = pl.program_id(0)
    n = pl.program_id(1)
    wg = wg_ref[...]
    wu = wu_ref[...]
    cw = cw_ref[...]

    @pl.when(m % tiles_per_seq == 0)
    def _():
        carry[n] = jnp.zeros(carry.shape[1:], F32)

    tail = carry[n]
    for r0 in range(0, bm, ROW_CHUNK):
        rs = slice(r0, r0 + ROW_CHUNK)
        h = h_ref[rs, :]
        g = jnp.dot(h, wg, preferred_element_type=F32)
        up = jnp.dot(h, wu, preferred_element_type=F32)
        gc = (_shift_rows(g, tail, 2) * cw[0:1, :] + _shift_rows(g, tail, 1) * cw[1:2, :]
              + g * cw[2:3, :])
        act_ref[rs, :] = (jax.nn.silu(gc) * up).astype(act_ref.dtype)
        tail = g[ROW_CHUNK - SUBLANES:, :]
    carry[n] = tail
    tail_ref[...] = tail


def _ffn1_sample_body(h_ref, wg_ref, wu_ref, cw_ref, fix1_ref, fix2_ref, act_ref, g_ref, gbuf, *,
                      bm, seq_len):
    h = h_ref[...]
    g = jnp.dot(h, wg_ref[...], preferred_element_type=F32)
    up = jnp.dot(h, wu_ref[...], preferred_element_type=F32)
    gbuf[SUBLANES:SUBLANES + bm, :] = g
    gbuf[0:SUBLANES, :] = jnp.zeros((SUBLANES, g.shape[1]), F32)
    g_ref[...] = g
    t = lax.broadcasted_iota(jnp.int32, g.shape, 0) % seq_len
    gp1 = jnp.where(t >= 1, gbuf[SUBLANES - 1:SUBLANES - 1 + bm, :], fix1_ref[...])
    gp2 = jnp.where(t >= 2, gbuf[SUBLANES - 2:SUBLANES - 2 + bm, :], fix2_ref[...])
    cw = cw_ref[...]
    gc = gp2 * cw[0:1, :] + gp1 * cw[1:2, :] + g * cw[2:3, :]
    act_ref[...] = (jax.nn.silu(gc) * up).astype(act_ref.dtype)


def _ffn1(h, wg, wu, cw, layer, *, bm, bn, seq_len, fix=None):
    m, k = h.shape
    f = wg.shape[2]
    nn = f // bn
    sample = fix is not None
    tiles_per_seq = max(seq_len // bm, 1)
    in_specs = [pl.BlockSpec((bm, k), lambda i, j: (i, 0)),
                pl.BlockSpec((None, k, bn), lambda i, j: (layer, 0, j)),
                pl.BlockSpec((None, k, bn), lambda i, j: (layer, 0, j)),
                pl.BlockSpec((None, cw.shape[1], bn), lambda i, j: (layer, 0, j))]
    args = [h, wg, wu, cw]
    out_shape = [jax.ShapeDtypeStruct((m, f), BF16)]
    out_specs = [pl.BlockSpec((bm, bn), lambda i, j: (i, j))]
    if sample:
        in_specs += [pl.BlockSpec((bm, bn), lambda i, j: (i, j))] * 2
        args += list(fix)
        out_shape.append(jax.ShapeDtypeStruct((m, f), F32))
        out_specs.append(pl.BlockSpec((bm, bn), lambda i, j: (i, j)))
        scratch = [pltpu.VMEM((SUBLANES + bm, bn), F32)]
    else:
        out_shape.append(jax.ShapeDtypeStruct((m // bm, SUBLANES, f), F32))
        out_specs.append(pl.BlockSpec((None, SUBLANES, bn), lambda i, j: (i, 0, j)))
        scratch = [pltpu.VMEM((nn, SUBLANES, bn), F32)]
    outs = pl.pallas_call(
        functools.partial(_ffn1_body, sample=sample, bm=bm, tiles_per_seq=tiles_per_seq,
                          seq_len=seq_len),
        out_shape=out_shape,
        grid=(m // bm, nn),
        in_specs=in_specs,
        out_specs=out_specs,
        scratch_shapes=scratch,
        compiler_params=_cparams(2),
        name="ffn_gate_up",
    )(*args)
    if sample:
        return outs
    act, tails = outs
    return act, tails[tiles_per_seq - 1::tiles_per_seq]


CONV_ROWS = 32
HALO = 32


def _ln_silu(c, g, b):
    mu = jnp.mean(c, axis=-1, keepdims=True)
    d = c - mu
    var = jnp.mean(d * d, axis=-1, keepdims=True)
    y = d * lax.rsqrt(var + EPS)
    return jax.nn.silu(y * g + b)


def _conv_p_body(u_ref, dw_ref, db_ref, lg_ref, lb_ref, o_ref, ubuf, cbuf, *, bm, taps,
                 tiles_per_seq):
    m = pl.program_id(0)
    d = u_ref.shape[1]
    nc = d // LANES

    @pl.when(m % tiles_per_seq == 0)
    def _():
        ubuf[:, 0:HALO, :] = jnp.zeros((nc, HALO, LANES), F32)

    @pl.when(m % tiles_per_seq != 0)
    def _():
        ubuf[:, 0:HALO, :] = ubuf[:, bm:bm + HALO, :]

    for c in range(nc):
        ubuf[c, HALO:HALO + bm, :] = u_ref[:, c * LANES:(c + 1) * LANES]
    off = HALO - (taps - 1)

    def lane_chunk(c, carry):
        wv = dw_ref[c]
        bias = db_ref[c]
        for r0 in range(0, bm, CONV_ROWS):
            acc = jnp.zeros((CONV_ROWS, LANES), F32)
            for k in range(taps):
                acc = acc + ubuf[c, r0 + off + k:r0 + off + k + CONV_ROWS, :] * wv[k:k + 1, :]
            cbuf[c, r0:r0 + CONV_ROWS, :] = acc + bias
        return carry

    lax.fori_loop(0, nc, lane_chunk, 0)
    cfull = jnp.concatenate([cbuf[c] for c in range(nc)], axis=-1)
    o_ref[...] = _ln_silu(cfull, lg_ref[...], lb_ref[...]).astype(o_ref.dtype)


def _lane_chunked(x):
    l, r, d = x.shape
    return x.reshape(l, r, d // LANES, LANES).transpose(0, 2, 1, 3)


def _conv_prompt(u, dw, db, lg, lb, layer, *, bm, seq_len):
    m, d = u.shape
    taps = dw.shape[1]
    nc = d // LANES
    vec = lambda: pl.BlockSpec((None, 1, d), lambda i: (layer, 0, 0))
    return pl.pallas_call(
        functools.partial(_conv_p_body, bm=bm, taps=taps, tiles_per_seq=seq_len // bm),
        out_shape=jax.ShapeDtypeStruct((m, d), BF16),
        grid=(m // bm,),
        in_specs=[pl.BlockSpec((bm, d), lambda i: (i, 0)),
                  pl.BlockSpec((None, nc, taps, LANES), lambda i: (layer, 0, 0, 0)),
                  pl.BlockSpec((None, nc, 1, LANES), lambda i: (layer, 0, 0, 0)),
                  vec(), vec()],
        out_specs=pl.BlockSpec((bm, d), lambda i: (i, 0)),
        scratch_shapes=[pltpu.VMEM((nc, HALO + bm, LANES), F32),
                        pltpu.VMEM((nc, bm, LANES), F32)],
        compiler_params=_cparams(1),
        name="conv_prompt",
    )(u, _lane_chunked(dw), _lane_chunked(db.reshape(db.shape[0], 1, d)),
      lg.reshape(lg.shape[0], 1, d), lb.reshape(lb.shape[0], 1, d))


def _conv_s_body(ext_ref, dw_ref, db_ref, lg_ref, lb_ref, o_ref, cbuf, *, nb, taps, seq):
    d = dw_ref.shape[1]
    for c in range(d // LANES):
        cs = slice(c * LANES, (c + 1) * LANES)
        wv = dw_ref[:, cs]
        bias = db_ref[:, cs]
        for b in range(nb):
            acc = jnp.zeros((seq, LANES), F32)
            for k in range(taps):
                acc = acc + ext_ref[b, k:k + seq, cs] * wv[k:k + 1, :]
            cbuf[b * seq:(b + 1) * seq, cs] = acc + bias
    o_ref[...] = _ln_silu(cbuf[...], lg_ref[...], lb_ref[...]).astype(o_ref.dtype)


def _conv_sample(ext, dw, db, lg, lb, layer, *, nb, seq):
    b, rows, d = ext.shape
    taps = dw.shape[1]
    vec = lambda: pl.BlockSpec((None, 1, d), lambda i: (layer, 0, 0))
    return pl.pallas_call(
        functools.partial(_conv_s_body, nb=nb, taps=taps, seq=seq),
        out_shape=jax.ShapeDtypeStruct((b * seq, d), BF16),
        grid=(b // nb,),
        in_specs=[pl.BlockSpec((nb, rows, d), lambda i: (i, 0, 0)),
                  pl.BlockSpec((None, taps, d), lambda i: (layer, 0, 0)),
                  vec(), vec(), vec()],
        out_specs=pl.BlockSpec((nb * seq, d), lambda i: (i, 0)),
        scratch_shapes=[pltpu.VMEM((nb * seq, d), F32)],
        compiler_params=_cparams(1),
        name="conv_sample",
    )(ext, dw, db.reshape(db.shape[0], 1, d), lg.reshape(lg.shape[0], 1, d),
      lb.reshape(lb.shape[0], 1, d))


def _s5_abar(lam_re, lam_im, dt):
    mag = jnp.exp(lam_re * dt)
    ang = lam_im * dt
    return mag * jnp.cos(ang), mag * jnp.sin(ang)


def _s5_disc_body(are_ref, aim_ref, ldt_ref, arep_ref, airep_ref, bre_ref, bim_ref,
                  abr_ref, abi_ref, bbr_ref, bbi_ref):
    dt = jnp.exp(ldt_ref[...])
    abar_re, abar_im = _s5_abar(are_ref[...], aim_ref[...], dt)
    abr_ref[...] = abar_re
    abi_ref[...] = abar_im
    lam_re = arep_ref[...]
    lam_im = airep_ref[...]
    rep_re, rep_im = _s5_abar(lam_re, lam_im, dt)
    nr = rep_re - 1.0
    ni = rep_im
    den = lam_re * lam_re + lam_im * lam_im
    q_re = (nr * lam_re + ni * lam_im) / den
    q_im = (ni * lam_re - nr * lam_im) / den
    br = bre_ref[...]
    bi = bim_ref[...]
    bbr_ref[...] = q_re * br - q_im * bi
    bbi_ref[...] = q_re * bi + q_im * br


def _s5_discretise(a_re, a_im, log_dt, b_re, b_im):
    ns, g, p = a_re.shape
    rows = ns * g
    two = lambda x: x.reshape(rows, p)
    rep = lambda x: jnp.repeat(x.reshape(rows, p), GROUP_SIZE, axis=1)
    wide = lambda x: x.reshape(rows, p * GROUP_SIZE)
    outs = pl.pallas_call(
        _s5_disc_body,
        out_shape=[jax.ShapeDtypeStruct((rows, p), F32)] * 2
        + [jax.ShapeDtypeStruct((rows, p * GROUP_SIZE), F32)] * 2,
        name="s5_discretise",
    )(two(a_re), two(a_im), log_dt.reshape(rows, 1), rep(a_re), rep(a_im), wide(b_re), wide(b_im))
    abr, abi, bbr, bbi = outs
    return (abr.reshape(ns, g, p), abi.reshape(ns, g, p),
            bbr.reshape(ns, g, p, GROUP_SIZE), bbi.reshape(ns, g, p, GROUP_SIZE))


SLAB_GROUPS = 16
SLAB = SLAB_GROUPS * GROUP_SIZE


def _s5_block_weights(bb_re, bb_im, c_re, c_im):
    g, p, gs = bb_re.shape
    s = g // SLAB_GROUPS
    eye = jnp.eye(SLAB_GROUPS, dtype=F32)

    def bexp(bb):
        x = bb.reshape(s, SLAB_GROUPS, p, gs)
        return jnp.einsum('sgpi,gh->sgihp', x, eye).reshape(s, SLAB, SLAB_GROUPS * p)

    def cexp(cc):
        x = cc.reshape(s, SLAB_GROUPS, gs, p)
        return jnp.einsum('sgop,gh->sgpho', x, eye).reshape(s, SLAB_GROUPS * p, SLAB)

    bmat = jnp.concatenate([bexp(bb_re), bexp(bb_im)], axis=2).astype(BF16)
    cmat = jnp.concatenate([cexp(c_re), cexp(-c_im)], axis=1).astype(BF16)
    return bmat, cmat


def _s5_pitch(rows):
    return rows + SUBLANES // 2


def _s5_project_in(u, bmat_ref, z, *, rows, pitch, n_slab, tiles_half):
    tiles_slab = tiles_half // n_slab
    for s in range(n_slab):
        bu = jnp.dot(u[:, s * SLAB:(s + 1) * SLAB].astype(BF16), bmat_ref[s],
                     preferred_element_type=F32)
        for half in range(2):
            for q in range(tiles_slab):
                j = half * tiles_half + s * tiles_slab + q
                col = (half * tiles_slab + q) * LANES
                z[j * pitch:j * pitch + rows, :] = bu[:, col:col + LANES]


def _s5_project_out(u, z, cmat_ref, d_ref, o_ref, *, rows, pitch, n_slab, tiles_half):
    tiles_slab = tiles_half // n_slab
    for s in range(n_slab):
        parts = []
        for half in range(2):
            for q in range(tiles_slab):
                j = half * tiles_half + s * tiles_slab + q
                parts.append(z[j * pitch:j * pitch + rows, :].astype(BF16))
        st = jnp.concatenate(parts, axis=-1)
        y = jnp.dot(st, cmat_ref[s], preferred_element_type=F32)
        sl = slice(s * SLAB, (s + 1) * SLAB)
        y = y + d_ref[:, sl] * u[:, sl]
        o_ref[:, sl] = jax.nn.gelu(y).astype(o_ref.dtype)


def _s5_step(z, t, state, abar, *, pitch, tiles_half):
    nv = tiles_half // SUBLANES
    new = []
    for k in range(nv):
        idx_re = pl.ds(k * SUBLANES * pitch + t, SUBLANES, stride=pitch)
        idx_im = pl.ds((tiles_half + k * SUBLANES) * pitch + t, SUBLANES, stride=pitch)
        ar, ai = abar[k]
        sr, si = state[k]
        nr = ar * sr - ai * si + z[idx_re, :]
        ni = ar * si + ai * sr + z[idx_im, :]
        z[idx_re, :] = nr
        z[idx_im, :] = ni
        new.append((nr, ni))
    return tuple(new)


def _s5_p_body(u_ref, bmat_ref, cmat_ref, abr_ref, abi_ref, d_ref, o_ref, sre_ref, sim_ref,
               z, st_re, st_im, *, rows, tiles_per_seq, n_slab, tiles_half):
    m = pl.program_id(0)
    pitch = _s5_pitch(rows)
    nv = tiles_half // SUBLANES
    u = u_ref[...]
    _s5_project_in(u, bmat_ref, z, rows=rows, pitch=pitch, n_slab=n_slab, tiles_half=tiles_half)

    @pl.when(m % tiles_per_seq == 0)
    def _():
        st_re[...] = jnp.zeros(st_re.shape, F32)
        st_im[...] = jnp.zeros(st_im.shape, F32)

    vsl = lambda k: slice(k * SUBLANES, (k + 1) * SUBLANES)
    abar = tuple((abr_ref[vsl(k), :], abi_ref[vsl(k), :]) for k in range(nv))
    state0 = tuple((st_re[vsl(k), :], st_im[vsl(k), :]) for k in range(nv))

    def step(t, state):
        return _s5_step(z, t, state, abar, pitch=pitch, tiles_half=tiles_half)

    state = lax.fori_loop(0, rows, step, state0)
    for k in range(nv):
        st_re[vsl(k), :] = state[k][0]
        st_im[vsl(k), :] = state[k][1]
    sre_ref[...] = st_re[...]
    sim_ref[...] = st_im[...]
    _s5_project_out(u, z, cmat_ref, d_ref, o_ref, rows=rows, pitch=pitch, n_slab=n_slab,
                    tiles_half=tiles_half)


def _s5_prompt(u, bmat, cmat, abar_re, abar_im, dskip, *, rows, seq_len):
    m, d = u.shape
    n_slab = bmat.shape[0]
    tiles_half = abar_re.shape[0]
    pitch = _s5_pitch(rows)
    tiles_per_seq = seq_len // rows
    n_seq = m // seq_len
    st_spec = pl.BlockSpec((None, tiles_half, LANES), lambda i: (i // tiles_per_seq, 0, 0))
    full = lambda a: pl.BlockSpec(a.shape, lambda i: (0,) * a.ndim)
    return pl.pallas_call(
        functools.partial(_s5_p_body, rows=rows, tiles_per_seq=tiles_per_seq, n_slab=n_slab,
                          tiles_half=tiles_half),
        out_shape=[jax.ShapeDtypeStruct((m, d), BF16),
                   jax.ShapeDtypeStruct((n_seq, tiles_half, LANES), F32),
                   jax.ShapeDtypeStruct((n_seq, tiles_half, LANES), F32)],
        grid=(m // rows,),
        in_specs=[pl.BlockSpec((rows, d), lambda i: (i, 0)),
                  full(bmat), full(cmat), full(abar_re), full(abar_im), full(dskip)],
        out_specs=[pl.BlockSpec((rows, d), lambda i: (i, 0)), st_spec, st_spec],
        scratch_shapes=[pltpu.VMEM((2 * tiles_half * pitch, LANES), F32),
                        pltpu.VMEM((tiles_half, LANES), F32),
                        pltpu.VMEM((tiles_half, LANES), F32)],
        compiler_params=_cparams(1),
        name="s5_prompt",
    )(u, bmat, cmat, abar_re, abar_im, dskip)


def _s5_s_body(u_ref, h0r_ref, h0i_ref, bmat_ref, cmat_ref, abr_ref, abi_ref, d_ref,
               o_ref, sre_ref, sim_ref, z, zh, *, rows, seq, n_slab, tiles_half):
    pitch = _s5_pitch(rows)
    nb = rows // seq
    hp = _s5_pitch(nb)
    nv = tiles_half // SUBLANES
    u = u_ref[...]
    _s5_project_in(u, bmat_ref, z, rows=rows, pitch=pitch, n_slab=n_slab, tiles_half=tiles_half)
    for j in range(tiles_half):
        cs = slice(j * LANES, (j + 1) * LANES)
        zh[j * hp:j * hp + nb, :] = h0r_ref[:, cs]
        zh[(tiles_half + j) * hp:(tiles_half + j) * hp + nb, :] = h0i_ref[:, cs]

    vsl = lambda k: slice(k * SUBLANES, (k + 1) * SUBLANES)
    abar = tuple((abr_ref[vsl(k), :], abi_ref[vsl(k), :]) for k in range(nv))

    def one_seq(b, carry):
        state = []
        for k in range(nv):
            idx_re = pl.ds(k * SUBLANES * hp + b, SUBLANES, stride=hp)
            idx_im = pl.ds((tiles_half + k * SUBLANES) * hp + b, SUBLANES, stride=hp)
            state.append((zh[idx_re, :], zh[idx_im, :]))
        state = tuple(state)
        for t in range(seq):
            state = _s5_step(z, b * seq + t, state, abar, pitch=pitch, tiles_half=tiles_half)
        for k in range(nv):
            idx_re = pl.ds(k * SUBLANES * hp + b, SUBLANES, stride=hp)
            idx_im = pl.ds((tiles_half + k * SUBLANES) * hp + b, SUBLANES, stride=hp)
            zh[idx_re, :] = state[k][0]
            zh[idx_im, :] = state[k][1]
        return carry

    lax.fori_loop(0, nb, one_seq, 0)
    for j in range(tiles_half):
        cs = slice(j * LANES, (j + 1) * LANES)
        sre_ref[:, cs] = zh[j * hp:j * hp + nb, :]
        sim_ref[:, cs] = zh[(tiles_half + j) * hp:(tiles_half + j) * hp + nb, :]
    _s5_project_out(u, z, cmat_ref, d_ref, o_ref, rows=rows, pitch=pitch, n_slab=n_slab,
                    tiles_half=tiles_half)


def _s5_sample(u, h0_re, h0_im, bmat, cmat, abar_re, abar_im, dskip, *, rows, seq):
    m, d = u.shape
    nstate = h0_re.shape[1]
    n_slab = bmat.shape[0]
    tiles_half = abar_re.shape[0]
    pitch = _s5_pitch(rows)
    nb = rows // seq
    hp = _s5_pitch(nb)
    full = lambda a: pl.BlockSpec(a.shape, lambda i: (0,) * a.ndim)
    st_spec = pl.BlockSpec((nb, nstate), lambda i: (i, 0))
    return pl.pallas_call(
        functools.partial(_s5_s_body, rows=rows, seq=seq, n_slab=n_slab, tiles_half=tiles_half),
        out_shape=[jax.ShapeDtypeStruct((m, d), BF16),
                   jax.ShapeDtypeStruct(h0_re.shape, F32),
                   jax.ShapeDtypeStruct(h0_im.shape, F32)],
        grid=(m // rows,),
        in_specs=[pl.BlockSpec((rows, d), lambda i: (i, 0)), st_spec, st_spec,
                  full(bmat), full(cmat), full(abar_re), full(abar_im), full(dskip)],
        out_specs=[pl.BlockSpec((rows, d), lambda i: (i, 0)), st_spec, st_spec],
        scratch_shapes=[pltpu.VMEM((2 * tiles_half * pitch, LANES), F32),
                        pltpu.VMEM((2 * tiles_half * hp, LANES), F32)],
        compiler_params=_cparams(1),
        name="s5_sample",
    )(u, h0_re, h0_im, bmat, cmat, abar_re, abar_im, dskip)


def _run_chain(x, states, p, *, seq_len, bm, sample):
    m, d = x.shape
    n_seq = m // seq_len
    depth = p["norm_mix"].shape[0]
    conv_st, ssm_re_st, ssm_im_st, ffn_st = states
    new_conv, new_re, new_im, new_ffn = [], [], [], []
    nxt_dtype = lambda i: F32 if (i % N_MIXERS == 1) else BF16
    hn = _rmsnorm(x, p["norm_mix"][0], nxt_dtype(0), bm)
    y = None
    for i in range(depth):
        j = i // N_MIXERS
        if i % N_MIXERS == 0:
            (u,) = _mm(hn, p["conv_w_in"], j, glu=True, bm=bm, bn=512)
            u3 = u.reshape(n_seq, seq_len, d)
            cs = p["conv_dw"].shape[1] - 1
            if sample:
                ext = jnp.concatenate([conv_st[j], u3], axis=1)
                new_conv.append(ext[:, -cs:])
                c = _conv_sample(ext, p["conv_dw"], p["conv_dw_b"], p["conv_ln_g"],
                                 p["conv_ln_b"], j, nb=8, seq=seq_len)
            else:
                new_conv.append(u3[:, -cs:])
                c = _conv_prompt(u, p["conv_dw"], p["conv_dw_b"], p["conv_ln_g"],
                                 p["conv_ln_b"], j, bm=256, seq_len=seq_len)
            x, hn = _mm(c, p["conv_w_out"], j, glu=False, bm=bm, bn=512, resid=x,
                        gamma=p["norm_ffn"][i], hn_dtype=BF16)
        else:
            bmat, cmat = p["s5_bmat"][j], p["s5_cmat"][j]
            abr, abi = p["s5_abar_re"][j], p["s5_abar_im"][j]
            dsk = p["ssm_D"][j].reshape(1, d)
            if sample:
                rows = 128
                v, s_re, s_im = _s5_sample(hn, ssm_re_st[j].reshape(n_seq, -1),
                                           ssm_im_st[j].reshape(n_seq, -1),
                                           bmat, cmat, abr, abi, dsk, rows=rows, seq=seq_len)
            else:
                v, s_re, s_im = _s5_prompt(hn, bmat, cmat, abr, abi, dsk, rows=256,
                                           seq_len=seq_len)
            g, pdim = p["ssm_A_re"].shape[1:]
            new_re.append(s_re.reshape(n_seq, g, pdim))
            new_im.append(s_im.reshape(n_seq, g, pdim))
            x, hn = _mm(v, p["ssm_w_glu"], j, glu=True, bm=bm, bn=512, resid=x,
                        gamma=p["norm_ffn"][i], hn_dtype=BF16)
        f = p["ffn_w_gate"].shape[2]
        if sample:
            c0, c1 = ffn_st[i][:, 0], ffn_st[i][:, 1]
            zero = jnp.zeros_like(c0)
            fix1 = jnp.stack([c1] + [zero] * (seq_len - 1), axis=1).reshape(m, f)
            fix2 = jnp.stack([c0, c1] + [zero] * (seq_len - 2), axis=1).reshape(m, f)
            act, gproj = _ffn1(hn, p["ffn_w_gate"], p["ffn_w_up"], p["ffn_conv"], i, bm=bm,
                               bn=512, seq_len=seq_len, fix=(fix1, fix2))
            fs = p["ffn_conv"].shape[1] - 1
            new_ffn.append(gproj.reshape(n_seq, seq_len, f)[:, -fs:])
        else:
            act, tail = _ffn1(hn, p["ffn_w_gate"], p["ffn_w_up"], p["ffn_conv"], i, bm=bm,
                              bn=512, seq_len=seq_len)
            fs = p["ffn_conv"].shape[1] - 1
            new_ffn.append(tail[:, -fs:])
        last = i == depth - 1
        gamma = p["norm_final"] if last else p["norm_mix"][i + 1]
        outs = _mm(act, p["ffn_w_down"], i, glu=False, bm=min(bm, 512), bn=512, resid=x, gamma=gamma,
                   hn_dtype=F32 if last else nxt_dtype(i + 1), write_x=not last)
        if last:
            (y,) = outs
        else:
            x, hn = outs
    return (y, jnp.stack(new_conv), jnp.stack(new_re), jnp.stack(new_im), jnp.stack(new_ffn))


def kernel(x_prompt, x_sample, state_conv, state_ssm_re, state_ssm_im, state_ffn, norm_mix, norm_ffn, norm_final, conv_w_in, conv_dw, conv_dw_b, conv_ln_g, conv_ln_b, conv_w_out, ssm_A_re, ssm_A_im, ssm_log_dt, ssm_B_re, ssm_B_im, ssm_C_re, ssm_C_im, ssm_D, ssm_w_glu, ffn_w_gate, ffn_w_up, ffn_conv, ffn_w_down):
    bp, sp, d = x_prompt.shape
    bs, ss, _ = x_sample.shape
    abr, abi, bbr, bbi = _s5_discretise(ssm_A_re, ssm_A_im, ssm_log_dt, ssm_B_re, ssm_B_im)
    ns, g, pdim = ssm_A_re.shape
    blocks = [_s5_block_weights(bbr[j], bbi[j], ssm_C_re[j], ssm_C_im[j]) for j in range(ns)]
    tiles_half = g * pdim // LANES
    p = dict(
        norm_mix=norm_mix, norm_ffn=norm_ffn, norm_final=norm_final,
        conv_w_in=conv_w_in.astype(BF16), conv_dw=conv_dw, conv_dw_b=conv_dw_b,
        conv_ln_g=conv_ln_g, conv_ln_b=conv_ln_b, conv_w_out=conv_w_out.astype(BF16),
        ssm_A_re=ssm_A_re, ssm_D=ssm_D, ssm_w_glu=ssm_w_glu.astype(BF16),
        s5_bmat=[b[0] for b in blocks], s5_cmat=[b[1] for b in blocks],
        s5_abar_re=abr.reshape(ns, tiles_half, LANES), s5_abar_im=abi.reshape(ns, tiles_half, LANES),
        ffn_w_gate=ffn_w_gate.astype(BF16), ffn_w_up=ffn_w_up.astype(BF16),
        ffn_conv=ffn_conv, ffn_w_down=ffn_w_down.astype(BF16),
    )
    nc = conv_w_in.shape[0]
    depth = norm_mix.shape[0]
    zc = jnp.zeros((nc, bp, conv_dw.shape[1] - 1, d), x_prompt.dtype)
    zs = jnp.zeros((ns, bp, g, pdim), state_ssm_re.dtype)
    zf = jnp.zeros((depth, bp, ffn_conv.shape[1] - 1, ffn_w_gate.shape[2]), x_prompt.dtype)
    y_p, conv_p, re_p, im_p, ffn_p = _run_chain(
        x_prompt.reshape(bp * sp, d), (zc, zs, zs, zf), p, seq_len=sp, bm=1024, sample=False)
    y_s, conv_s, re_s, im_s, ffn_s = _run_chain(
        x_sample.reshape(bs * ss, d), (state_conv, state_ssm_re, state_ssm_im, state_ffn), p,
        seq_len=ss, bm=bs * ss, sample=True)
    return (y_p.reshape(bp, sp, d), y_s.reshape(bs, ss, d), conv_p, conv_s,
            re_p, im_p, re_s, im_s, ffn_p, ffn_s)
```

```python
import functools

import jax
import jax.numpy as jnp
from jax import lax
from jax.experimental import pallas as pl
from jax.experimental.pallas import tpu as pltpu

F32 = jnp.float32
BF16 = jnp.bfloat16

EPS = 1e-6
N_MIXERS = 2
GROUP_SIZE = 16
LANES = 128
SUBLANES = 8
VMEM_LIMIT = 56 * 1024 * 1024
ROW_CHUNK = 256


def _cparams(n_axes):
    return pltpu.CompilerParams(
        dimension_semantics=("arbitrary",) * n_axes, vmem_limit_bytes=VMEM_LIMIT)


def _rms_scale(ss, n):
    return lax.rsqrt(ss / n + EPS)


def _rmsnorm_body(x_ref, g_ref, o_ref):
    x = x_ref[...]
    ss = jnp.sum(x * x, axis=-1, keepdims=True)
    o_ref[...] = ((x * _rms_scale(ss, x.shape[-1])) * g_ref[...]).astype(o_ref.dtype)


def _rmsnorm(x, g, out_dtype, bm):
    m, d = x.shape
    return pl.pallas_call(
        _rmsnorm_body,
        out_shape=jax.ShapeDtypeStruct((m, d), out_dtype),
        grid=(m // bm,),
        in_specs=[pl.BlockSpec((bm, d), lambda i: (i, 0)),
                  pl.BlockSpec((1, d), lambda i: (0, 0))],
        out_specs=pl.BlockSpec((bm, d), lambda i: (i, 0)),
        compiler_params=_cparams(1),
        name="rmsnorm",
    )(x, g.reshape(1, d))


def _mm_body(*refs, glu, has_res, write_x, emit, nn, bn, n_out):
    it = iter(refs)
    lhs_ref = next(it)
    wa_ref = next(it)
    wb_ref = next(it) if glu else None
    res_ref = next(it) if has_res else None
    g_ref = next(it) if has_res else None
    xo_ref = next(it) if write_x else None
    hn_ref = next(it) if has_res else None
    wao_ref = next(it) if emit else None
    wbo_ref = next(it) if (emit and glu) else None
    rowbuf = next(it) if has_res else None
    ss_ref = next(it) if has_res else None

    n = pl.program_id(1)
    bm = lhs_ref.shape[0]
    if emit:
        wao_ref[...] = wa_ref[...].astype(BF16)
        wa_ref = wao_ref
        if glu:
            wbo_ref[...] = wb_ref[...].astype(BF16)
            wb_ref = wbo_ref
    if has_res:
        @pl.when(n == 0)
        def _():
            ss_ref[...] = jnp.zeros(ss_ref.shape, F32)
    for r0 in range(0, bm, ROW_CHUNK):
        rs = slice(r0, min(r0 + ROW_CHUNK, bm))
        x = lhs_ref[rs, :]
        val = jnp.dot(x, wa_ref[...], preferred_element_type=F32)
        if glu:
            gate = jnp.dot(x, wb_ref[...], preferred_element_type=F32)
            val = val * jax.nn.sigmoid(gate)
        if has_res:
            val = res_ref[rs, :] + val
        if write_x:
            xo_ref[rs, :] = val
        if has_res:
            rowbuf[n, rs, :] = val
            ss_ref[rs, :] = ss_ref[rs, :] + jnp.sum(val * val, axis=-1, keepdims=True)

    if has_res:
        @pl.when(n == nn - 1)
        def _():
            scale = _rms_scale(ss_ref[...], n_out)
            for j in range(nn):
                sl = slice(j * bn, (j + 1) * bn)
                hn_ref[:, sl] = ((rowbuf[j] * scale) * g_ref[:, sl]).astype(hn_ref.dtype)


def _mm(lhs, w, *, glu, bm, bn, layer=None, resid=None, gamma=None, hn_dtype=None, write_x=True):
    m, k = lhs.shape
    emit = layer is not None
    if emit:
        assert m == bm
        n_out = w.shape[2] // (2 if glu else 1)
    else:
        n_out = w[0].shape[1]
    nn = n_out // bn
    has_res = resid is not None
    in_specs = [pl.BlockSpec((bm, k), lambda i, j: (i, 0))]
    args = [lhs]
    if emit:
        in_specs.append(pl.BlockSpec((None, k, bn), lambda i, j: (layer, 0, j)))
        args.append(w)
        if glu:
            in_specs.append(pl.BlockSpec((None, k, bn), lambda i, j: (layer, 0, j + nn)))
            args.append(w)
    else:
        for wi in w:
            in_specs.append(pl.BlockSpec((k, bn), lambda i, j: (0, j)))
            args.append(wi)
    out_shape, out_specs, scratch = [], [], []
    if has_res:
        in_specs += [pl.BlockSpec((bm, bn), lambda i, j: (i, j)),
                     pl.BlockSpec((1, n_out), lambda i, j: (0, 0))]
        args += [resid, gamma.reshape(1, n_out)]
    if write_x:
        out_shape.append(jax.ShapeDtypeStruct((m, n_out), F32))
        out_specs.append(pl.BlockSpec((bm, bn), lambda i, j: (i, j)))
    if has_res:
        out_shape.append(jax.ShapeDtypeStruct((m, n_out), hn_dtype))
        out_specs.append(pl.BlockSpec((bm, n_out), lambda i, j: (i, 0)))
        scratch += [pltpu.VMEM((nn, bm, bn), F32), pltpu.VMEM((bm, 1), F32)]
    if emit:
        for _ in range(2 if glu else 1):
            out_shape.append(jax.ShapeDtypeStruct((k, n_out), BF16))
            out_specs.append(pl.BlockSpec((k, bn), lambda i, j: (0, j)))
    return pl.pallas_call(
        functools.partial(_mm_body, glu=glu, has_res=has_res, write_x=write_x, emit=emit,
                          nn=nn, bn=bn, n_out=n_out),
        out_shape=out_shape,
        grid=(m // bm, nn),
        in_specs=in_specs,
        out_specs=out_specs,
        scratch_shapes=scratch,
        compiler_params=_cparams(2),
        name="mm_glu" if glu else "mm_lin",
    )(*args)


def _shift_rows(g, tail, k):
    r = pltpu.roll(g, k, 0)
    t = pltpu.roll(tail, k, 0)
    i8 = lax.broadcasted_iota(jnp.int32, tail.shape, 0)
    head = jnp.where(i8 < k, t, r[0:SUBLANES, :])
    return jnp.concatenate([head, r[SUBLANES:, :]], axis=0)


def _ffn1_prompt_body(h_ref, wg_ref, wu_ref, cw_ref, act_ref, tail_ref, carry, *, bm, tiles_per_seq):
    m = pl.program_id(0)
    n = pl.program_id(1)
    cw = cw_ref[...]
    taps = cw.shape[0]

    @pl.when(m % tiles_per_seq == 0)
    def _():
        carry[n] = jnp.zeros(carry.shape[1:], F32)

    tail = carry[n]
    for r0 in range(0, bm, ROW_CHUNK):
        rs = slice(r0, r0 + ROW_CHUNK)
        h = h_ref[rs, :]
        g = jnp.dot(h, wg_ref[...], preferred_element_type=F32)
        up = jnp.dot(h, wu_ref[...], preferred_element_type=F32)
        gc = g * cw[taps - 1:taps, :]
        for k in range(taps - 1):
            gc = gc + _shift_rows(g, tail, taps - 1 - k) * cw[k:k + 1, :]
        act_ref[rs, :] = (jax.nn.silu(gc) * up).astype(act_ref.dtype)
        tail = g[ROW_CHUNK - SUBLANES:, :]
    carry[n] = tail
    tail_ref[...] = tail


def _ffn1_prompt(h, wg, wu, cw, layer, *, bm, bn, seq_len):
    m, k = h.shape
    f = wg.shape[1]
    nn = f // bn
    tiles_per_seq = seq_len // bm
    act, tails = pl.pallas_call(
        functools.partial(_ffn1_prompt_body, bm=bm, tiles_per_seq=tiles_per_seq),
        out_shape=[jax.ShapeDtypeStruct((m, f), BF16),
                   jax.ShapeDtypeStruct((m // bm, SUBLANES, f), F32)],
        grid=(m // bm, nn),
        in_specs=[pl.BlockSpec((bm, k), lambda i, j: (i, 0)),
                  pl.BlockSpec((k, bn), lambda i, j: (0, j)),
                  pl.BlockSpec((k, bn), lambda i, j: (0, j)),
                  pl.BlockSpec((None, cw.shape[1], bn), lambda i, j: (layer, 0, j))],
        out_specs=[pl.BlockSpec((bm, bn), lambda i, j: (i, j)),
                   pl.BlockSpec((None, SUBLANES, bn), lambda i, j: (i, 0, j))],
        scratch_shapes=[pltpu.VMEM((nn, SUBLANES, bn), F32)],
        compiler_params=_cparams(2),
        name="ffn_gate_up",
    )(h, wg, wu, cw)
    return act, tails[tiles_per_seq - 1::tiles_per_seq]


def _ffn1_sample_body(h_ref, wg_ref, wu_ref, cw_ref, c_ref, act_ref, st_ref, wgo_ref, wuo_ref, *,
                      nb, seq):
    wgo_ref[...] = wg_ref[...].astype(BF16)
    wuo_ref[...] = wu_ref[...].astype(BF16)
    h = h_ref[...]
    g = jnp.dot(h, wgo_ref[...], preferred_element_type=F32)
    up = jnp.dot(h, wuo_ref[...], preferred_element_type=F32)
    cw = cw_ref[...]
    taps = cw.shape[0]
    hist = [c_ref[r] for r in range(taps - 1)] + [g[t * nb:(t + 1) * nb, :] for t in range(seq)]
    for t in range(seq):
        gc = hist[t] * cw[0:1, :]
        for k in range(1, taps):
            gc = gc + hist[t + k] * cw[k:k + 1, :]
        act_ref[t * nb:(t + 1) * nb, :] = (
            jax.nn.silu(gc) * up[t * nb:(t + 1) * nb, :]).astype(act_ref.dtype)
    for r in range(taps - 1):
        st_ref[r] = hist[seq + r]


def _ffn1_sample(h, wg, wu, cw, cache, layer, *, bn, seq):
    m, k = h.shape
    f = wg.shape[2]
    nb = m // seq
    hist = cw.shape[1] - 1
    wspec = pl.BlockSpec((None, k, bn), lambda j: (layer, 0, j))
    wout = pl.BlockSpec((k, bn), lambda j: (0, j))
    return pl.pallas_call(
        functools.partial(_ffn1_sample_body, nb=nb, seq=seq),
        out_shape=[jax.ShapeDtypeStruct((m, f), BF16),
                   jax.ShapeDtypeStruct((hist, nb, f), F32),
                   jax.ShapeDtypeStruct((k, f), BF16),
                   jax.ShapeDtypeStruct((k, f), BF16)],
        grid=(f // bn,),
        in_specs=[pl.BlockSpec((m, k), lambda j: (0, 0)), wspec, wspec,
                  pl.BlockSpec((None, cw.shape[1], bn), lambda j: (layer, 0, j)),
                  pl.BlockSpec((None, hist, nb, bn), lambda j: (layer, 0, 0, j))],
        out_specs=[pl.BlockSpec((m, bn), lambda j: (0, j)),
                   pl.BlockSpec((hist, nb, bn), lambda j: (0, 0, j)), wout, wout],
        compiler_params=_cparams(1),
        name="ffn_gate_up_s",
    )(h, wg, wu, cw, cache)


CONV_ROWS = 32
HALO = 32


def _ln_silu(c, g, b):
    mu = jnp.mean(c, axis=-1, keepdims=True)
    d = c - mu
    var = jnp.mean(d * d, axis=-1, keepdims=True)
    y = d * lax.rsqrt(var + EPS)
    return jax.nn.silu(y * g + b)


def _conv_p_body(u_ref, dw_ref, db_ref, lg_ref, lb_ref, o_ref, ubuf, cbuf, *, bm, taps,
                 tiles_per_seq):
    m = pl.program_id(0)
    d = u_ref.shape[1]
    nc = d // LANES

    @pl.when(m % tiles_per_seq == 0)
    def _():
        ubuf[:, 0:HALO, :] = jnp.zeros((nc, HALO, LANES), F32)

    @pl.when(m % tiles_per_seq != 0)
    def _():
        ubuf[:, 0:HALO, :] = ubuf[:, bm:bm + HALO, :]

    for c in range(nc):
        ubuf[c, HALO:HALO + bm, :] = u_ref[:, c * LANES:(c + 1) * LANES]
    off = HALO - (taps - 1)

    def lane_chunk(c, carry):
        wv = dw_ref[c]
        bias = db_ref[c]
        for r0 in range(0, bm, CONV_ROWS):
            acc = jnp.zeros((CONV_ROWS, LANES), F32)
            for k in range(taps):
                acc = acc + ubuf[c, r0 + off + k:r0 + off + k + CONV_ROWS, :] * wv[k:k + 1, :]
            cbuf[c, r0:r0 + CONV_ROWS, :] = acc + bias
        return carry

    lax.fori_loop(0, nc, lane_chunk, 0)
    cfull = jnp.concatenate([cbuf[c] for c in range(nc)], axis=-1)
    o_ref[...] = _ln_silu(cfull, lg_ref[...], lb_ref[...]).astype(o_ref.dtype)


def _lane_chunked(x):
    l, r, d = x.shape
    return x.reshape(l, r, d // LANES, LANES).transpose(0, 2, 1, 3)


def _vec3(x):
    return x.reshape(x.shape[0], 1, x.shape[1])


def _conv_prompt(u, dw, db, lg, lb, layer, *, bm, seq_len):
    m, d = u.shape
    taps = dw.shape[1]
    nc = d // LANES
    vec = lambda: pl.BlockSpec((None, 1, d), lambda i: (layer, 0, 0))
    return pl.pallas_call(
        functools.partial(_conv_p_body, bm=bm, taps=taps, tiles_per_seq=seq_len // bm),
        out_shape=jax.ShapeDtypeStruct((m, d), BF16),
        grid=(m // bm,),
        in_specs=[pl.BlockSpec((bm, d), lambda i: (i, 0)),
                  pl.BlockSpec((None, nc, taps, LANES), lambda i: (layer, 0, 0, 0)),
                  pl.BlockSpec((None, nc, 1, LANES), lambda i: (layer, 0, 0, 0)),
                  vec(), vec()],
        out_specs=pl.BlockSpec((bm, d), lambda i: (i, 0)),
        scratch_shapes=[pltpu.VMEM((nc, HALO + bm, LANES), F32),
                        pltpu.VMEM((nc, bm, LANES), F32)],
        compiler_params=_cparams(1),
        name="conv_prompt",
    )(u, _lane_chunked(dw), _lane_chunked(_vec3(db)), _vec3(lg), _vec3(lb))


def _conv_s_body(cache_ref, u_ref, dw_ref, db_ref, lg_ref, lb_ref, o_ref, st_ref, cfull, *,
                 taps, seq, nsteps):
    c = pl.program_id(0)
    hist = taps - 1
    ext = lambda r: cache_ref[r] if r < hist else u_ref[r - hist]
    wv = dw_ref[...]
    bias = db_ref[...]
    for t in range(seq):
        acc = ext(t) * wv[0:1, :]
        for k in range(1, taps):
            acc = acc + ext(t + k) * wv[k:k + 1, :]
        cfull[c, t] = acc + bias
    for r in range(hist):
        st_ref[r] = ext(r + seq)

    @pl.when(c == nsteps - 1)
    def _():
        for t in range(seq):
            row = jnp.concatenate([cfull[cc, t] for cc in range(nsteps)], axis=-1)
            o_ref[t] = _ln_silu(row, lg_ref[...], lb_ref[...]).astype(o_ref.dtype)


def _conv_sample(cache, u, dw, db, lg, lb, layer, *, bc):
    seq, b, d = u.shape
    taps = dw.shape[1]
    hist = taps - 1
    nsteps = d // bc
    vec = lambda: pl.BlockSpec((None, 1, d), lambda i: (layer, 0, 0))
    return pl.pallas_call(
        functools.partial(_conv_s_body, taps=taps, seq=seq, nsteps=nsteps),
        out_shape=[jax.ShapeDtypeStruct((seq, b, d), BF16),
                   jax.ShapeDtypeStruct((hist, b, d), F32)],
        grid=(nsteps,),
        in_specs=[pl.BlockSpec((None, hist, b, bc), lambda i: (layer, 0, 0, i)),
                  pl.BlockSpec((seq, b, bc), lambda i: (0, 0, i)),
                  pl.BlockSpec((None, taps, bc), lambda i: (layer, 0, i)),
                  pl.BlockSpec((None, 1, bc), lambda i: (layer, 0, i)),
                  vec(), vec()],
        out_specs=[pl.BlockSpec((seq, b, d), lambda i: (0, 0, 0)),
                   pl.BlockSpec((hist, b, bc), lambda i: (0, 0, i))],
        scratch_shapes=[pltpu.VMEM((nsteps, seq, b, bc), F32)],
        compiler_params=_cparams(1),
        name="conv_sample",
    )(cache, u, dw, _vec3(db), _vec3(lg), _vec3(lb))


def _s5_abar(lam_re, lam_im, dt):
    mag = jnp.exp(lam_re * dt)
    ang = lam_im * dt
    return mag * jnp.cos(ang), mag * jnp.sin(ang)


def _s5_disc_body(are_ref, aim_ref, ldt_ref, arep_ref, airep_ref, bre_ref, bim_ref,
                  abr_ref, abi_ref, bbr_ref, bbi_ref):
    dt = jnp.exp(ldt_ref[...])
    abar_re, abar_im = _s5_abar(are_ref[...], aim_ref[...], dt)
    abr_ref[...] = abar_re
    abi_ref[...] = abar_im
    lam_re = arep_ref[...]
    lam_im = airep_ref[...]
    rep_re, rep_im = _s5_abar(lam_re, lam_im, dt)
    nr = rep_re - 1.0
    ni = rep_im
    den = lam_re * lam_re + lam_im * lam_im
    q_re = (nr * lam_re + ni * lam_im) / den
    q_im = (ni * lam_re - nr * lam_im) / den
    br = bre_ref[...]
    bi = bim_ref[...]
    bbr_ref[...] = q_re * br - q_im * bi
    bbi_ref[...] = q_re * bi + q_im * br


def _s5_discretise(a_re, a_im, log_dt, b_re, b_im):
    ns, g, p = a_re.shape
    rows = ns * g
    two = lambda x: x.reshape(rows, p)
    rep = lambda x: jnp.repeat(x.reshape(rows, p), GROUP_SIZE, axis=1)
    wide = lambda x: x.reshape(rows, p * GROUP_SIZE)
    outs = pl.pallas_call(
        _s5_disc_body,
        out_shape=[jax.ShapeDtypeStruct((rows, p), F32)] * 2
        + [jax.ShapeDtypeStruct((rows, p * GROUP_SIZE), F32)] * 2,
        name="s5_discretise",
    )(two(a_re), two(a_im), log_dt.reshape(rows, 1), rep(a_re), rep(a_im), wide(b_re), wide(b_im))
    abr, abi, bbr, bbi = outs
    return (abr.reshape(ns, g, p), abi.reshape(ns, g, p),
            bbr.reshape(ns, g, p, GROUP_SIZE), bbi.reshape(ns, g, p, GROUP_SIZE))


SLAB_GROUPS = 16
SLAB = SLAB_GROUPS * GROUP_SIZE


def _s5_block_weights(bb_re, bb_im, c_re, c_im):
    g, p, gs = bb_re.shape
    s = g // SLAB_GROUPS
    eye = jnp.eye(SLAB_GROUPS, dtype=F32)

    def bexp(bb):
        x = bb.reshape(s, SLAB_GROUPS, p, gs)
        return jnp.einsum('sgpi,gh->sgihp', x, eye).reshape(s, SLAB, SLAB_GROUPS * p)

    def cexp(cc):
        x = cc.reshape(s, SLAB_GROUPS, gs, p)
        return jnp.einsum('sgop,gh->sgpho', x, eye).reshape(s, SLAB_GROUPS * p, SLAB)

    bmat = jnp.concatenate([bexp(bb_re), bexp(bb_im)], axis=2).astype(BF16)
    cmat = jnp.concatenate([cexp(c_re), cexp(-c_im)], axis=1).astype(BF16)
    return bmat, cmat


def _s5_pitch(rows):
    return rows + SUBLANES // 2


def _s5_project_in(u, bmat_ref, z, *, rows, pitch, n_slab, tiles_half):
    tiles_slab = tiles_half // n_slab
    for s in range(n_slab):
        bu = jnp.dot(u[:, s * SLAB:(s + 1) * SLAB].astype(BF16), bmat_ref[s],
                     preferred_element_type=F32)
        for half in range(2):
            for q in range(tiles_slab):
                j = half * tiles_half + s * tiles_slab + q
                col = (half * tiles_slab + q) * LANES
                z[j * pitch:j * pitch + rows, :] = bu[:, col:col + LANES]


def _s5_project_out(u, z, cmat_ref, d_ref, store, *, rows, pitch, n_slab, tiles_half):
    tiles_slab = tiles_half // n_slab
    for s in range(n_slab):
        parts = []
        for half in range(2):
            for q in range(tiles_slab):
                j = half * tiles_half + s * tiles_slab + q
                parts.append(z[j * pitch:j * pitch + rows, :].astype(BF16))
        st = jnp.concatenate(parts, axis=-1)
        y = jnp.dot(st, cmat_ref[s], preferred_element_type=F32)
        sl = slice(s * SLAB, (s + 1) * SLAB)
        y = y + d_ref[:, sl] * u[:, sl]
        store(sl, jax.nn.gelu(y))


def _s5_step(z, t, state, abar, *, pitch, tiles_half):
    nv = tiles_half // SUBLANES
    new = []
    for k in range(nv):
        idx_re = pl.ds(k * SUBLANES * pitch + t, SUBLANES, stride=pitch)
        idx_im = pl.ds((tiles_half + k * SUBLANES) * pitch + t, SUBLANES, stride=pitch)
        ar, ai = abar[k]
        sr, si = state[k]
        nr = ar * sr - ai * si + z[idx_re, :]
        ni = ar * si + ai * sr + z[idx_im, :]
        z[idx_re, :] = nr
        z[idx_im, :] = ni
        new.append((nr, ni))
    return tuple(new)


def _s5_p_body(u_ref, bmat_ref, cmat_ref, abr_ref, abi_ref, d_ref, o_ref, sre_ref, sim_ref,
               z, st_re, st_im, *, rows, tiles_per_seq, n_slab, tiles_half):
    m = pl.program_id(0)
    pitch = _s5_pitch(rows)
    nv = tiles_half // SUBLANES
    u = u_ref[...]
    _s5_project_in(u, bmat_ref, z, rows=rows, pitch=pitch, n_slab=n_slab, tiles_half=tiles_half)

    @pl.when(m % tiles_per_seq == 0)
    def _():
        st_re[...] = jnp.zeros(st_re.shape, F32)
        st_im[...] = jnp.zeros(st_im.shape, F32)

    vsl = lambda k: slice(k * SUBLANES, (k + 1) * SUBLANES)
    abar = tuple((abr_ref[vsl(k), :], abi_ref[vsl(k), :]) for k in range(nv))
    state0 = tuple((st_re[vsl(k), :], st_im[vsl(k), :]) for k in range(nv))

    def step(t, state):
        return _s5_step(z, t, state, abar, pitch=pitch, tiles_half=tiles_half)

    state = lax.fori_loop(0, rows, step, state0)
    for k in range(nv):
        st_re[vsl(k), :] = state[k][0]
        st_im[vsl(k), :] = state[k][1]
    sre_ref[...] = st_re[...]
    sim_ref[...] = st_im[...]

    def store(sl, val):
        o_ref[:, sl] = val.astype(o_ref.dtype)

    _s5_project_out(u, z, cmat_ref, d_ref, store, rows=rows, pitch=pitch, n_slab=n_slab,
                    tiles_half=tiles_half)


def _s5_prompt(u, bmat, cmat, abar_re, abar_im, dskip, *, rows, seq_len):
    m, d = u.shape
    n_slab = bmat.shape[0]
    tiles_half = abar_re.shape[0]
    pitch = _s5_pitch(rows)
    tiles_per_seq = seq_len // rows
    n_seq = m // seq_len
    st_spec = pl.BlockSpec((None, tiles_half, LANES), lambda i: (i // tiles_per_seq, 0, 0))
    full = lambda a: pl.BlockSpec(a.shape, lambda i: (0,) * a.ndim)
    return pl.pallas_call(
        functools.partial(_s5_p_body, rows=rows, tiles_per_seq=tiles_per_seq, n_slab=n_slab,
                          tiles_half=tiles_half),
        out_shape=[jax.ShapeDtypeStruct((m, d), BF16),
                   jax.ShapeDtypeStruct((n_seq, tiles_half, LANES), F32),
                   jax.ShapeDtypeStruct((n_seq, tiles_half, LANES), F32)],
        grid=(m // rows,),
        in_specs=[pl.BlockSpec((rows, d), lambda i: (i, 0)),
                  full(bmat), full(cmat), full(abar_re), full(abar_im), full(dskip)],
        out_specs=[pl.BlockSpec((rows, d), lambda i: (i, 0)), st_spec, st_spec],
        scratch_shapes=[pltpu.VMEM((2 * tiles_half * pitch, LANES), F32),
                        pltpu.VMEM((tiles_half, LANES), F32),
                        pltpu.VMEM((tiles_half, LANES), F32)],
        compiler_params=_cparams(1),
        name="s5_prompt",
    )(u, bmat, cmat, abar_re, abar_im, dskip)


def _s5_s_body(u_ref, h0r_ref, h0i_ref, bmat_ref, cmat_ref, abr_ref, abi_ref, d_ref,
               o_ref, sre_ref, sim_ref, z, zh, *, nb, seq, n_slab, tiles_half):
    rows = nb * seq
    pitch = _s5_pitch(rows)
    hp = _s5_pitch(nb)
    nv = tiles_half // SUBLANES
    d = u_ref.shape[2]
    u = u_ref[...].reshape(rows, d)
    _s5_project_in(u, bmat_ref, z, rows=rows, pitch=pitch, n_slab=n_slab, tiles_half=tiles_half)
    for j in range(tiles_half):
        cs = slice(j * LANES, (j + 1) * LANES)
        zh[j * hp:j * hp + nb, :] = h0r_ref[:, cs]
        zh[(tiles_half + j) * hp:(tiles_half + j) * hp + nb, :] = h0i_ref[:, cs]

    vsl = lambda k: slice(k * SUBLANES, (k + 1) * SUBLANES)
    abar = tuple((abr_ref[vsl(k), :], abi_ref[vsl(k), :]) for k in range(nv))

    def one_seq(b, carry):
        state = []
        for k in range(nv):
            idx_re = pl.ds(k * SUBLANES * hp + b, SUBLANES, stride=hp)
            idx_im = pl.ds((tiles_half + k * SUBLANES) * hp + b, SUBLANES, stride=hp)
            state.append((zh[idx_re, :], zh[idx_im, :]))
        state = tuple(state)
        for t in range(seq):
            state = _s5_step(z, t * nb + b, state, abar, pitch=pitch, tiles_half=tiles_half)
        for k in range(nv):
            idx_re = pl.ds(k * SUBLANES * hp + b, SUBLANES, stride=hp)
            idx_im = pl.ds((tiles_half + k * SUBLANES) * hp + b, SUBLANES, stride=hp)
            zh[idx_re, :] = state[k][0]
            zh[idx_im, :] = state[k][1]
        return carry

    lax.fori_loop(0, nb, one_seq, 0)
    for j in range(tiles_half):
        cs = slice(j * LANES, (j + 1) * LANES)
        sre_ref[:, cs] = zh[j * hp:j * hp + nb, :]
        sim_ref[:, cs] = zh[(tiles_half + j) * hp:(tiles_half + j) * hp + nb, :]

    def store(sl, val):
        o_ref[:, :, sl] = val.reshape(seq, nb, val.shape[1]).astype(o_ref.dtype)

    _s5_project_out(u, z, cmat_ref, d_ref, store, rows=rows, pitch=pitch, n_slab=n_slab,
                    tiles_half=tiles_half)


def _s5_sample(u, h0_re, h0_im, bmat, cmat, abar_re, abar_im, dskip, *, nb):
    seq, b, d = u.shape
    nstate = h0_re.shape[1]
    n_slab = bmat.shape[0]
    tiles_half = abar_re.shape[0]
    pitch = _s5_pitch(nb * seq)
    hp = _s5_pitch(nb)
    full = lambda a: pl.BlockSpec(a.shape, lambda i: (0,) * a.ndim)
    st_spec = pl.BlockSpec((nb, nstate), lambda i: (i, 0))
    u_spec = pl.BlockSpec((seq, nb, d), lambda i: (0, i, 0))
    return pl.pallas_call(
        functools.partial(_s5_s_body, nb=nb, seq=seq, n_slab=n_slab, tiles_half=tiles_half),
        out_shape=[jax.ShapeDtypeStruct((seq, b, d), BF16),
                   jax.ShapeDtypeStruct(h0_re.shape, F32),
                   jax.ShapeDtypeStruct(h0_im.shape, F32)],
        grid=(b // nb,),
        in_specs=[u_spec, st_spec, st_spec,
                  full(bmat), full(cmat), full(abar_re), full(abar_im), full(dskip)],
        out_specs=[u_spec, st_spec, st_spec],
        scratch_shapes=[pltpu.VMEM((2 * tiles_half * pitch, LANES), F32),
                        pltpu.VMEM((2 * tiles_half * hp, LANES), F32)],
        compiler_params=_cparams(1),
        name="s5_sample",
    )(u, h0_re, h0_im, bmat, cmat, abar_re, abar_im, dskip)


def _hn_dtype(i):
    return F32 if (i % N_MIXERS == 1) else BF16


def _run_sample(x, states, p):
    seq, nb, d = x.shape
    m = seq * nb
    depth = p["norm_mix"].shape[0]
    conv_st, ssm_re_st, ssm_im_st, ffn_st = states
    new_conv, new_re, new_im, new_ffn, wbf = [], [], [], [], []
    x = x.reshape(m, d)
    hn = _rmsnorm(x, p["norm_mix"][0], _hn_dtype(0), m)
    y = None
    for i in range(depth):
        j = i // N_MIXERS
        w = {}
        if i % N_MIXERS == 0:
            u, w["in_a"], w["in_b"] = _mm(hn, p["conv_w_in"], glu=True, bm=m, bn=512, layer=j)
            c, st = _conv_sample(conv_st, u.reshape(seq, nb, d), p["conv_dw"], p["conv_dw_b"],
                                 p["conv_ln_g"], p["conv_ln_b"], j, bc=256)
            new_conv.append(st)
            x, hn, w["out"] = _mm(c.reshape(m, d), p["conv_w_out"], glu=False, bm=m, bn=512,
                                  layer=j, resid=x, gamma=p["norm_ffn"][i], hn_dtype=BF16)
        else:
            v, s_re, s_im = _s5_sample(hn.reshape(seq, nb, d), ssm_re_st[j].reshape(nb, -1),
                                       ssm_im_st[j].reshape(nb, -1), p["s5_bmat"][j],
                                       p["s5_cmat"][j], p["s5_abar_re"][j], p["s5_abar_im"][j],
                                       p["ssm_D"][j].reshape(1, d), nb=32)
            new_re.append(s_re.reshape(ssm_re_st.shape[1:]))
            new_im.append(s_im.reshape(ssm_im_st.shape[1:]))
            x, hn, w["glu_a"], w["glu_b"] = _mm(
                v.reshape(m, d), p["ssm_w_glu"], glu=True, bm=m, bn=512, layer=j, resid=x,
                gamma=p["norm_ffn"][i], hn_dtype=BF16)
        act, st, w["gate"], w["up"] = _ffn1_sample(hn, p["ffn_w_gate"], p["ffn_w_up"],
                                                   p["ffn_conv"], ffn_st, i, bn=512, seq=seq)
        new_ffn.append(st)
        last = i == depth - 1
        gamma = p["norm_final"] if last else p["norm_mix"][i + 1]
        outs = _mm(act, p["ffn_w_down"], glu=False, bm=m, bn=256, layer=i, resid=x, gamma=gamma,
                   hn_dtype=F32 if last else _hn_dtype(i + 1), write_x=not last)
        if last:
            y, w["down"] = outs
        else:
            x, hn, w["down"] = outs
        wbf.append(w)
    return (y.reshape(seq, nb, d), jnp.stack(new_conv), jnp.stack(new_re), jnp.stack(new_im),
            jnp.stack(new_ffn)), wbf


def _run_prompt(x, p, wbf, *, seq_len, bm):
    m, d = x.shape
    n_seq = m // seq_len
    depth = p["norm_mix"].shape[0]
    new_conv, new_re, new_im, new_ffn = [], [], [], []
    hn = _rmsnorm(x, p["norm_mix"][0], _hn_dtype(0), bm)
    y = None
    for i in range(depth):
        j = i // N_MIXERS
        w = wbf[i]
        if i % N_MIXERS == 0:
            (u,) = _mm(hn, (w["in_a"], w["in_b"]), glu=True, bm=bm, bn=512)
            cs = p["conv_dw"].shape[1] - 1
            new_conv.append(u.reshape(n_seq, seq_len, d)[:, -cs:])
            c = _conv_prompt(u, p["conv_dw"], p["conv_dw_b"], p["conv_ln_g"], p["conv_ln_b"], j,
                             bm=256, seq_len=seq_len)
            x, hn = _mm(c, (w["out"],), glu=False, bm=bm, bn=512, resid=x,
                        gamma=p["norm_ffn"][i], hn_dtype=BF16)
        else:
            v, s_re, s_im = _s5_prompt(hn, p["s5_bmat"][j], p["s5_cmat"][j], p["s5_abar_re"][j],
                                       p["s5_abar_im"][j], p["ssm_D"][j].reshape(1, d), rows=256,
                                       seq_len=seq_len)
            g, pdim = p["ssm_A_re"].shape[1:]
            new_re.append(s_re.reshape(n_seq, g, pdim))
            new_im.append(s_im.reshape(n_seq, g, pdim))
            x, hn = _mm(v, (w["glu_a"], w["glu_b"]), glu=True, bm=bm, bn=512, resid=x,
                        gamma=p["norm_ffn"][i], hn_dtype=BF16)
        act, tail = _ffn1_prompt(hn, w["gate"], w["up"], p["ffn_conv"], i, bm=bm, bn=512,
                                 seq_len=seq_len)
        fs = p["ffn_conv"].shape[1] - 1
        new_ffn.append(tail[:, -fs:])
        last = i == depth - 1
        gamma = p["norm_final"] if last else p["norm_mix"][i + 1]
        outs = _mm(act, (w["down"],), glu=False, bm=512, bn=512, resid=x, gamma=gamma,
                   hn_dtype=F32 if last else _hn_dtype(i + 1), write_x=not last)
        if last:
            (y,) = outs
        else:
            x, hn = outs
    return (y, jnp.stack(new_conv), jnp.stack(new_re), jnp.stack(new_im), jnp.stack(new_ffn))


def kernel(x_prompt, x_sample, state_conv, state_ssm_re, state_ssm_im, state_ffn, norm_mix, norm_ffn, norm_final, conv_w_in, conv_dw, conv_dw_b, conv_ln_g, conv_ln_b, conv_w_out, ssm_A_re, ssm_A_im, ssm_log_dt, ssm_B_re, ssm_B_im, ssm_C_re, ssm_C_im, ssm_D, ssm_w_glu, ffn_w_gate, ffn_w_up, ffn_conv, ffn_w_down):
    bp, sp, d = x_prompt.shape
    abr, abi, bbr, bbi = _s5_discretise(ssm_A_re, ssm_A_im, ssm_log_dt, ssm_B_re, ssm_B_im)
    ns, g, pdim = ssm_A_re.shape
    blocks = [_s5_block_weights(bbr[j], bbi[j], ssm_C_re[j], ssm_C_im[j]) for j in range(ns)]
    tiles_half = g * pdim // LANES
    p = dict(
        norm_mix=norm_mix, norm_ffn=norm_ffn, norm_final=norm_final,
        conv_w_in=conv_w_in, conv_dw=conv_dw, conv_dw_b=conv_dw_b,
        conv_ln_g=conv_ln_g, conv_ln_b=conv_ln_b, conv_w_out=conv_w_out,
        ssm_A_re=ssm_A_re, ssm_D=ssm_D, ssm_w_glu=ssm_w_glu,
        s5_bmat=[b[0] for b in blocks], s5_cmat=[b[1] for b in blocks],
        s5_abar_re=abr.reshape(ns, tiles_half, LANES), s5_abar_im=abi.reshape(ns, tiles_half, LANES),
        ffn_w_gate=ffn_w_gate, ffn_w_up=ffn_w_up, ffn_conv=ffn_conv, ffn_w_down=ffn_w_down,
    )
    tm = lambda a: jnp.swapaxes(a, -3, -2)
    (y_s, conv_s, re_s, im_s, ffn_s), wbf = _run_sample(
        tm(x_sample), (tm(state_conv), state_ssm_re, state_ssm_im, tm(state_ffn)), p)
    y_p, conv_p, re_p, im_p, ffn_p = _run_prompt(
        x_prompt.reshape(bp * sp, d), p, wbf, seq_len=sp, bm=1024)
    return (y_p.reshape(bp, sp, d), tm(y_s), conv_p, tm(conv_s), re_p, im_p, re_s, im_s, ffn_p,
            tm(ffn_s))
```

```python
import functools

import jax
import jax.numpy as jnp
from jax import lax
from jax.experimental import pallas as pl
from jax.experimental.pallas import tpu as pltpu

F32 = jnp.float32
BF16 = jnp.bfloat16

EPS = 1e-6
N_MIXERS = 2
GROUP_SIZE = 16
LANES = 128
SUBLANES = 8
VMEM_LIMIT = 56 * 1024 * 1024
ROW_CHUNK = 256


def _cparams(n_axes):
    return pltpu.CompilerParams(
        dimension_semantics=("arbitrary",) * n_axes, vmem_limit_bytes=VMEM_LIMIT)


def _rms_scale(ss, n):
    return lax.rsqrt(ss / n + EPS)


def _rmsnorm_body(x_ref, g_ref, o_ref):
    x = x_ref[...]
    ss = jnp.sum(x * x, axis=-1, keepdims=True)
    o_ref[...] = ((x * _rms_scale(ss, x.shape[-1])) * g_ref[...]).astype(o_ref.dtype)


def _rmsnorm(x, g, out_dtype, bm):
    m, d = x.shape
    return pl.pallas_call(
        _rmsnorm_body,
        out_shape=jax.ShapeDtypeStruct((m, d), out_dtype),
        grid=(m // bm,),
        in_specs=[pl.BlockSpec((bm, d), lambda i: (i, 0)),
                  pl.BlockSpec((1, d), lambda i: (0, 0))],
        out_specs=pl.BlockSpec((bm, d), lambda i: (i, 0)),
        compiler_params=_cparams(1),
        name="rmsnorm",
    )(x, g.reshape(1, d))


def _mm_body(*refs, glu, has_res, write_x, emit, nn, bn, n_out):
    it = iter(refs)
    lhs_ref = next(it)
    wa_ref = next(it)
    wb_ref = next(it) if glu else None
    res_ref = next(it) if has_res else None
    g_ref = next(it) if has_res else None
    xo_ref = next(it) if write_x else None
    hn_ref = next(it) if has_res else None
    wao_ref = next(it) if emit else None
    wbo_ref = next(it) if (emit and glu) else None
    rowbuf = next(it) if has_res else None
    ss_ref = next(it) if has_res else None

    n = pl.program_id(1)
    bm = lhs_ref.shape[0]
    if emit:
        wao_ref[...] = wa_ref[...].astype(BF16)
        wa_ref = wao_ref
        if glu:
            wbo_ref[...] = wb_ref[...].astype(BF16)
            wb_ref = wbo_ref
    if has_res:
        @pl.when(n == 0)
        def _():
            ss_ref[...] = jnp.zeros(ss_ref.shape, F32)
    for r0 in range(0, bm, ROW_CHUNK):
        rs = slice(r0, min(r0 + ROW_CHUNK, bm))
        x = lhs_ref[rs, :]
        val = jnp.dot(x, wa_ref[...], preferred_element_type=F32)
        if glu:
            gate = jnp.dot(x, wb_ref[...], preferred_element_type=F32)
            val = val * jax.nn.sigmoid(gate)
        if has_res:
            val = res_ref[rs, :] + val
        if write_x:
            xo_ref[rs, :] = val
        if has_res:
            rowbuf[n, rs, :] = val
            ss_ref[rs, :] = ss_ref[rs, :] + jnp.sum(val * val, axis=-1, keepdims=True)

    if has_res:
        @pl.when(n == nn - 1)
        def _():
            scale = _rms_scale(ss_ref[...], n_out)
            for j in range(nn):
                sl = slice(j * bn, (j + 1) * bn)
                hn_ref[:, sl] = ((rowbuf[j] * scale) * g_ref[:, sl]).astype(hn_ref.dtype)


def _mm(lhs, w, *, glu, bm, bn, layer=None, resid=None, gamma=None, hn_dtype=None, write_x=True):
    m, k = lhs.shape
    emit = layer is not None
    if emit:
        assert m == bm
        n_out = w.shape[2] // (2 if glu else 1)
    else:
        n_out = w[0].shape[1]
    nn = n_out // bn
    has_res = resid is not None
    in_specs = [pl.BlockSpec((bm, k), lambda i, j: (i, 0))]
    args = [lhs]
    if emit:
        in_specs.append(pl.BlockSpec((None, k, bn), lambda i, j: (layer, 0, j)))
        args.append(w)
        if glu:
            in_specs.append(pl.BlockSpec((None, k, bn), lambda i, j: (layer, 0, j + nn)))
            args.append(w)
    else:
        for wi in w:
            in_specs.append(pl.BlockSpec((k, bn), lambda i, j: (0, j)))
            args.append(wi)
    out_shape, out_specs, scratch = [], [], []
    if has_res:
        in_specs += [pl.BlockSpec((bm, bn), lambda i, j: (i, j)),
                     pl.BlockSpec((1, n_out), lambda i, j: (0, 0))]
        args += [resid, gamma.reshape(1, n_out)]
    if write_x:
        out_shape.append(jax.ShapeDtypeStruct((m, n_out), F32))
        out_specs.append(pl.BlockSpec((bm, bn), lambda i, j: (i, j)))
    if has_res:
        out_shape.append(jax.ShapeDtypeStruct((m, n_out), hn_dtype))
        out_specs.append(pl.BlockSpec((bm, n_out), lambda i, j: (i, 0)))
        scratch += [pltpu.VMEM((nn, bm, bn), F32), pltpu.VMEM((bm, 1), F32)]
    if emit:
        for _ in range(2 if glu else 1):
            out_shape.append(jax.ShapeDtypeStruct((k, n_out), BF16))
            out_specs.append(pl.BlockSpec((k, bn), lambda i, j: (0, j)))
    return pl.pallas_call(
        functools.partial(_mm_body, glu=glu, has_res=has_res, write_x=write_x, emit=emit,
                          nn=nn, bn=bn, n_out=n_out),
        out_shape=out_shape,
        grid=(m // bm, nn),
        in_specs=in_specs,
        out_specs=out_specs,
        scratch_shapes=scratch,
        compiler_params=_cparams(2),
        name="mm_glu" if glu else "mm_lin",
    )(*args)


def _shift_rows(g, tail, k):
    r = pltpu.roll(g, k, 0)
    t = pltpu.roll(tail, k, 0)
    i8 = lax.broadcasted_iota(jnp.int32, tail.shape, 0)
    head = jnp.where(i8 < k, t, r[0:SUBLANES, :])
    return jnp.concatenate([head, r[SUBLANES:, :]], axis=0)


def _ffn1_prompt_body(h_ref, wg_ref, wu_ref, cw_ref, act_ref, tail_ref, carry, *, bm, tiles_per_seq):
    m = pl.program_id(0)
    n = pl.program_id(1)
    cw = cw_ref[...]
    taps = cw.shape[0]

    @pl.when(m % tiles_per_seq == 0)
    def _():
        carry[n] = jnp.zeros(carry.shape[1:], F32)

    tail = carry[n]
    for r0 in range(0, bm, ROW_CHUNK):
        rs = slice(r0, r0 + ROW_CHUNK)
        h = h_ref[rs, :]
        g = jnp.dot(h, wg_ref[...], preferred_element_type=F32)
        up = jnp.dot(h, wu_ref[...], preferred_element_type=F32)
        gc = g * cw[taps - 1:taps, :]
        for k in range(taps - 1):
            gc = gc + _shift_rows(g, tail, taps - 1 - k) * cw[k:k + 1, :]
        act_ref[rs, :] = (jax.nn.silu(gc) * up).astype(act_ref.dtype)
        tail = g[ROW_CHUNK - SUBLANES:, :]
    carry[n] = tail
    tail_ref[...] = tail


def _ffn1_prompt(h, wg, wu, cw, layer, *, bm, bn, seq_len):
    m, k = h.shape
    f = wg.shape[1]
    nn = f // bn
    tiles_per_seq = seq_len // bm
    act, tails = pl.pallas_call(
        functools.partial(_ffn1_prompt_body, bm=bm, tiles_per_seq=tiles_per_seq),
        out_shape=[jax.ShapeDtypeStruct((m, f), BF16),
                   jax.ShapeDtypeStruct((m // bm, SUBLANES, f), F32)],
        grid=(m // bm, nn),
        in_specs=[pl.BlockSpec((bm, k), lambda i, j: (i, 0)),
                  pl.BlockSpec((k, bn), lambda i, j: (0, j)),
                  pl.BlockSpec((k, bn), lambda i, j: (0, j)),
                  pl.BlockSpec((None, cw.shape[1], bn), lambda i, j: (layer, 0, j))],
        out_specs=[pl.BlockSpec((bm, bn), lambda i, j: (i, j)),
                   pl.BlockSpec((None, SUBLANES, bn), lambda i, j: (i, 0, j))],
        scratch_shapes=[pltpu.VMEM((nn, SUBLANES, bn), F32)],
        compiler_params=_cparams(2),
        name="ffn_gate_up",
    )(h, wg, wu, cw)
    return act, tails[tiles_per_seq - 1::tiles_per_seq]


def _ffn1_sample_body(h_ref, wg_ref, wu_ref, cw_ref, c_ref, act_ref, st_ref, wgo_ref, wuo_ref, *,
                      nb, seq):
    wgo_ref[...] = wg_ref[...].astype(BF16)
    wuo_ref[...] = wu_ref[...].astype(BF16)
    h = h_ref[...]
    g = jnp.dot(h, wgo_ref[...], preferred_element_type=F32)
    up = jnp.dot(h, wuo_ref[...], preferred_element_type=F32)
    cw = cw_ref[...]
    taps = cw.shape[0]
    hist = [c_ref[r] for r in range(taps - 1)] + [g[t * nb:(t + 1) * nb, :] for t in range(seq)]
    for t in range(seq):
        gc = hist[t] * cw[0:1, :]
        for k in range(1, taps):
            gc = gc + hist[t + k] * cw[k:k + 1, :]
        act_ref[t * nb:(t + 1) * nb, :] = (
            jax.nn.silu(gc) * up[t * nb:(t + 1) * nb, :]).astype(act_ref.dtype)
    for r in range(taps - 1):
        st_ref[r] = hist[seq + r]


def _ffn1_sample(h, wg, wu, cw, cache, layer, *, bn, seq):
    m, k = h.shape
    f = wg.shape[2]
    nb = m // seq
    hist = cw.shape[1] - 1
    wspec = pl.BlockSpec((None, k, bn), lambda j: (layer, 0, j))
    wout = pl.BlockSpec((k, bn), lambda j: (0, j))
    return pl.pallas_call(
        functools.partial(_ffn1_sample_body, nb=nb, seq=seq),
        out_shape=[jax.ShapeDtypeStruct((m, f), BF16),
                   jax.ShapeDtypeStruct((hist, nb, f), F32),
                   jax.ShapeDtypeStruct((k, f), BF16),
                   jax.ShapeDtypeStruct((k, f), BF16)],
        grid=(f // bn,),
        in_specs=[pl.BlockSpec((m, k), lambda j: (0, 0)), wspec, wspec,
                  pl.BlockSpec((None, cw.shape[1], bn), lambda j: (layer, 0, j)),
                  pl.BlockSpec((None, hist, nb, bn), lambda j: (layer, 0, 0, j))],
        out_specs=[pl.BlockSpec((m, bn), lambda j: (0, j)),
                   pl.BlockSpec((hist, nb, bn), lambda j: (0, 0, j)), wout, wout],
        compiler_params=_cparams(1),
        name="ffn_gate_up_s",
    )(h, wg, wu, cw, cache)


CONV_ROWS = 32
HALO = 32


def _ln_silu(c, g, b):
    mu = jnp.mean(c, axis=-1, keepdims=True)
    d = c - mu
    var = jnp.mean(d * d, axis=-1, keepdims=True)
    y = d * lax.rsqrt(var + EPS)
    return jax.nn.silu(y * g + b)


def _conv_p_body(u_ref, dw_ref, db_ref, lg_ref, lb_ref, o_ref, ubuf, cbuf, *, bm, taps,
                 tiles_per_seq):
    m = pl.program_id(0)
    d = u_ref.shape[1]
    nc = d // LANES

    @pl.when(m % tiles_per_seq == 0)
    def _():
        ubuf[:, 0:HALO, :] = jnp.zeros((nc, HALO, LANES), F32)

    @pl.when(m % tiles_per_seq != 0)
    def _():
        ubuf[:, 0:HALO, :] = ubuf[:, bm:bm + HALO, :]

    for c in range(nc):
        ubuf[c, HALO:HALO + bm, :] = u_ref[:, c * LANES:(c + 1) * LANES]
    off = HALO - (taps - 1)

    def lane_chunk(c, carry):
        wv = dw_ref[c]
        bias = db_ref[c]
        for r0 in range(0, bm, CONV_ROWS):
            acc = jnp.zeros((CONV_ROWS, LANES), F32)
            for k in range(taps):
                acc = acc + ubuf[c, r0 + off + k:r0 + off + k + CONV_ROWS, :] * wv[k:k + 1, :]
            cbuf[c, r0:r0 + CONV_ROWS, :] = acc + bias
        return carry

    lax.fori_loop(0, nc, lane_chunk, 0)
    cfull = jnp.concatenate([cbuf[c] for c in range(nc)], axis=-1)
    o_ref[...] = _ln_silu(cfull, lg_ref[...], lb_ref[...]).astype(o_ref.dtype)


def _lane_chunked(x):
    l, r, d = x.shape
    return x.reshape(l, r, d // LANES, LANES).transpose(0, 2, 1, 3)


def _vec3(x):
    return x.reshape(x.shape[0], 1, x.shape[1])


def _conv_prompt(u, dw, db, lg, lb, layer, *, bm, seq_len):
    m, d = u.shape
    taps = dw.shape[1]
    nc = d // LANES
    vec = lambda: pl.BlockSpec((None, 1, d), lambda i: (layer, 0, 0))
    return pl.pallas_call(
        functools.partial(_conv_p_body, bm=bm, taps=taps, tiles_per_seq=seq_len // bm),
        out_shape=jax.ShapeDtypeStruct((m, d), BF16),
        grid=(m // bm,),
        in_specs=[pl.BlockSpec((bm, d), lambda i: (i, 0)),
                  pl.BlockSpec((None, nc, taps, LANES), lambda i: (layer, 0, 0, 0)),
                  pl.BlockSpec((None, nc, 1, LANES), lambda i: (layer, 0, 0, 0)),
                  vec(), vec()],
        out_specs=pl.BlockSpec((bm, d), lambda i: (i, 0)),
        scratch_shapes=[pltpu.VMEM((nc, HALO + bm, LANES), F32),
                        pltpu.VMEM((nc, bm, LANES), F32)],
        compiler_params=_cparams(1),
        name="conv_prompt",
    )(u, _lane_chunked(dw), _lane_chunked(_vec3(db)), _vec3(lg), _vec3(lb))


def _conv_s_body(cache_ref, u_ref, dw_ref, db_ref, lg_ref, lb_ref, o_ref, st_ref, cfull, *,
                 taps, seq, nsteps):
    c = pl.program_id(0)
    hist = taps - 1
    ext = lambda r: cache_ref[r] if r < hist else u_ref[r - hist]
    wv = dw_ref[...]
    bias = db_ref[...]
    for t in range(seq):
        acc = ext(t) * wv[0:1, :]
        for k in range(1, taps):
            acc = acc + ext(t + k) * wv[k:k + 1, :]
        cfull[c, t] = acc + bias
    for r in range(hist):
        st_ref[r] = ext(r + seq)

    @pl.when(c == nsteps - 1)
    def _():
        for t in range(seq):
            row = jnp.concatenate([cfull[cc, t] for cc in range(nsteps)], axis=-1)
            o_ref[t] = _ln_silu(row, lg_ref[...], lb_ref[...]).astype(o_ref.dtype)


def _conv_sample(cache, u, dw, db, lg, lb, layer, *, bc):
    seq, b, d = u.shape
    taps = dw.shape[1]
    hist = taps - 1
    nsteps = d // bc
    vec = lambda: pl.BlockSpec((None, 1, d), lambda i: (layer, 0, 0))
    return pl.pallas_call(
        functools.partial(_conv_s_body, taps=taps, seq=seq, nsteps=nsteps),
        out_shape=[jax.ShapeDtypeStruct((seq, b, d), BF16),
                   jax.ShapeDtypeStruct((hist, b, d), F32)],
        grid=(nsteps,),
        in_specs=[pl.BlockSpec((None, hist, b, bc), lambda i: (layer, 0, 0, i)),
                  pl.BlockSpec((seq, b, bc), lambda i: (0, 0, i)),
                  pl.BlockSpec((None, taps, bc), lambda i: (layer, 0, i)),
                  pl.BlockSpec((None, 1, bc), lambda i: (layer, 0, i)),
                  vec(), vec()],
        out_specs=[pl.BlockSpec((seq, b, d), lambda i: (0, 0, 0)),
                   pl.BlockSpec((hist, b, bc), lambda i: (0, 0, i))],
        scratch_shapes=[pltpu.VMEM((nsteps, seq, b, bc), F32)],
        compiler_params=_cparams(1),
        name="conv_sample",
    )(cache, u, dw, _vec3(db), _vec3(lg), _vec3(lb))


def _s5_abar(lam_re, lam_im, dt):
    mag = jnp.exp(lam_re * dt)
    ang = lam_im * dt
    return mag * jnp.cos(ang), mag * jnp.sin(ang)


def _s5_disc_body(are_ref, aim_ref, ldt_ref, arep_ref, airep_ref, bre_ref, bim_ref,
                  abr_ref, abi_ref, bbr_ref, bbi_ref):
    dt = jnp.exp(ldt_ref[...])
    abar_re, abar_im = _s5_abar(are_ref[...], aim_ref[...], dt)
    abr_ref[...] = abar_re
    abi_ref[...] = abar_im
    lam_re = arep_ref[...]
    lam_im = airep_ref[...]
    rep_re, rep_im = _s5_abar(lam_re, lam_im, dt)
    nr = rep_re - 1.0
    ni = rep_im
    den = lam_re * lam_re + lam_im * lam_im
    q_re = (nr * lam_re + ni * lam_im) / den
    q_im = (ni * lam_re - nr * lam_im) / den
    br = bre_ref[...]
    bi = bim_ref[...]
    bbr_ref[...] = q_re * br - q_im * bi
    bbi_ref[...] = q_re * bi + q_im * br


def _s5_discretise(a_re, a_im, log_dt, b_re, b_im):
    ns, g, p = a_re.shape
    rows = ns * g
    two = lambda x: x.reshape(rows, p)
    rep = lambda x: jnp.repeat(x.reshape(rows, p), GROUP_SIZE, axis=1)
    wide = lambda x: x.reshape(rows, p * GROUP_SIZE)
    outs = pl.pallas_call(
        _s5_disc_body,
        out_shape=[jax.ShapeDtypeStruct((rows, p), F32)] * 2
        + [jax.ShapeDtypeStruct((rows, p * GROUP_SIZE), F32)] * 2,
        name="s5_discretise",
    )(two(a_re), two(a_im), log_dt.reshape(rows, 1), rep(a_re), rep(a_im), wide(b_re), wide(b_im))
    abr, abi, bbr, bbi = outs
    return (abr.reshape(ns, g, p), abi.reshape(ns, g, p),
            bbr.reshape(ns, g, p, GROUP_SIZE), bbi.reshape(ns, g, p, GROUP_SIZE))


SLAB_GROUPS = 16
SLAB = SLAB_GROUPS * GROUP_SIZE


def _s5_expand_body(xbr_ref, xbi_ref, xcr_ref, xci_ref, bm_ref, cm_ref, *, p):
    gpt = LANES // p
    tiles_slab = SLAB_GROUPS // gpt
    rg = lax.broadcasted_iota(jnp.int32, (SLAB, LANES), 0) // GROUP_SIZE
    lg = lax.broadcasted_iota(jnp.int32, (SLAB, LANES), 1) // p
    for half, x_ref in enumerate((xbr_ref, xbi_ref)):
        x = x_ref[...]
        for a in range(tiles_slab):
            col = (half * tiles_slab + a) * LANES
            bm_ref[:, col:col + LANES] = jnp.where(rg == gpt * a + lg, x, 0.0).astype(BF16)
    cg = lax.broadcasted_iota(jnp.int32, (LANES, SLAB), 1) // GROUP_SIZE
    rc = lax.broadcasted_iota(jnp.int32, (LANES, SLAB), 0) // p
    for half, x_ref in enumerate((xcr_ref, xci_ref)):
        x = x_ref[...]
        for a in range(tiles_slab):
            row = (half * tiles_slab + a) * LANES
            cm_ref[row:row + LANES, :] = jnp.where(cg == gpt * a + rc, x, 0.0).astype(BF16)


def _s5_block_weights(bb_re, bb_im, c_re, c_im):
    ns, g, p, gs = bb_re.shape
    s = g // SLAB_GROUPS
    n = ns * s
    gpt = LANES // p
    bcomp = lambda x: jnp.tile(
        x.reshape(n, SLAB_GROUPS, p, gs).transpose(0, 1, 3, 2).reshape(n, SLAB, p), (1, 1, gpt))
    ccomp = lambda x: jnp.tile(
        x.reshape(n, SLAB_GROUPS, gs, p).transpose(0, 3, 1, 2).reshape(n, p, SLAB), (1, gpt, 1))
    width = 2 * SLAB_GROUPS * p
    bspec = pl.BlockSpec((None, SLAB, LANES), lambda i: (i, 0, 0))
    cspec = pl.BlockSpec((None, LANES, SLAB), lambda i: (i, 0, 0))
    bmat, cmat = pl.pallas_call(
        functools.partial(_s5_expand_body, p=p),
        out_shape=[jax.ShapeDtypeStruct((n, SLAB, width), BF16),
                   jax.ShapeDtypeStruct((n, width, SLAB), BF16)],
        grid=(n,),
        in_specs=[bspec, bspec, cspec, cspec],
        out_specs=[pl.BlockSpec((None, SLAB, width), lambda i: (i, 0, 0)),
                   pl.BlockSpec((None, width, SLAB), lambda i: (i, 0, 0))],
        compiler_params=_cparams(1),
        name="s5_expand",
    )(bcomp(bb_re), bcomp(bb_im), ccomp(c_re), ccomp(-c_im))
    return bmat.reshape(ns, s, SLAB, width), cmat.reshape(ns, s, width, SLAB)


def _s5_pitch(rows):
    return rows + SUBLANES // 2


def _s5_project_in(u, bmat_ref, z, *, rows, pitch, n_slab, tiles_half):
    tiles_slab = tiles_half // n_slab
    for s in range(n_slab):
        bu = jnp.dot(u[:, s * SLAB:(s + 1) * SLAB].astype(BF16), bmat_ref[s],
                     preferred_element_type=F32)
        for half in range(2):
            for q in range(tiles_slab):
                j = half * tiles_half + s * tiles_slab + q
                col = (half * tiles_slab + q) * LANES
                z[j * pitch:j * pitch + rows, :] = bu[:, col:col + LANES]


def _s5_project_out(u, z, cmat_ref, d_ref, store, *, rows, pitch, n_slab, tiles_half):
    tiles_slab = tiles_half // n_slab
    for s in range(n_slab):
        parts = []
        for half in range(2):
            for q in range(tiles_slab):
                j = half * tiles_half + s * tiles_slab + q
                parts.append(z[j * pitch:j * pitch + rows, :].astype(BF16))
        st = jnp.concatenate(parts, axis=-1)
        y = jnp.dot(st, cmat_ref[s], preferred_element_type=F32)
        sl = slice(s * SLAB, (s + 1) * SLAB)
        y = y + d_ref[:, sl] * u[:, sl]
        store(sl, jax.nn.gelu(y))


def _s5_step(z, t, state, abar, *, pitch, tiles_half):
    nv = tiles_half // SUBLANES
    new = []
    for k in range(nv):
        idx_re = pl.ds(k * SUBLANES * pitch + t, SUBLANES, stride=pitch)
        idx_im = pl.ds((tiles_half + k * SUBLANES) * pitch + t, SUBLANES, stride=pitch)
        ar, ai = abar[k]
        sr, si = state[k]
        nr = ar * sr - ai * si + z[idx_re, :]
        ni = ar * si + ai * sr + z[idx_im, :]
        z[idx_re, :] = nr
        z[idx_im, :] = ni
        new.append((nr, ni))
    return tuple(new)


def _s5_p_body(u_ref, bmat_ref, cmat_ref, abr_ref, abi_ref, d_ref, o_ref, sre_ref, sim_ref,
               z, st_re, st_im, *, rows, tiles_per_seq, n_slab, tiles_half):
    m = pl.program_id(0)
    pitch = _s5_pitch(rows)
    nv = tiles_half // SUBLANES
    u = u_ref[...]
    _s5_project_in(u, bmat_ref, z, rows=rows, pitch=pitch, n_slab=n_slab, tiles_half=tiles_half)

    @pl.when(m % tiles_per_seq == 0)
    def _():
        st_re[...] = jnp.zeros(st_re.shape, F32)
        st_im[...] = jnp.zeros(st_im.shape, F32)

    vsl = lambda k: slice(k * SUBLANES, (k + 1) * SUBLANES)
    abar = tuple((abr_ref[vsl(k), :], abi_ref[vsl(k), :]) for k in range(nv))
    state0 = tuple((st_re[vsl(k), :], st_im[vsl(k), :]) for k in range(nv))

    def step(t, state):
        return _s5_step(z, t, state, abar, pitch=pitch, tiles_half=tiles_half)

    state = lax.fori_loop(0, rows, step, state0, unroll=4)
    for k in range(nv):
        st_re[vsl(k), :] = state[k][0]
        st_im[vsl(k), :] = state[k][1]
    sre_ref[...] = st_re[...]
    sim_ref[...] = st_im[...]

    def store(sl, val):
        o_ref[:, sl] = val.astype(o_ref.dtype)

    _s5_project_out(u, z, cmat_ref, d_ref, store, rows=rows, pitch=pitch, n_slab=n_slab,
                    tiles_half=tiles_half)


def _s5_prompt(u, bmat, cmat, abar_re, abar_im, dskip, *, rows, seq_len):
    m, d = u.shape
    n_slab = bmat.shape[0]
    tiles_half = abar_re.shape[0]
    pitch = _s5_pitch(rows)
    tiles_per_seq = seq_len // rows
    n_seq = m // seq_len
    st_spec = pl.BlockSpec((None, tiles_half, LANES), lambda i: (i // tiles_per_seq, 0, 0))
    full = lambda a: pl.BlockSpec(a.shape, lambda i: (0,) * a.ndim)
    return pl.pallas_call(
        functools.partial(_s5_p_body, rows=rows, tiles_per_seq=tiles_per_seq, n_slab=n_slab,
                          tiles_half=tiles_half),
        out_shape=[jax.ShapeDtypeStruct((m, d), BF16),
                   jax.ShapeDtypeStruct((n_seq, tiles_half, LANES), F32),
                   jax.ShapeDtypeStruct((n_seq, tiles_half, LANES), F32)],
        grid=(m // rows,),
        in_specs=[pl.BlockSpec((rows, d), lambda i: (i, 0)),
                  full(bmat), full(cmat), full(abar_re), full(abar_im), full(dskip)],
        out_specs=[pl.BlockSpec((rows, d), lambda i: (i, 0)), st_spec, st_spec],
        scratch_shapes=[pltpu.VMEM((2 * tiles_half * pitch, LANES), F32),
                        pltpu.VMEM((tiles_half, LANES), F32),
                        pltpu.VMEM((tiles_half, LANES), F32)],
        compiler_params=_cparams(1),
        name="s5_prompt",
    )(u, bmat, cmat, abar_re, abar_im, dskip)


def _s5_s_body(u_ref, h0r_ref, h0i_ref, bmat_ref, cmat_ref, abr_ref, abi_ref, d_ref,
               o_ref, sre_ref, sim_ref, z, zh, *, nb, seq, n_slab, tiles_half):
    rows = nb * seq
    pitch = _s5_pitch(rows)
    hp = _s5_pitch(nb)
    nv = tiles_half // SUBLANES
    d = u_ref.shape[2]
    u = u_ref[...].reshape(rows, d)
    _s5_project_in(u, bmat_ref, z, rows=rows, pitch=pitch, n_slab=n_slab, tiles_half=tiles_half)
    for j in range(tiles_half):
        cs = slice(j * LANES, (j + 1) * LANES)
        zh[j * hp:j * hp + nb, :] = h0r_ref[:, cs]
        zh[(tiles_half + j) * hp:(tiles_half + j) * hp + nb, :] = h0i_ref[:, cs]

    vsl = lambda k: slice(k * SUBLANES, (k + 1) * SUBLANES)
    abar = tuple((abr_ref[vsl(k), :], abi_ref[vsl(k), :]) for k in range(nv))

    def one_seq(b, carry):
        state = []
        for k in range(nv):
            idx_re = pl.ds(k * SUBLANES * hp + b, SUBLANES, stride=hp)
            idx_im = pl.ds((tiles_half + k * SUBLANES) * hp + b, SUBLANES, stride=hp)
            state.append((zh[idx_re, :], zh[idx_im, :]))
        state = tuple(state)
        for t in range(seq):
            state = _s5_step(z, t * nb + b, state, abar, pitch=pitch, tiles_half=tiles_half)
        for k in range(nv):
            idx_re = pl.ds(k * SUBLANES * hp + b, SUBLANES, stride=hp)
            idx_im = pl.ds((tiles_half + k * SUBLANES) * hp + b, SUBLANES, stride=hp)
            zh[idx_re, :] = state[k][0]
            zh[idx_im, :] = state[k][1]
        return carry

    lax.fori_loop(0, nb, one_seq, 0)
    for j in range(tiles_half):
        cs = slice(j * LANES, (j + 1) * LANES)
        sre_ref[:, cs] = zh[j * hp:j * hp + nb, :]
        sim_ref[:, cs] = zh[(tiles_half + j) * hp:(tiles_half + j) * hp + nb, :]

    def store(sl, val):
        o_ref[:, :, sl] = val.reshape(seq, nb, val.shape[1]).astype(o_ref.dtype)

    _s5_project_out(u, z, cmat_ref, d_ref, store, rows=rows, pitch=pitch, n_slab=n_slab,
                    tiles_half=tiles_half)


def _s5_sample(u, h0_re, h0_im, bmat, cmat, abar_re, abar_im, dskip, *, nb):
    seq, b, d = u.shape
    nstate = h0_re.shape[1]
    n_slab = bmat.shape[0]
    tiles_half = abar_re.shape[0]
    pitch = _s5_pitch(nb * seq)
    hp = _s5_pitch(nb)
    full = lambda a: pl.BlockSpec(a.shape, lambda i: (0,) * a.ndim)
    st_spec = pl.BlockSpec((nb, nstate), lambda i: (i, 0))
    u_spec = pl.BlockSpec((seq, nb, d), lambda i: (0, i, 0))
    return pl.pallas_call(
        functools.partial(_s5_s_body, nb=nb, seq=seq, n_slab=n_slab, tiles_half=tiles_half),
        out_shape=[jax.ShapeDtypeStruct((seq, b, d), BF16),
                   jax.ShapeDtypeStruct(h0_re.shape, F32),
                   jax.ShapeDtypeStruct(h0_im.shape, F32)],
        grid=(b // nb,),
        in_specs=[u_spec, st_spec, st_spec,
                  full(bmat), full(cmat), full(abar_re), full(abar_im), full(dskip)],
        out_specs=[u_spec, st_spec, st_spec],
        scratch_shapes=[pltpu.VMEM((2 * tiles_half * pitch, LANES), F32),
                        pltpu.VMEM((2 * tiles_half * hp, LANES), F32)],
        compiler_params=_cparams(1),
        name="s5_sample",
    )(u, h0_re, h0_im, bmat, cmat, abar_re, abar_im, dskip)


def _hn_dtype(i):
    return F32 if (i % N_MIXERS == 1) else BF16


def _run_sample(x, states, p):
    seq, nb, d = x.shape
    m = seq * nb
    depth = p["norm_mix"].shape[0]
    conv_st, ssm_re_st, ssm_im_st, ffn_st = states
    new_conv, new_re, new_im, new_ffn, wbf = [], [], [], [], []
    x = x.reshape(m, d)
    hn = _rmsnorm(x, p["norm_mix"][0], _hn_dtype(0), m)
    y = None
    for i in range(depth):
        j = i // N_MIXERS
        w = {}
        if i % N_MIXERS == 0:
            u, w["in_a"], w["in_b"] = _mm(hn, p["conv_w_in"], glu=True, bm=m, bn=512, layer=j)
            c, st = _conv_sample(conv_st, u.reshape(seq, nb, d), p["conv_dw"], p["conv_dw_b"],
                                 p["conv_ln_g"], p["conv_ln_b"], j, bc=256)
            new_conv.append(st)
            x, hn, w["out"] = _mm(c.reshape(m, d), p["conv_w_out"], glu=False, bm=m, bn=512,
                                  layer=j, resid=x, gamma=p["norm_ffn"][i], hn_dtype=BF16)
        else:
            v, s_re, s_im = _s5_sample(hn.reshape(seq, nb, d), ssm_re_st[j].reshape(nb, -1),
                                       ssm_im_st[j].reshape(nb, -1), p["s5_bmat"][j],
                                       p["s5_cmat"][j], p["s5_abar_re"][j], p["s5_abar_im"][j],
                                       p["ssm_D"][j].reshape(1, d), nb=32)
            new_re.append(s_re.reshape(ssm_re_st.shape[1:]))
            new_im.append(s_im.reshape(ssm_im_st.shape[1:]))
            x, hn, w["glu_a"], w["glu_b"] = _mm(
                v.reshape(m, d), p["ssm_w_glu"], glu=True, bm=m, bn=512, layer=j, resid=x,
                gamma=p["norm_ffn"][i], hn_dtype=BF16)
        act, st, w["gate"], w["up"] = _ffn1_sample(hn, p["ffn_w_gate"], p["ffn_w_up"],
                                                   p["ffn_conv"], ffn_st, i, bn=512, seq=seq)
        new_ffn.append(st)
        last = i == depth - 1
        gamma = p["norm_final"] if last else p["norm_mix"][i + 1]
        outs = _mm(act, p["ffn_w_down"], glu=False, bm=m, bn=256, layer=i, resid=x, gamma=gamma,
                   hn_dtype=F32 if last else _hn_dtype(i + 1), write_x=not last)
        if last:
            y, w["down"] = outs
        else:
            x, hn, w["down"] = outs
        wbf.append(w)
    return (y.reshape(seq, nb, d), jnp.stack(new_conv), jnp.stack(new_re), jnp.stack(new_im),
            jnp.stack(new_ffn)), wbf


def _run_prompt(x, p, wbf, *, seq_len, bm):
    m, d = x.shape
    n_seq = m // seq_len
    depth = p["norm_mix"].shape[0]
    new_conv, new_re, new_im, new_ffn = [], [], [], []
    hn = _rmsnorm(x, p["norm_mix"][0], _hn_dtype(0), bm)
    y = None
    for i in range(depth):
        j = i // N_MIXERS
        w = wbf[i]
        if i % N_MIXERS == 0:
            (u,) = _mm(hn, (w["in_a"], w["in_b"]), glu=True, bm=bm, bn=1024)
            cs = p["conv_dw"].shape[1] - 1
            new_conv.append(u.reshape(n_seq, seq_len, d)[:, -cs:])
            c = _conv_prompt(u, p["conv_dw"], p["conv_dw_b"], p["conv_ln_g"], p["conv_ln_b"], j,
                             bm=256, seq_len=seq_len)
            x, hn = _mm(c, (w["out"],), glu=False, bm=bm, bn=1024, resid=x,
                        gamma=p["norm_ffn"][i], hn_dtype=BF16)
        else:
            v, s_re, s_im = _s5_prompt(hn, p["s5_bmat"][j], p["s5_cmat"][j], p["s5_abar_re"][j],
                                       p["s5_abar_im"][j], p["ssm_D"][j].reshape(1, d), rows=256,
                                       seq_len=seq_len)
            g, pdim = p["ssm_A_re"].shape[1:]
            new_re.append(s_re.reshape(n_seq, g, pdim))
            new_im.append(s_im.reshape(n_seq, g, pdim))
            x, hn = _mm(v, (w["glu_a"], w["glu_b"]), glu=True, bm=bm, bn=512, resid=x,
                        gamma=p["norm_ffn"][i], hn_dtype=BF16)
        act, tail = _ffn1_prompt(hn, w["gate"], w["up"], p["ffn_conv"], i, bm=seq_len, bn=512,
                                 seq_len=seq_len)
        fs = p["ffn_conv"].shape[1] - 1
        new_ffn.append(tail[:, -fs:])
        last = i == depth - 1
        gamma = p["norm_final"] if last else p["norm_mix"][i + 1]
        outs = _mm(act, (w["down"],), glu=False, bm=512, bn=512, resid=x, gamma=gamma,
                   hn_dtype=F32 if last else _hn_dtype(i + 1), write_x=not last)
        if last:
            (y,) = outs
        else:
            x, hn = outs
    return (y, jnp.stack(new_conv), jnp.stack(new_re), jnp.stack(new_im), jnp.stack(new_ffn))


def kernel(x_prompt, x_sample, state_conv, state_ssm_re, state_ssm_im, state_ffn, norm_mix, norm_ffn, norm_final, conv_w_in, conv_dw, conv_dw_b, conv_ln_g, conv_ln_b, conv_w_out, ssm_A_re, ssm_A_im, ssm_log_dt, ssm_B_re, ssm_B_im, ssm_C_re, ssm_C_im, ssm_D, ssm_w_glu, ffn_w_gate, ffn_w_up, ffn_conv, ffn_w_down):
    bp, sp, d = x_prompt.shape
    abr, abi, bbr, bbi = _s5_discretise(ssm_A_re, ssm_A_im, ssm_log_dt, ssm_B_re, ssm_B_im)
    ns, g, pdim = ssm_A_re.shape
    bmat, cmat = _s5_block_weights(bbr, bbi, ssm_C_re, ssm_C_im)
    tiles_half = g * pdim // LANES
    p = dict(
        norm_mix=norm_mix, norm_ffn=norm_ffn, norm_final=norm_final,
        conv_w_in=conv_w_in, conv_dw=conv_dw, conv_dw_b=conv_dw_b,
        conv_ln_g=conv_ln_g, conv_ln_b=conv_ln_b, conv_w_out=conv_w_out,
        ssm_A_re=ssm_A_re, ssm_D=ssm_D, ssm_w_glu=ssm_w_glu,
        s5_bmat=bmat, s5_cmat=cmat,
        s5_abar_re=abr.reshape(ns, tiles_half, LANES), s5_abar_im=abi.reshape(ns, tiles_half, LANES),
        ffn_w_gate=ffn_w_gate, ffn_w_up=ffn_w_up, ffn_conv=ffn_conv, ffn_w_down=ffn_w_down,
    )
    tm = lambda a: jnp.swapaxes(a, -3, -2)
    (y_s, conv_s, re_s, im_s, ffn_s), wbf = _run_sample(
        tm(x_sample), (tm(state_conv), state_ssm_re, state_ssm_im, tm(state_ffn)), p)
    y_p, conv_p, re_p, im_p, ffn_p = _run_prompt(
        x_prompt.reshape(bp * sp, d), p, wbf, seq_len=sp, bm=1024)
    return (y_p.reshape(bp, sp, d), tm(y_s), conv_p, tm(conv_s), re_p, im_p, re_s, im_s, ffn_p,
            tm(ffn_s))
```

```python
import functools

import jax
import jax.numpy as jnp
from jax import lax
from jax.experimental import pallas as pl
from jax.experimental.pallas import tpu as pltpu

F32 = jnp.float32
BF16 = jnp.bfloat16

EPS = 1e-6
N_MIXERS = 2
GROUP_SIZE = 16
LANES = 128
SUBLANES = 8
VMEM_LIMIT = 56 * 1024 * 1024
ROW_CHUNK = 256


def _cparams(n_axes):
    return pltpu.CompilerParams(
        dimension_semantics=("arbitrary",) * n_axes, vmem_limit_bytes=VMEM_LIMIT)


def _rms_scale(ss, n):
    return lax.rsqrt(ss / n + EPS)


def _rmsnorm_body(x_ref, g_ref, o_ref):
    x = x_ref[...]
    ss = jnp.sum(x * x, axis=-1, keepdims=True)
    o_ref[...] = ((x * _rms_scale(ss, x.shape[-1])) * g_ref[...]).astype(o_ref.dtype)


def _rmsnorm(x, g, out_dtype, bm):
    m, d = x.shape
    return pl.pallas_call(
        _rmsnorm_body,
        out_shape=jax.ShapeDtypeStruct((m, d), out_dtype),
        grid=(m // bm,),
        in_specs=[pl.BlockSpec((bm, d), lambda i: (i, 0)),
                  pl.BlockSpec((1, d), lambda i: (0, 0))],
        out_specs=pl.BlockSpec((bm, d), lambda i: (i, 0)),
        compiler_params=_cparams(1),
        name="rmsnorm",
    )(x, g.reshape(1, d))


def _mm_body(*refs, glu, has_res, write_x, emit, nn, bn, n_out):
    it = iter(refs)
    lhs_ref = next(it)
    wa_ref = next(it)
    wb_ref = next(it) if glu else None
    res_ref = next(it) if has_res else None
    g_ref = next(it) if has_res else None
    xo_ref = next(it) if write_x else None
    hn_ref = next(it) if has_res else None
    wao_ref = next(it) if emit else None
    wbo_ref = next(it) if (emit and glu) else None
    rowbuf = next(it) if has_res else None
    ss_ref = next(it) if has_res else None

    n = pl.program_id(1)
    bm = lhs_ref.shape[0]
    if emit:
        wao_ref[...] = wa_ref[...].astype(BF16)
        wa_ref = wao_ref
        if glu:
            wbo_ref[...] = wb_ref[...].astype(BF16)
            wb_ref = wbo_ref
    if has_res:
        @pl.when(n == 0)
        def _():
            ss_ref[...] = jnp.zeros(ss_ref.shape, F32)
    for r0 in range(0, bm, ROW_CHUNK):
        rs = slice(r0, min(r0 + ROW_CHUNK, bm))
        x = lhs_ref[rs, :]
        val = jnp.dot(x, wa_ref[...], preferred_element_type=F32)
        if glu:
            gate = jnp.dot(x, wb_ref[...], preferred_element_type=F32)
            val = val * jax.nn.sigmoid(gate)
        if has_res:
            val = res_ref[rs, :] + val
        if write_x:
            xo_ref[rs, :] = val
        if has_res:
            rowbuf[n, rs, :] = val
            ss_ref[rs, :] = ss_ref[rs, :] + jnp.sum(val * val, axis=-1, keepdims=True)

    if has_res:
        @pl.when(n == nn - 1)
        def _():
            scale = _rms_scale(ss_ref[...], n_out)
            for j in range(nn):
                sl = slice(j * bn, (j + 1) * bn)
                hn_ref[:, sl] = ((rowbuf[j] * scale) * g_ref[:, sl]).astype(hn_ref.dtype)


def _mm(lhs, w, *, glu, bm, bn, layer=None, resid=None, gamma=None, hn_dtype=None, write_x=True):
    m, k = lhs.shape
    emit = layer is not None
    if emit:
        assert m == bm
        n_out = w.shape[2] // (2 if glu else 1)
    else:
        n_out = w[0].shape[1]
    nn = n_out // bn
    has_res = resid is not None
    in_specs = [pl.BlockSpec((bm, k), lambda i, j: (i, 0))]
    args = [lhs]
    if emit:
        in_specs.append(pl.BlockSpec((None, k, bn), lambda i, j: (layer, 0, j)))
        args.append(w)
        if glu:
            in_specs.append(pl.BlockSpec((None, k, bn), lambda i, j: (layer, 0, j + nn)))
            args.append(w)
    else:
        for wi in w:
            in_specs.append(pl.BlockSpec((k, bn), lambda i, j: (0, j)))
            args.append(wi)
    out_shape, out_specs, scratch = [], [], []
    if has_res:
        in_specs += [pl.BlockSpec((bm, bn), lambda i, j: (i, j)),
                     pl.BlockSpec((1, n_out), lambda i, j: (0, 0))]
        args += [resid, gamma.reshape(1, n_out)]
    if write_x:
        out_shape.append(jax.ShapeDtypeStruct((m, n_out), F32))
        out_specs.append(pl.BlockSpec((bm, bn), lambda i, j: (i, j)))
    if has_res:
        out_shape.append(jax.ShapeDtypeStruct((m, n_out), hn_dtype))
        out_specs.append(pl.BlockSpec((bm, n_out), lambda i, j: (i, 0)))
        scratch += [pltpu.VMEM((nn, bm, bn), F32), pltpu.VMEM((bm, 1), F32)]
    if emit:
        for _ in range(2 if glu else 1):
            out_shape.append(jax.ShapeDtypeStruct((k, n_out), BF16))
            out_specs.append(pl.BlockSpec((k, bn), lambda i, j: (0, j)))
    return pl.pallas_call(
        functools.partial(_mm_body, glu=glu, has_res=has_res, write_x=write_x, emit=emit,
                          nn=nn, bn=bn, n_out=n_out),
        out_shape=out_shape,
        grid=(m // bm, nn),
        in_specs=in_specs,
        out_specs=out_specs,
        scratch_shapes=scratch,
        compiler_params=_cparams(2),
        name="mm_glu" if glu else "mm_lin",
    )(*args)


def _mmr_body(*refs, glu, write_x, bn, n_out):
    it = iter(refs)
    lhs_ref = next(it)
    wa_ref = next(it)
    wb_ref = next(it) if glu else None
    res_ref = next(it)
    g_ref = next(it)
    xo_ref = next(it) if write_x else None
    hn_ref = next(it)
    bm = lhs_ref.shape[0]
    for r0 in range(0, bm, ROW_CHUNK):
        rs = slice(r0, r0 + ROW_CHUNK)
        x = lhs_ref[rs, :]
        ss = jnp.zeros((ROW_CHUNK, 1), F32)
        vals = []
        for j in range(n_out // bn):
            sl = slice(j * bn, (j + 1) * bn)
            val = jnp.dot(x, wa_ref[:, sl], preferred_element_type=F32)
            if glu:
                gate = jnp.dot(x, wb_ref[:, sl], preferred_element_type=F32)
                val = val * jax.nn.sigmoid(gate)
            val = res_ref[rs, sl] + val
            if write_x:
                xo_ref[rs, sl] = val
            ss = ss + jnp.sum(val * val, axis=-1, keepdims=True)
            vals.append(val)
        scale = _rms_scale(ss, n_out)
        for j, val in enumerate(vals):
            sl = slice(j * bn, (j + 1) * bn)
            hn_ref[rs, sl] = ((val * scale) * g_ref[:, sl]).astype(hn_ref.dtype)


def _mm_resident(lhs, w, *, glu, bm, bn, resid, gamma, hn_dtype, write_x=True):
    m, k = lhs.shape
    n_out = w[0].shape[1]
    row = lambda width: pl.BlockSpec((bm, width), lambda i: (i, 0))
    in_specs = [row(k)] + [pl.BlockSpec((k, n_out), lambda i: (0, 0)) for _ in w]
    in_specs += [row(n_out), pl.BlockSpec((1, n_out), lambda i: (0, 0))]
    out_shape, out_specs = [], []
    if write_x:
        out_shape.append(jax.ShapeDtypeStruct((m, n_out), F32))
        out_specs.append(row(n_out))
    out_shape.append(jax.ShapeDtypeStruct((m, n_out), hn_dtype))
    out_specs.append(row(n_out))
    return pl.pallas_call(
        functools.partial(_mmr_body, glu=glu, write_x=write_x, bn=bn, n_out=n_out),
        out_shape=out_shape,
        grid=(m // bm,),
        in_specs=in_specs,
        out_specs=out_specs,
        compiler_params=_cparams(1),
        name="mmr_glu" if glu else "mmr_lin",
    )(lhs, *w, resid, gamma.reshape(1, n_out))


def _shift_rows(g, tail, k):
    r = pltpu.roll(g, k, 0)
    t = pltpu.roll(tail, k, 0)
    i8 = lax.broadcasted_iota(jnp.int32, tail.shape, 0)
    head = jnp.where(i8 < k, t, r[0:SUBLANES, :])
    return jnp.concatenate([head, r[SUBLANES:, :]], axis=0)


def _ffn1_prompt_body(h_ref, wg_ref, wu_ref, cw_ref, act_ref, tail_ref, carry, *, bm, tiles_per_seq):
    m = pl.program_id(0)
    n = pl.program_id(1)
    cw = cw_ref[...]
    taps = cw.shape[0]

    @pl.when(m % tiles_per_seq == 0)
    def _():
        carry[n] = jnp.zeros(carry.shape[1:], F32)

    tail = carry[n]
    for r0 in range(0, bm, ROW_CHUNK):
        rs = slice(r0, r0 + ROW_CHUNK)
        h = h_ref[rs, :]
        g = jnp.dot(h, wg_ref[...], preferred_element_type=F32)
        up = jnp.dot(h, wu_ref[...], preferred_element_type=F32)
        gc = g * cw[taps - 1:taps, :]
        for k in range(taps - 1):
            gc = gc + _shift_rows(g, tail, taps - 1 - k) * cw[k:k + 1, :]
        act_ref[rs, :] = (jax.nn.silu(gc) * up).astype(act_ref.dtype)
        tail = g[ROW_CHUNK - SUBLANES:, :]
    carry[n] = tail
    tail_ref[...] = tail


def _ffn1_prompt(h, wg, wu, cw, layer, *, bm, bn, seq_len):
    m, k = h.shape
    f = wg.shape[1]
    nn = f // bn
    tiles_per_seq = seq_len // bm
    act, tails = pl.pallas_call(
        functools.partial(_ffn1_prompt_body, bm=bm, tiles_per_seq=tiles_per_seq),
        out_shape=[jax.ShapeDtypeStruct((m, f), BF16),
                   jax.ShapeDtypeStruct((m // bm, SUBLANES, f), F32)],
        grid=(m // bm, nn),
        in_specs=[pl.BlockSpec((bm, k), lambda i, j: (i, 0)),
                  pl.BlockSpec((k, bn), lambda i, j: (0, j)),
                  pl.BlockSpec((k, bn), lambda i, j: (0, j)),
                  pl.BlockSpec((None, cw.shape[1], bn), lambda i, j: (layer, 0, j))],
        out_specs=[pl.BlockSpec((bm, bn), lambda i, j: (i, j)),
                   pl.BlockSpec((None, SUBLANES, bn), lambda i, j: (i, 0, j))],
        scratch_shapes=[pltpu.VMEM((nn, SUBLANES, bn), F32)],
        compiler_params=_cparams(2),
        name="ffn_gate_up",
    )(h, wg, wu, cw)
    return act, tails[tiles_per_seq - 1::tiles_per_seq]


def _ffn1_sample_body(h_ref, wg_ref, wu_ref, cw_ref, c_ref, act_ref, st_ref, wgo_ref, wuo_ref, *,
                      nb, seq):
    wgo_ref[...] = wg_ref[...].astype(BF16)
    wuo_ref[...] = wu_ref[...].astype(BF16)
    h = h_ref[...]
    g = jnp.dot(h, wgo_ref[...], preferred_element_type=F32)
    up = jnp.dot(h, wuo_ref[...], preferred_element_type=F32)
    cw = cw_ref[...]
    taps = cw.shape[0]
    hist = [c_ref[r] for r in range(taps - 1)] + [g[t * nb:(t + 1) * nb, :] for t in range(seq)]
    for t in range(seq):
        gc = hist[t] * cw[0:1, :]
        for k in range(1, taps):
            gc = gc + hist[t + k] * cw[k:k + 1, :]
        act_ref[t * nb:(t + 1) * nb, :] = (
            jax.nn.silu(gc) * up[t * nb:(t + 1) * nb, :]).astype(act_ref.dtype)
    for r in range(taps - 1):
        st_ref[r] = hist[seq + r]


def _ffn1_sample(h, wg, wu, cw, cache, layer, *, bn, seq):
    m, k = h.shape
    f = wg.shape[2]
    nb = m // seq
    hist = cw.shape[1] - 1
    wspec = pl.BlockSpec((None, k, bn), lambda j: (layer, 0, j))
    wout = pl.BlockSpec((k, bn), lambda j: (0, j))
    return pl.pallas_call(
        functools.partial(_ffn1_sample_body, nb=nb, seq=seq),
        out_shape=[jax.ShapeDtypeStruct((m, f), BF16),
                   jax.ShapeDtypeStruct((hist, nb, f), F32),
                   jax.ShapeDtypeStruct((k, f), BF16),
                   jax.ShapeDtypeStruct((k, f), BF16)],
        grid=(f // bn,),
        in_specs=[pl.BlockSpec((m, k), lambda j: (0, 0)), wspec, wspec,
                  pl.BlockSpec((None, cw.shape[1], bn), lambda j: (layer, 0, j)),
                  pl.BlockSpec((None, hist, nb, bn), lambda j: (layer, 0, 0, j))],
        out_specs=[pl.BlockSpec((m, bn), lambda j: (0, j)),
                   pl.BlockSpec((hist, nb, bn), lambda j: (0, 0, j)), wout, wout],
        compiler_params=_cparams(1),
        name="ffn_gate_up_s",
    )(h, wg, wu, cw, cache)


CONV_ROWS = 32
HALO = 32


def _ln_silu(c, g, b):
    mu = jnp.mean(c, axis=-1, keepdims=True)
    d = c - mu
    var = jnp.mean(d * d, axis=-1, keepdims=True)
    y = d * lax.rsqrt(var + EPS)
    return jax.nn.silu(y * g + b)


def _conv_p_body(u_ref, dw_ref, db_ref, lg_ref, lb_ref, o_ref, ubuf, cbuf, *, bm, taps,
                 tiles_per_seq):
    m = pl.program_id(0)
    d = u_ref.shape[1]
    nc = d // LANES

    @pl.when(m % tiles_per_seq == 0)
    def _():
        ubuf[:, 0:HALO, :] = jnp.zeros((nc, HALO, LANES), F32)

    @pl.when(m % tiles_per_seq != 0)
    def _():
        ubuf[:, 0:HALO, :] = ubuf[:, bm:bm + HALO, :]

    for c in range(nc):
        ubuf[c, HALO:HALO + bm, :] = u_ref[:, c * LANES:(c + 1) * LANES]
    off = HALO - (taps - 1)

    def lane_chunk(c, carry):
        wv = dw_ref[c]
        bias = db_ref[c]
        for r0 in range(0, bm, CONV_ROWS):
            acc = jnp.zeros((CONV_ROWS, LANES), F32)
            for k in range(taps):
                acc = acc + ubuf[c, r0 + off + k:r0 + off + k + CONV_ROWS, :] * wv[k:k + 1, :]
            cbuf[c, r0:r0 + CONV_ROWS, :] = acc + bias
        return carry

    lax.fori_loop(0, nc, lane_chunk, 0)
    cfull = jnp.concatenate([cbuf[c] for c in range(nc)], axis=-1)
    o_ref[...] = _ln_silu(cfull, lg_ref[...], lb_ref[...]).astype(o_ref.dtype)


def _lane_chunked(x):
    l, r, d = x.shape
    return x.reshape(l, r, d // LANES, LANES).transpose(0, 2, 1, 3)


def _vec3(x):
    return x.reshape(x.shape[0], 1, x.shape[1])


def _conv_prompt(u, dw, db, lg, lb, layer, *, bm, seq_len):
    m, d = u.shape
    taps = dw.shape[1]
    nc = d // LANES
    vec = lambda: pl.BlockSpec((None, 1, d), lambda i: (layer, 0, 0))
    return pl.pallas_call(
        functools.partial(_conv_p_body, bm=bm, taps=taps, tiles_per_seq=seq_len // bm),
        out_shape=jax.ShapeDtypeStruct((m, d), BF16),
        grid=(m // bm,),
        in_specs=[pl.BlockSpec((bm, d), lambda i: (i, 0)),
                  pl.BlockSpec((None, nc, taps, LANES), lambda i: (layer, 0, 0, 0)),
                  pl.BlockSpec((None, nc, 1, LANES), lambda i: (layer, 0, 0, 0)),
                  vec(), vec()],
        out_specs=pl.BlockSpec((bm, d), lambda i: (i, 0)),
        scratch_shapes=[pltpu.VMEM((nc, HALO + bm, LANES), F32),
                        pltpu.VMEM((nc, bm, LANES), F32)],
        compiler_params=_cparams(1),
        name="conv_prompt",
    )(u, _lane_chunked(dw), _lane_chunked(_vec3(db)), _vec3(lg), _vec3(lb))


def _conv_s_body(cache_ref, u_ref, dw_ref, db_ref, lg_ref, lb_ref, o_ref, st_ref, cfull, *,
                 taps, seq, nsteps):
    c = pl.program_id(0)
    hist = taps - 1
    ext = lambda r: cache_ref[r] if r < hist else u_ref[r - hist]
    wv = dw_ref[...]
    bias = db_ref[...]
    for t in range(seq):
        acc = ext(t) * wv[0:1, :]
        for k in range(1, taps):
            acc = acc + ext(t + k) * wv[k:k + 1, :]
        cfull[c, t] = acc + bias
    for r in range(hist):
        st_ref[r] = ext(r + seq)

    @pl.when(c == nsteps - 1)
    def _():
        for t in range(seq):
            row = jnp.concatenate([cfull[cc, t] for cc in range(nsteps)], axis=-1)
            o_ref[t] = _ln_silu(row, lg_ref[...], lb_ref[...]).astype(o_ref.dtype)


def _conv_sample(cache, u, dw, db, lg, lb, layer, *, bc):
    seq, b, d = u.shape
    taps = dw.shape[1]
    hist = taps - 1
    nsteps = d // bc
    vec = lambda: pl.BlockSpec((None, 1, d), lambda i: (layer, 0, 0))
    return pl.pallas_call(
        functools.partial(_conv_s_body, taps=taps, seq=seq, nsteps=nsteps),
        out_shape=[jax.ShapeDtypeStruct((seq, b, d), BF16),
                   jax.ShapeDtypeStruct((hist, b, d), F32)],
        grid=(nsteps,),
        in_specs=[pl.BlockSpec((None, hist, b, bc), lambda i: (layer, 0, 0, i)),
                  pl.BlockSpec((seq, b, bc), lambda i: (0, 0, i)),
                  pl.BlockSpec((None, taps, bc), lambda i: (layer, 0, i)),
                  pl.BlockSpec((None, 1, bc), lambda i: (layer, 0, i)),
                  vec(), vec()],
        out_specs=[pl.BlockSpec((seq, b, d), lambda i: (0, 0, 0)),
                   pl.BlockSpec((hist, b, bc), lambda i: (0, 0, i))],
        scratch_shapes=[pltpu.VMEM((nsteps, seq, b, bc), F32)],
        compiler_params=_cparams(1),
        name="conv_sample",
    )(cache, u, dw, _vec3(db), _vec3(lg), _vec3(lb))


def _s5_abar(lam_re, lam_im, dt):
    mag = jnp.exp(lam_re * dt)
    ang = lam_im * dt
    return mag * jnp.cos(ang), mag * jnp.sin(ang)


def _s5_disc_body(are_ref, aim_ref, ldt_ref, arep_ref, airep_ref, bre_ref, bim_ref,
                  abr_ref, abi_ref, bbr_ref, bbi_ref):
    dt = jnp.exp(ldt_ref[...])
    abar_re, abar_im = _s5_abar(are_ref[...], aim_ref[...], dt)
    abr_ref[...] = abar_re
    abi_ref[...] = abar_im
    lam_re = arep_ref[...]
    lam_im = airep_ref[...]
    rep_re, rep_im = _s5_abar(lam_re, lam_im, dt)
    nr = rep_re - 1.0
    ni = rep_im
    den = lam_re * lam_re + lam_im * lam_im
    q_re = (nr * lam_re + ni * lam_im) / den
    q_im = (ni * lam_re - nr * lam_im) / den
    br = bre_ref[...]
    bi = bim_ref[...]
    bbr_ref[...] = q_re * br - q_im * bi
    bbi_ref[...] = q_re * bi + q_im * br


def _s5_discretise(a_re, a_im, log_dt, b_re, b_im):
    ns, g, p = a_re.shape
    rows = ns * g
    two = lambda x: x.reshape(rows, p)
    rep = lambda x: jnp.repeat(x.reshape(rows, p), GROUP_SIZE, axis=1)
    wide = lambda x: x.reshape(rows, p * GROUP_SIZE)
    outs = pl.pallas_call(
        _s5_disc_body,
        out_shape=[jax.ShapeDtypeStruct((rows, p), F32)] * 2
        + [jax.ShapeDtypeStruct((rows, p * GROUP_SIZE), F32)] * 2,
        name="s5_discretise",
    )(two(a_re), two(a_im), log_dt.reshape(rows, 1), rep(a_re), rep(a_im), wide(b_re), wide(b_im))
    abr, abi, bbr, bbi = outs
    return (abr.reshape(ns, g, p), abi.reshape(ns, g, p),
            bbr.reshape(ns, g, p, GROUP_SIZE), bbi.reshape(ns, g, p, GROUP_SIZE))


SLAB_GROUPS = 16
SLAB = SLAB_GROUPS * GROUP_SIZE


def _s5_expand_body(xbr_ref, xbi_ref, xcr_ref, xci_ref, bm_ref, cm_ref, *, p):
    gpt = LANES // p
    tiles_slab = SLAB_GROUPS // gpt
    rg = lax.broadcasted_iota(jnp.int32, (SLAB, LANES), 0) // GROUP_SIZE
    lg = lax.broadcasted_iota(jnp.int32, (SLAB, LANES), 1) // p
    for half, x_ref in enumerate((xbr_ref, xbi_ref)):
        x = x_ref[...]
        for a in range(tiles_slab):
            col = (half * tiles_slab + a) * LANES
            bm_ref[:, col:col + LANES] = jnp.where(rg == gpt * a + lg, x, 0.0).astype(BF16)
    cg = lax.broadcasted_iota(jnp.int32, (LANES, SLAB), 1) // GROUP_SIZE
    rc = lax.broadcasted_iota(jnp.int32, (LANES, SLAB), 0) // p
    for half, x_ref in enumerate((xcr_ref, xci_ref)):
        x = x_ref[...]
        for a in range(tiles_slab):
            row = (half * tiles_slab + a) * LANES
            cm_ref[row:row + LANES, :] = jnp.where(cg == gpt * a + rc, x, 0.0).astype(BF16)


def _s5_block_weights(bb_re, bb_im, c_re, c_im):
    ns, g, p, gs = bb_re.shape
    s = g // SLAB_GROUPS
    n = ns * s
    gpt = LANES // p
    bcomp = lambda x: jnp.tile(
        x.reshape(n, SLAB_GROUPS, p, gs).transpose(0, 1, 3, 2).reshape(n, SLAB, p), (1, 1, gpt))
    ccomp = lambda x: jnp.tile(
        x.reshape(n, SLAB_GROUPS, gs, p).transpose(0, 3, 1, 2).reshape(n, p, SLAB), (1, gpt, 1))
    width = 2 * SLAB_GROUPS * p
    bspec = pl.BlockSpec((None, SLAB, LANES), lambda i: (i, 0, 0))
    cspec = pl.BlockSpec((None, LANES, SLAB), lambda i: (i, 0, 0))
    bmat, cmat = pl.pallas_call(
        functools.partial(_s5_expand_body, p=p),
        out_shape=[jax.ShapeDtypeStruct((n, SLAB, width), BF16),
                   jax.ShapeDtypeStruct((n, width, SLAB), BF16)],
        grid=(n,),
        in_specs=[bspec, bspec, cspec, cspec],
        out_specs=[pl.BlockSpec((None, SLAB, width), lambda i: (i, 0, 0)),
                   pl.BlockSpec((None, width, SLAB), lambda i: (i, 0, 0))],
        compiler_params=_cparams(1),
        name="s5_expand",
    )(bcomp(bb_re), bcomp(bb_im), ccomp(c_re), ccomp(-c_im))
    return bmat.reshape(ns, s, SLAB, width), cmat.reshape(ns, s, width, SLAB)


def _s5_pitch(rows):
    return rows + SUBLANES // 2


def _s5_project_in(u, bmat_ref, z, *, rows, pitch, n_slab, tiles_half):
    tiles_slab = tiles_half // n_slab
    for s in range(n_slab):
        bu = jnp.dot(u[:, s * SLAB:(s + 1) * SLAB].astype(BF16), bmat_ref[s],
                     preferred_element_type=F32)
        for half in range(2):
            for q in range(tiles_slab):
                j = half * tiles_half + s * tiles_slab + q
                col = (half * tiles_slab + q) * LANES
                z[j * pitch:j * pitch + rows, :] = bu[:, col:col + LANES]


def _s5_project_out(u, z, cmat_ref, d_ref, store, *, rows, pitch, n_slab, tiles_half):
    tiles_slab = tiles_half // n_slab
    for s in range(n_slab):
        parts = []
        for half in range(2):
            for q in range(tiles_slab):
                j = half * tiles_half + s * tiles_slab + q
                parts.append(z[j * pitch:j * pitch + rows, :].astype(BF16))
        st = jnp.concatenate(parts, axis=-1)
        y = jnp.dot(st, cmat_ref[s], preferred_element_type=F32)
        sl = slice(s * SLAB, (s + 1) * SLAB)
        y = y + d_ref[:, sl] * u[:, sl]
        store(sl, jax.nn.gelu(y))


def _s5_step(z, t, state, abar, *, pitch, tiles_half):
    nv = tiles_half // SUBLANES
    new = []
    for k in range(nv):
        idx_re = pl.ds(k * SUBLANES * pitch + t, SUBLANES, stride=pitch)
        idx_im = pl.ds((tiles_half + k * SUBLANES) * pitch + t, SUBLANES, stride=pitch)
        ar, ai = abar[k]
        sr, si = state[k]
        nr = ar * sr - ai * si + z[idx_re, :]
        ni = ar * si + ai * sr + z[idx_im, :]
        z[idx_re, :] = nr
        z[idx_im, :] = ni
        new.append((nr, ni))
    return tuple(new)


def _s5_p_body(u_ref, bmat_ref, cmat_ref, abr_ref, abi_ref, d_ref, o_ref, sre_ref, sim_ref,
               z, st_re, st_im, *, rows, tiles_per_seq, n_slab, tiles_half):
    m = pl.program_id(0)
    pitch = _s5_pitch(rows)
    nv = tiles_half // SUBLANES
    u = u_ref[...]
    _s5_project_in(u, bmat_ref, z, rows=rows, pitch=pitch, n_slab=n_slab, tiles_half=tiles_half)

    @pl.when(m % tiles_per_seq == 0)
    def _():
        st_re[...] = jnp.zeros(st_re.shape, F32)
        st_im[...] = jnp.zeros(st_im.shape, F32)

    vsl = lambda k: slice(k * SUBLANES, (k + 1) * SUBLANES)
    abar = tuple((abr_ref[vsl(k), :], abi_ref[vsl(k), :]) for k in range(nv))
    state0 = tuple((st_re[vsl(k), :], st_im[vsl(k), :]) for k in range(nv))

    def step(t, state):
        return _s5_step(z, t, state, abar, pitch=pitch, tiles_half=tiles_half)

    state = lax.fori_loop(0, rows, step, state0, unroll=4)
    for k in range(nv):
        st_re[vsl(k), :] = state[k][0]
        st_im[vsl(k), :] = state[k][1]
    sre_ref[...] = st_re[...]
    sim_ref[...] = st_im[...]

    def store(sl, val):
        o_ref[:, sl] = val.astype(o_ref.dtype)

    _s5_project_out(u, z, cmat_ref, d_ref, store, rows=rows, pitch=pitch, n_slab=n_slab,
                    tiles_half=tiles_half)


def _s5_prompt(u, bmat, cmat, abar_re, abar_im, dskip, *, rows, seq_len):
    m, d = u.shape
    n_slab = bmat.shape[0]
    tiles_half = abar_re.shape[0]
    pitch = _s5_pitch(rows)
    tiles_per_seq = seq_len // rows
    n_seq = m // seq_len
    st_spec = pl.BlockSpec((None, tiles_half, LANES), lambda i: (i // tiles_per_seq, 0, 0))
    full = lambda a: pl.BlockSpec(a.shape, lambda i: (0,) * a.ndim)
    return pl.pallas_call(
        functools.partial(_s5_p_body, rows=rows, tiles_per_seq=tiles_per_seq, n_slab=n_slab,
                          tiles_half=tiles_half),
        out_shape=[jax.ShapeDtypeStruct((m, d), BF16),
                   jax.ShapeDtypeStruct((n_seq, tiles_half, LANES), F32),
                   jax.ShapeDtypeStruct((n_seq, tiles_half, LANES), F32)],
        grid=(m // rows,),
        in_specs=[pl.BlockSpec((rows, d), lambda i: (i, 0)),
                  full(bmat), full(cmat), full(abar_re), full(abar_im), full(dskip)],
        out_specs=[pl.BlockSpec((rows, d), lambda i: (i, 0)), st_spec, st_spec],
        scratch_shapes=[pltpu.VMEM((2 * tiles_half * pitch, LANES), F32),
                        pltpu.VMEM((tiles_half, LANES), F32),
                        pltpu.VMEM((tiles_half, LANES), F32)],
        compiler_params=_cparams(1),
        name="s5_prompt",
    )(u, bmat, cmat, abar_re, abar_im, dskip)


def _s5_s_body(u_ref, h0r_ref, h0i_ref, bmat_ref, cmat_ref, abr_ref, abi_ref, d_ref,
               o_ref, sre_ref, sim_ref, z, zh, *, nb, seq, n_slab, tiles_half):
    rows = nb * seq
    pitch = _s5_pitch(rows)
    hp = _s5_pitch(nb)
    nv = tiles_half // SUBLANES
    d = u_ref.shape[2]
    u = u_ref[...].reshape(rows, d)
    _s5_project_in(u, bmat_ref, z, rows=rows, pitch=pitch, n_slab=n_slab, tiles_half=tiles_half)
    for j in range(tiles_half):
        cs = slice(j * LANES, (j + 1) * LANES)
        zh[j * hp:j * hp + nb, :] = h0r_ref[:, cs]
        zh[(tiles_half + j) * hp:(tiles_half + j) * hp + nb, :] = h0i_ref[:, cs]

    vsl = lambda k: slice(k * SUBLANES, (k + 1) * SUBLANES)
    abar = tuple((abr_ref[vsl(k), :], abi_ref[vsl(k), :]) for k in range(nv))

    def one_seq(b, carry):
        state = []
        for k in range(nv):
            idx_re = pl.ds(k * SUBLANES * hp + b, SUBLANES, stride=hp)
            idx_im = pl.ds((tiles_half + k * SUBLANES) * hp + b, SUBLANES, stride=hp)
            state.append((zh[idx_re, :], zh[idx_im, :]))
        state = tuple(state)
        for t in range(seq):
            state = _s5_step(z, t * nb + b, state, abar, pitch=pitch, tiles_half=tiles_half)
        for k in range(nv):
            idx_re = pl.ds(k * SUBLANES * hp + b, SUBLANES, stride=hp)
            idx_im = pl.ds((tiles_half + k * SUBLANES) * hp + b, SUBLANES, stride=hp)
            zh[idx_re, :] = state[k][0]
            zh[idx_im, :] = state[k][1]
        return carry

    lax.fori_loop(0, nb, one_seq, 0)
    for j in range(tiles_half):
        cs = slice(j * LANES, (j + 1) * LANES)
        sre_ref[:, cs] = zh[j * hp:j * hp + nb, :]
        sim_ref[:, cs] = zh[(tiles_half + j) * hp:(tiles_half + j) * hp + nb, :]

    def store(sl, val):
        o_ref[:, :, sl] = val.reshape(seq, nb, val.shape[1]).astype(o_ref.dtype)

    _s5_project_out(u, z, cmat_ref, d_ref, store, rows=rows, pitch=pitch, n_slab=n_slab,
                    tiles_half=tiles_half)


def _s5_sample(u, h0_re, h0_im, bmat, cmat, abar_re, abar_im, dskip, *, nb):
    seq, b, d = u.shape
    nstate = h0_re.shape[1]
    n_slab = bmat.shape[0]
    tiles_half = abar_re.shape[0]
    pitch = _s5_pitch(nb * seq)
    hp = _s5_pitch(nb)
    full = lambda a: pl.BlockSpec(a.shape, lambda i: (0,) * a.ndim)
    st_spec = pl.BlockSpec((nb, nstate), lambda i: (i, 0))
    u_spec = pl.BlockSpec((seq, nb, d), lambda i: (0, i, 0))
    return pl.pallas_call(
        functools.partial(_s5_s_body, nb=nb, seq=seq, n_slab=n_slab, tiles_half=tiles_half),
        out_shape=[jax.ShapeDtypeStruct((seq, b, d), BF16),
                   jax.ShapeDtypeStruct(h0_re.shape, F32),
                   jax.ShapeDtypeStruct(h0_im.shape, F32)],
        grid=(b // nb,),
        in_specs=[u_spec, st_spec, st_spec,
                  full(bmat), full(cmat), full(abar_re), full(abar_im), full(dskip)],
        out_specs=[u_spec, st_spec, st_spec],
        scratch_shapes=[pltpu.VMEM((2 * tiles_half * pitch, LANES), F32),
                        pltpu.VMEM((2 * tiles_half * hp, LANES), F32)],
        compiler_params=_cparams(1),
        name="s5_sample",
    )(u, h0_re, h0_im, bmat, cmat, abar_re, abar_im, dskip)


def _hn_dtype(i):
    return F32 if (i % N_MIXERS == 1) else BF16


def _run_sample(x, states, p):
    seq, nb, d = x.shape
    m = seq * nb
    depth = p["norm_mix"].shape[0]
    conv_st, ssm_re_st, ssm_im_st, ffn_st = states
    new_conv, new_re, new_im, new_ffn, wbf = [], [], [], [], []
    x = x.reshape(m, d)
    hn = _rmsnorm(x, p["norm_mix"][0], _hn_dtype(0), m)
    y = None
    for i in range(depth):
        j = i // N_MIXERS
        w = {}
        if i % N_MIXERS == 0:
            u, w["in_a"], w["in_b"] = _mm(hn, p["conv_w_in"], glu=True, bm=m, bn=512, layer=j)
            c, st = _conv_sample(conv_st, u.reshape(seq, nb, d), p["conv_dw"], p["conv_dw_b"],
                                 p["conv_ln_g"], p["conv_ln_b"], j, bc=256)
            new_conv.append(st)
            x, hn, w["out"] = _mm(c.reshape(m, d), p["conv_w_out"], glu=False, bm=m, bn=512,
                                  layer=j, resid=x, gamma=p["norm_ffn"][i], hn_dtype=BF16)
        else:
            v, s_re, s_im = _s5_sample(hn.reshape(seq, nb, d), ssm_re_st[j].reshape(nb, -1),
                                       ssm_im_st[j].reshape(nb, -1), p["s5_bmat"][j],
                                       p["s5_cmat"][j], p["s5_abar_re"][j], p["s5_abar_im"][j],
                                       p["ssm_D"][j].reshape(1, d), nb=32)
            new_re.append(s_re.reshape(ssm_re_st.shape[1:]))
            new_im.append(s_im.reshape(ssm_im_st.shape[1:]))
            x, hn, w["glu_a"], w["glu_b"] = _mm(
                v.reshape(m, d), p["ssm_w_glu"], glu=True, bm=m, bn=512, layer=j, resid=x,
                gamma=p["norm_ffn"][i], hn_dtype=BF16)
        act, st, w["gate"], w["up"] = _ffn1_sample(hn, p["ffn_w_gate"], p["ffn_w_up"],
                                                   p["ffn_conv"], ffn_st, i, bn=512, seq=seq)
        new_ffn.append(st)
        last = i == depth - 1
        gamma = p["norm_final"] if last else p["norm_mix"][i + 1]
        outs = _mm(act, p["ffn_w_down"], glu=False, bm=m, bn=256, layer=i, resid=x, gamma=gamma,
                   hn_dtype=F32 if last else _hn_dtype(i + 1), write_x=not last)
        if last:
            y, w["down"] = outs
        else:
            x, hn, w["down"] = outs
        wbf.append(w)
    return (y.reshape(seq, nb, d), jnp.stack(new_conv), jnp.stack(new_re), jnp.stack(new_im),
            jnp.stack(new_ffn)), wbf


def _run_prompt(x, p, wbf, *, seq_len, bm):
    m, d = x.shape
    n_seq = m // seq_len
    depth = p["norm_mix"].shape[0]
    new_conv, new_re, new_im, new_ffn = [], [], [], []
    hn = _rmsnorm(x, p["norm_mix"][0], _hn_dtype(0), bm)
    y = None
    for i in range(depth):
        j = i // N_MIXERS
        w = wbf[i]
        if i % N_MIXERS == 0:
            (u,) = _mm(hn, (w["in_a"], w["in_b"]), glu=True, bm=bm, bn=1024)
            cs = p["conv_dw"].shape[1] - 1
            new_conv.append(u.reshape(n_seq, seq_len, d)[:, -cs:])
            c = _conv_prompt(u, p["conv_dw"], p["conv_dw_b"], p["conv_ln_g"], p["conv_ln_b"], j,
                             bm=256, seq_len=seq_len)
            x, hn = _mm_resident(c, (w["out"],), glu=False, bm=512, bn=512, resid=x,
                        gamma=p["norm_ffn"][i], hn_dtype=BF16)
        else:
            v, s_re, s_im = _s5_prompt(hn, p["s5_bmat"][j], p["s5_cmat"][j], p["s5_abar_re"][j],
                                       p["s5_abar_im"][j], p["ssm_D"][j].reshape(1, d), rows=256,
                                       seq_len=seq_len)
            g, pdim = p["ssm_A_re"].shape[1:]
            new_re.append(s_re.reshape(n_seq, g, pdim))
            new_im.append(s_im.reshape(n_seq, g, pdim))
            x, hn = _mm_resident(v, (w["glu_a"], w["glu_b"]), glu=True, bm=512, bn=512, resid=x,
                        gamma=p["norm_ffn"][i], hn_dtype=BF16)
        act, tail = _ffn1_prompt(hn, w["gate"], w["up"], p["ffn_conv"], i, bm=seq_len, bn=512,
                                 seq_len=seq_len)
        fs = p["ffn_conv"].shape[1] - 1
        new_ffn.append(tail[:, -fs:])
        last = i == depth - 1
        gamma = p["norm_final"] if last else p["norm_mix"][i + 1]
        outs = _mm_resident(act, (w["down"],), glu=False, bm=256, bn=512, resid=x, gamma=gamma,
                   hn_dtype=F32 if last else _hn_dtype(i + 1), write_x=not last)
        if last:
            (y,) = outs
        else:
            x, hn = outs
    return (y, jnp.stack(new_conv), jnp.stack(new_re), jnp.stack(new_im), jnp.stack(new_ffn))


def kernel(x_prompt, x_sample, state_conv, state_ssm_re, state_ssm_im, state_ffn, norm_mix, norm_ffn, norm_final, conv_w_in, conv_dw, conv_dw_b, conv_ln_g, conv_ln_b, conv_w_out, ssm_A_re, ssm_A_im, ssm_log_dt, ssm_B_re, ssm_B_im, ssm_C_re, ssm_C_im, ssm_D, ssm_w_glu, ffn_w_gate, ffn_w_up, ffn_conv, ffn_w_down):
    bp, sp, d = x_prompt.shape
    abr, abi, bbr, bbi = _s5_discretise(ssm_A_re, ssm_A_im, ssm_log_dt, ssm_B_re, ssm_B_im)
    ns, g, pdim = ssm_A_re.shape
    bmat, cmat = _s5_block_weights(bbr, bbi, ssm_C_re, ssm_C_im)
    tiles_half = g * pdim // LANES
    p = dict(
        norm_mix=norm_mix, norm_ffn=norm_ffn, norm_final=norm_final,
        conv_w_in=conv_w_in, conv_dw=conv_dw, conv_dw_b=conv_dw_b,
        conv_ln_g=conv_ln_g, conv_ln_b=conv_ln_b, conv_w_out=conv_w_out,
        ssm_A_re=ssm_A_re, ssm_D=ssm_D, ssm_w_glu=ssm_w_glu,
        s5_bmat=bmat, s5_cmat=cmat,
        s5_abar_re=abr.reshape(ns, tiles_half, LANES), s5_abar_im=abi.reshape(ns, tiles_half, LANES),
        ffn_w_gate=ffn_w_gate, ffn_w_up=ffn_w_up, ffn_conv=ffn_conv, ffn_w_down=ffn_w_down,
    )
    tm = lambda a: jnp.swapaxes(a, -3, -2)
    (y_s, conv_s, re_s, im_s, ffn_s), wbf = _run_sample(
        tm(x_sample), (tm(state_conv), state_ssm_re, state_ssm_im, tm(state_ffn)), p)
    y_p, conv_p, re_p, im_p, ffn_p = _run_prompt(
        x_prompt.reshape(bp * sp, d), p, wbf, seq_len=sp, bm=1024)
    return (y_p.reshape(bp, sp, d), tm(y_s), conv_p, tm(conv_s), re_p, im_p, re_s, im_s, ffn_p,
            tm(ffn_s))
```

```python
import functools

import jax
import jax.numpy as jnp
from jax import lax
from jax.experimental import pallas as pl
from jax.experimental.pallas import tpu as pltpu

F32 = jnp.float32
BF16 = jnp.bfloat16

EPS = 1e-6
N_MIXERS = 2
GROUP_SIZE = 16
LANES = 128
SUBLANES = 8
VMEM_LIMIT = 56 * 1024 * 1024
ROW_CHUNK = 256


def _cparams(n_axes):
    return pltpu.CompilerParams(
        dimension_semantics=("arbitrary",) * n_axes, vmem_limit_bytes=VMEM_LIMIT)


def _rms_scale(ss, n):
    return lax.rsqrt(ss / n + EPS)


def _rmsnorm_body(x_ref, g_ref, o_ref):
    x = x_ref[...]
    ss = jnp.sum(x * x, axis=-1, keepdims=True)
    o_ref[...] = ((x * _rms_scale(ss, x.shape[-1])) * g_ref[...]).astype(o_ref.dtype)


def _rmsnorm(x, g, out_dtype, bm):
    m, d = x.shape
    return pl.pallas_call(
        _rmsnorm_body,
        out_shape=jax.ShapeDtypeStruct((m, d), out_dtype),
        grid=(m // bm,),
        in_specs=[pl.BlockSpec((bm, d), lambda i: (i, 0)),
                  pl.BlockSpec((1, d), lambda i: (0, 0))],
        out_specs=pl.BlockSpec((bm, d), lambda i: (i, 0)),
        compiler_params=_cparams(1),
        name="rmsnorm",
    )(x, g.reshape(1, d))


def _mm_body(*refs, glu, has_res, write_x, emit, prenorm, nn, bn, n_out):
    it = iter(refs)
    lhs_ref = next(it)
    pg_ref = next(it) if prenorm else None
    wa_ref = next(it)
    wb_ref = next(it) if glu else None
    res_ref = next(it) if has_res else None
    g_ref = next(it) if has_res else None
    xo_ref = next(it) if write_x else None
    hn_ref = next(it) if has_res else None
    wao_ref = next(it) if emit else None
    wbo_ref = next(it) if (emit and glu) else None
    rowbuf = next(it) if has_res else None
    ss_ref = next(it) if has_res else None
    hn_scr = next(it) if prenorm else None

    n = pl.program_id(1)
    bm = lhs_ref.shape[0]
    if prenorm:
        @pl.when(n == 0)
        def _():
            x = lhs_ref[...]
            ss = jnp.sum(x * x, axis=-1, keepdims=True)
            hn_scr[...] = ((x * _rms_scale(ss, x.shape[-1])) * pg_ref[...]).astype(BF16)
        lhs_ref = hn_scr
    if emit:
        wao_ref[...] = wa_ref[...].astype(BF16)
        wa_ref = wao_ref
        if glu:
            wbo_ref[...] = wb_ref[...].astype(BF16)
            wb_ref = wbo_ref
    if has_res:
        @pl.when(n == 0)
        def _():
            ss_ref[...] = jnp.zeros(ss_ref.shape, F32)
    for r0 in range(0, bm, ROW_CHUNK):
        rs = slice(r0, min(r0 + ROW_CHUNK, bm))
        x = lhs_ref[rs, :]
        val = jnp.dot(x, wa_ref[...], preferred_element_type=F32)
        if glu:
            gate = jnp.dot(x, wb_ref[...], preferred_element_type=F32)
            val = val * jax.nn.sigmoid(gate)
        if has_res:
            val = res_ref[rs, :] + val
        if write_x:
            xo_ref[rs, :] = val
        if has_res:
            rowbuf[n, rs, :] = val
            ss_ref[rs, :] = ss_ref[rs, :] + jnp.sum(val * val, axis=-1, keepdims=True)

    if has_res:
        @pl.when(n == nn - 1)
        def _():
            scale = _rms_scale(ss_ref[...], n_out)
            for j in range(nn):
                sl = slice(j * bn, (j + 1) * bn)
                hn_ref[:, sl] = ((rowbuf[j] * scale) * g_ref[:, sl]).astype(hn_ref.dtype)


def _mm(lhs, w, *, glu, bm, bn, layer=None, resid=None, gamma=None, hn_dtype=None, write_x=True,
        prenorm=None):
    m, k = lhs.shape
    emit = layer is not None
    if emit:
        assert m == bm
        n_out = w.shape[2] // (2 if glu else 1)
    else:
        n_out = w[0].shape[1]
    nn = n_out // bn
    has_res = resid is not None
    in_specs = [pl.BlockSpec((bm, k), lambda i, j: (i, 0))]
    args = [lhs]
    if prenorm is not None:
        in_specs.append(pl.BlockSpec((1, k), lambda i, j: (0, 0)))
        args.append(prenorm.reshape(1, k))
    if emit:
        in_specs.append(pl.BlockSpec((None, k, bn), lambda i, j: (layer, 0, j)))
        args.append(w)
        if glu:
            in_specs.append(pl.BlockSpec((None, k, bn), lambda i, j: (layer, 0, j + nn)))
            args.append(w)
    else:
        for wi in w:
            in_specs.append(pl.BlockSpec((k, bn), lambda i, j: (0, j)))
            args.append(wi)
    out_shape, out_specs, scratch = [], [], []
    if has_res:
        in_specs += [pl.BlockSpec((bm, bn), lambda i, j: (i, j)),
                     pl.BlockSpec((1, n_out), lambda i, j: (0, 0))]
        args += [resid, gamma.reshape(1, n_out)]
    if write_x:
        out_shape.append(jax.ShapeDtypeStruct((m, n_out), F32))
        out_specs.append(pl.BlockSpec((bm, bn), lambda i, j: (i, j)))
    if has_res:
        out_shape.append(jax.ShapeDtypeStruct((m, n_out), hn_dtype))
        out_specs.append(pl.BlockSpec((bm, n_out), lambda i, j: (i, 0)))
        scratch += [pltpu.VMEM((nn, bm, bn), F32), pltpu.VMEM((bm, 1), F32)]
    if emit:
        for _ in range(2 if glu else 1):
            out_shape.append(jax.ShapeDtypeStruct((k, n_out), BF16))
            out_specs.append(pl.BlockSpec((k, bn), lambda i, j: (0, j)))
    if prenorm is not None:
        scratch.append(pltpu.VMEM((bm, k), BF16))
    return pl.pallas_call(
        functools.partial(_mm_body, glu=glu, has_res=has_res, write_x=write_x, emit=emit,
                          prenorm=prenorm is not None, nn=nn, bn=bn, n_out=n_out),
        out_shape=out_shape,
        grid=(m // bm, nn),
        in_specs=in_specs,
        out_specs=out_specs,
        scratch_shapes=scratch,
        compiler_params=_cparams(2),
        name="mm_glu" if glu else "mm_lin",
    )(*args)


def _mmr_body(*refs, glu, write_x, bn, n_out):
    it = iter(refs)
    lhs_ref = next(it)
    wa_ref = next(it)
    wb_ref = next(it) if glu else None
    res_ref = next(it)
    g_ref = next(it)
    xo_ref = next(it) if write_x else None
    hn_ref = next(it)
    bm = lhs_ref.shape[0]
    for r0 in range(0, bm, ROW_CHUNK):
        rs = slice(r0, r0 + ROW_CHUNK)
        x = lhs_ref[rs, :]
        ss = jnp.zeros((ROW_CHUNK, 1), F32)
        vals = []
        for j in range(n_out // bn):
            sl = slice(j * bn, (j + 1) * bn)
            val = jnp.dot(x, wa_ref[:, sl], preferred_element_type=F32)
            if glu:
                gate = jnp.dot(x, wb_ref[:, sl], preferred_element_type=F32)
                val = val * jax.nn.sigmoid(gate)
            val = res_ref[rs, sl] + val
            if write_x:
                xo_ref[rs, sl] = val
            ss = ss + jnp.sum(val * val, axis=-1, keepdims=True)
            vals.append(val)
        scale = _rms_scale(ss, n_out)
        for j, val in enumerate(vals):
            sl = slice(j * bn, (j + 1) * bn)
            hn_ref[rs, sl] = ((val * scale) * g_ref[:, sl]).astype(hn_ref.dtype)


def _mm_resident(lhs, w, *, glu, bm, bn, resid, gamma, hn_dtype, write_x=True):
    m, k = lhs.shape
    n_out = w[0].shape[1]
    row = lambda width: pl.BlockSpec((bm, width), lambda i: (i, 0))
    in_specs = [row(k)] + [pl.BlockSpec((k, n_out), lambda i: (0, 0)) for _ in w]
    in_specs += [row(n_out), pl.BlockSpec((1, n_out), lambda i: (0, 0))]
    out_shape, out_specs = [], []
    if write_x:
        out_shape.append(jax.ShapeDtypeStruct((m, n_out), F32))
        out_specs.append(row(n_out))
    out_shape.append(jax.ShapeDtypeStruct((m, n_out), hn_dtype))
    out_specs.append(row(n_out))
    return pl.pallas_call(
        functools.partial(_mmr_body, glu=glu, write_x=write_x, bn=bn, n_out=n_out),
        out_shape=out_shape,
        grid=(m // bm,),
        in_specs=in_specs,
        out_specs=out_specs,
        compiler_params=_cparams(1),
        name="mmr_glu" if glu else "mmr_lin",
    )(lhs, *w, resid, gamma.reshape(1, n_out))


def _shift_rows(g, tail, k):
    r = pltpu.roll(g, k, 0)
    t = pltpu.roll(tail, k, 0)
    i8 = lax.broadcasted_iota(jnp.int32, tail.shape, 0)
    head = jnp.where(i8 < k, t, r[0:SUBLANES, :])
    return jnp.concatenate([head, r[SUBLANES:, :]], axis=0)


def _ffn1_prompt_body(h_ref, wg_ref, wu_ref, cw_ref, act_ref, tail_ref, carry, *, bm, tiles_per_seq):
    m = pl.program_id(0)
    n = pl.program_id(1)
    cw = cw_ref[...]
    taps = cw.shape[0]

    @pl.when(m % tiles_per_seq == 0)
    def _():
        carry[n] = jnp.zeros(carry.shape[1:], F32)

    tail = carry[n]
    for r0 in range(0, bm, ROW_CHUNK):
        rs = slice(r0, r0 + ROW_CHUNK)
        h = h_ref[rs, :]
        g = jnp.dot(h, wg_ref[...], preferred_element_type=F32)
        up = jnp.dot(h, wu_ref[...], preferred_element_type=F32)
        gc = g * cw[taps - 1:taps, :]
        for k in range(taps - 1):
            gc = gc + _shift_rows(g, tail, taps - 1 - k) * cw[k:k + 1, :]
        act_ref[rs, :] = (jax.nn.silu(gc) * up).astype(act_ref.dtype)
        tail = g[ROW_CHUNK - SUBLANES:, :]
    carry[n] = tail
    tail_ref[...] = tail


def _ffn1_prompt(h, wg, wu, cw, layer, *, bm, bn, seq_len):
    m, k = h.shape
    f = wg.shape[1]
    nn = f // bn
    tiles_per_seq = seq_len // bm
    act, tails = pl.pallas_call(
        functools.partial(_ffn1_prompt_body, bm=bm, tiles_per_seq=tiles_per_seq),
        out_shape=[jax.ShapeDtypeStruct((m, f), BF16),
                   jax.ShapeDtypeStruct((m // bm, SUBLANES, f), F32)],
        grid=(m // bm, nn),
        in_specs=[pl.BlockSpec((bm, k), lambda i, j: (i, 0)),
                  pl.BlockSpec((k, bn), lambda i, j: (0, j)),
                  pl.BlockSpec((k, bn), lambda i, j: (0, j)),
                  pl.BlockSpec((None, cw.shape[1], bn), lambda i, j: (layer, 0, j))],
        out_specs=[pl.BlockSpec((bm, bn), lambda i, j: (i, j)),
                   pl.BlockSpec((None, SUBLANES, bn), lambda i, j: (i, 0, j))],
        scratch_shapes=[pltpu.VMEM((nn, SUBLANES, bn), F32)],
        compiler_params=_cparams(2),
        name="ffn_gate_up",
    )(h, wg, wu, cw)
    return act, tails[tiles_per_seq - 1::tiles_per_seq]


def _ffn1_sample_body(h_ref, wg_ref, wu_ref, cw_ref, c_ref, act_ref, st_ref, wgo_ref, wuo_ref, *,
                      nb, seq):
    wgo_ref[...] = wg_ref[...].astype(BF16)
    wuo_ref[...] = wu_ref[...].astype(BF16)
    h = h_ref[...]
    g = jnp.dot(h, wgo_ref[...], preferred_element_type=F32)
    up = jnp.dot(h, wuo_ref[...], preferred_element_type=F32)
    cw = cw_ref[...]
    taps = cw.shape[0]
    hist = [c_ref[r] for r in range(taps - 1)] + [g[t * nb:(t + 1) * nb, :] for t in range(seq)]
    for t in range(seq):
        gc = hist[t] * cw[0:1, :]
        for k in range(1, taps):
            gc = gc + hist[t + k] * cw[k:k + 1, :]
        act_ref[t * nb:(t + 1) * nb, :] = (
            jax.nn.silu(gc) * up[t * nb:(t + 1) * nb, :]).astype(act_ref.dtype)
    for r in range(taps - 1):
        st_ref[r] = hist[seq + r]


def _ffn1_sample(h, wg, wu, cw, cache, layer, *, bn, seq):
    m, k = h.shape
    f = wg.shape[2]
    nb = m // seq
    hist = cw.shape[1] - 1
    wspec = pl.BlockSpec((None, k, bn), lambda j: (layer, 0, j))
    wout = pl.BlockSpec((k, bn), lambda j: (0, j))
    return pl.pallas_call(
        functools.partial(_ffn1_sample_body, nb=nb, seq=seq),
        out_shape=[jax.ShapeDtypeStruct((m, f), BF16),
                   jax.ShapeDtypeStruct((hist, nb, f), F32),
                   jax.ShapeDtypeStruct((k, f), BF16),
                   jax.ShapeDtypeStruct((k, f), BF16)],
        grid=(f // bn,),
        in_specs=[pl.BlockSpec((m, k), lambda j: (0, 0)), wspec, wspec,
                  pl.BlockSpec((None, cw.shape[1], bn), lambda j: (layer, 0, j)),
                  pl.BlockSpec((None, hist, nb, bn), lambda j: (layer, 0, 0, j))],
        out_specs=[pl.BlockSpec((m, bn), lambda j: (0, j)),
                   pl.BlockSpec((hist, nb, bn), lambda j: (0, 0, j)), wout, wout],
        compiler_params=_cparams(1),
        name="ffn_gate_up_s",
    )(h, wg, wu, cw, cache)


CONV_ROWS = 32
HALO = 32


def _ln_silu(c, g, b):
    mu = jnp.mean(c, axis=-1, keepdims=True)
    d = c - mu
    var = jnp.mean(d * d, axis=-1, keepdims=True)
    y = d * lax.rsqrt(var + EPS)
    return jax.nn.silu(y * g + b)


def _conv_p_body(u_ref, dw_ref, db_ref, lg_ref, lb_ref, o_ref, ubuf, cbuf, *, bm, taps,
                 tiles_per_seq):
    m = pl.program_id(0)
    d = u_ref.shape[1]
    nc = d // LANES

    @pl.when(m % tiles_per_seq == 0)
    def _():
        ubuf[:, 0:HALO, :] = jnp.zeros((nc, HALO, LANES), F32)

    @pl.when(m % tiles_per_seq != 0)
    def _():
        ubuf[:, 0:HALO, :] = ubuf[:, bm:bm + HALO, :]

    for c in range(nc):
        ubuf[c, HALO:HALO + bm, :] = u_ref[:, c * LANES:(c + 1) * LANES]
    off = HALO - (taps - 1)

    def lane_chunk(c, carry):
        wv = dw_ref[c]
        bias = db_ref[c]
        for r0 in range(0, bm, CONV_ROWS):
            acc = jnp.zeros((CONV_ROWS, LANES), F32)
            for k in range(taps):
                acc = acc + ubuf[c, r0 + off + k:r0 + off + k + CONV_ROWS, :] * wv[k:k + 1, :]
            cbuf[c, r0:r0 + CONV_ROWS, :] = acc + bias
        return carry

    lax.fori_loop(0, nc, lane_chunk, 0)
    cfull = jnp.concatenate([cbuf[c] for c in range(nc)], axis=-1)
    o_ref[...] = _ln_silu(cfull, lg_ref[...], lb_ref[...]).astype(o_ref.dtype)


def _lane_chunked(x):
    l, r, d = x.shape
    return x.reshape(l, r, d // LANES, LANES).transpose(0, 2, 1, 3)


def _vec3(x):
    return x.reshape(x.shape[0], 1, x.shape[1])


def _conv_prompt(u, dw, db, lg, lb, layer, *, bm, seq_len):
    m, d = u.shape
    taps = dw.shape[1]
    nc = d // LANES
    vec = lambda: pl.BlockSpec((None, 1, d), lambda i: (layer, 0, 0))
    return pl.pallas_call(
        functools.partial(_conv_p_body, bm=bm, taps=taps, tiles_per_seq=seq_len // bm),
        out_shape=jax.ShapeDtypeStruct((m, d), BF16),
        grid=(m // bm,),
        in_specs=[pl.BlockSpec((bm, d), lambda i: (i, 0)),
                  pl.BlockSpec((None, nc, taps, LANES), lambda i: (layer, 0, 0, 0)),
                  pl.BlockSpec((None, nc, 1, LANES), lambda i: (layer, 0, 0, 0)),
                  vec(), vec()],
        out_specs=pl.BlockSpec((bm, d), lambda i: (i, 0)),
        scratch_shapes=[pltpu.VMEM((nc, HALO + bm, LANES), F32),
                        pltpu.VMEM((nc, bm, LANES), F32)],
        compiler_params=_cparams(1),
        name="conv_prompt",
    )(u, _lane_chunked(dw), _lane_chunked(_vec3(db)), _vec3(lg), _vec3(lb))


def _conv_s_body(cache_ref, u_ref, dw_ref, db_ref, lg_ref, lb_ref, o_ref, st_ref, cfull, *,
                 taps, seq, nsteps):
    c = pl.program_id(0)
    hist = taps - 1
    ext = lambda r: cache_ref[r] if r < hist else u_ref[r - hist]
    wv = dw_ref[...]
    bias = db_ref[...]
    for t in range(seq):
        acc = ext(t) * wv[0:1, :]
        for k in range(1, taps):
            acc = acc + ext(t + k) * wv[k:k + 1, :]
        cfull[c, t] = acc + bias
    for r in range(hist):
        st_ref[r] = ext(r + seq)

    @pl.when(c == nsteps - 1)
    def _():
        for t in range(seq):
            row = jnp.concatenate([cfull[cc, t] for cc in range(nsteps)], axis=-1)
            o_ref[t] = _ln_silu(row, lg_ref[...], lb_ref[...]).astype(o_ref.dtype)


def _conv_sample(cache, u, dw, db, lg, lb, layer, *, bc):
    seq, b, d = u.shape
    taps = dw.shape[1]
    hist = taps - 1
    nsteps = d // bc
    vec = lambda: pl.BlockSpec((None, 1, d), lambda i: (layer, 0, 0))
    return pl.pallas_call(
        functools.partial(_conv_s_body, taps=taps, seq=seq, nsteps=nsteps),
        out_shape=[jax.ShapeDtypeStruct((seq, b, d), BF16),
                   jax.ShapeDtypeStruct((hist, b, d), F32)],
        grid=(nsteps,),
        in_specs=[pl.BlockSpec((None, hist, b, bc), lambda i: (layer, 0, 0, i)),
                  pl.BlockSpec((seq, b, bc), lambda i: (0, 0, i)),
                  pl.BlockSpec((None, taps, bc), lambda i: (layer, 0, i)),
                  pl.BlockSpec((None, 1, bc), lambda i: (layer, 0, i)),
                  vec(), vec()],
        out_specs=[pl.BlockSpec((seq, b, d), lambda i: (0, 0, 0)),
                   pl.BlockSpec((hist, b, bc), lambda i: (0, 0, i))],
        scratch_shapes=[pltpu.VMEM((nsteps, seq, b, bc), F32)],
        compiler_params=_cparams(1),
        name="conv_sample",
    )(cache, u, dw, _vec3(db), _vec3(lg), _vec3(lb))


def _s5_abar(lam_re, lam_im, dt):
    mag = jnp.exp(lam_re * dt)
    ang = lam_im * dt
    return mag * jnp.cos(ang), mag * jnp.sin(ang)


def _s5_disc_body(are_ref, aim_ref, ldt_ref, arep_ref, airep_ref, bre_ref, bim_ref,
                  abr_ref, abi_ref, bbr_ref, bbi_ref):
    dt = jnp.exp(ldt_ref[...])
    abar_re, abar_im = _s5_abar(are_ref[...], aim_ref[...], dt)
    abr_ref[...] = abar_re
    abi_ref[...] = abar_im
    lam_re = arep_ref[...]
    lam_im = airep_ref[...]
    rep_re, rep_im = _s5_abar(lam_re, lam_im, dt)
    nr = rep_re - 1.0
    ni = rep_im
    den = lam_re * lam_re + lam_im * lam_im
    q_re = (nr * lam_re + ni * lam_im) / den
    q_im = (ni * lam_re - nr * lam_im) / den
    br = bre_ref[...]
    bi = bim_ref[...]
    bbr_ref[...] = q_re * br - q_im * bi
    bbi_ref[...] = q_re * bi + q_im * br


def _s5_discretise(a_re, a_im, log_dt, b_re, b_im):
    ns, g, p = a_re.shape
    rows = ns * g
    two = lambda x: x.reshape(rows, p)
    rep = lambda x: jnp.repeat(x.reshape(rows, p), GROUP_SIZE, axis=1)
    wide = lambda x: x.reshape(rows, p * GROUP_SIZE)
    outs = pl.pallas_call(
        _s5_disc_body,
        out_shape=[jax.ShapeDtypeStruct((rows, p), F32)] * 2
        + [jax.ShapeDtypeStruct((rows, p * GROUP_SIZE), F32)] * 2,
        name="s5_discretise",
    )(two(a_re), two(a_im), log_dt.reshape(rows, 1), rep(a_re), rep(a_im), wide(b_re), wide(b_im))
    abr, abi, bbr, bbi = outs
    return (abr.reshape(ns, g, p), abi.reshape(ns, g, p),
            bbr.reshape(ns, g, p, GROUP_SIZE), bbi.reshape(ns, g, p, GROUP_SIZE))


SLAB_GROUPS = 16
SLAB = SLAB_GROUPS * GROUP_SIZE


def _s5_expand_body(xbr_ref, xbi_ref, xcr_ref, xci_ref, bm_ref, cm_ref, *, p):
    gpt = LANES // p
    tiles_slab = SLAB_GROUPS // gpt
    rg = lax.broadcasted_iota(jnp.int32, (SLAB, LANES), 0) // GROUP_SIZE
    lg = lax.broadcasted_iota(jnp.int32, (SLAB, LANES), 1) // p
    for half, x_ref in enumerate((xbr_ref, xbi_ref)):
        x = x_ref[...]
        for a in range(tiles_slab):
            col = (half * tiles_slab + a) * LANES
            bm_ref[:, col:col + LANES] = jnp.where(rg == gpt * a + lg, x, 0.0).astype(BF16)
    cg = lax.broadcasted_iota(jnp.int32, (LANES, SLAB), 1) // GROUP_SIZE
    rc = lax.broadcasted_iota(jnp.int32, (LANES, SLAB), 0) // p
    for half, x_ref in enumerate((xcr_ref, xci_ref)):
        x = x_ref[...]
        for a in range(tiles_slab):
            row = (half * tiles_slab + a) * LANES
            cm_ref[row:row + LANES, :] = jnp.where(cg == gpt * a + rc, x, 0.0).astype(BF16)


def _s5_block_weights(bb_re, bb_im, c_re, c_im):
    ns, g, p, gs = bb_re.shape
    s = g // SLAB_GROUPS
    n = ns * s
    gpt = LANES // p
    bcomp = lambda x: jnp.tile(
        x.reshape(n, SLAB_GROUPS, p, gs).transpose(0, 1, 3, 2).reshape(n, SLAB, p), (1, 1, gpt))
    ccomp = lambda x: jnp.tile(
        x.reshape(n, SLAB_GROUPS, gs, p).transpose(0, 3, 1, 2).reshape(n, p, SLAB), (1, gpt, 1))
    width = 2 * SLAB_GROUPS * p
    bspec = pl.BlockSpec((None, SLAB, LANES), lambda i: (i, 0, 0))
    cspec = pl.BlockSpec((None, LANES, SLAB), lambda i: (i, 0, 0))
    bmat, cmat = pl.pallas_call(
        functools.partial(_s5_expand_body, p=p),
        out_shape=[jax.ShapeDtypeStruct((n, SLAB, width), BF16),
                   jax.ShapeDtypeStruct((n, width, SLAB), BF16)],
        grid=(n,),
        in_specs=[bspec, bspec, cspec, cspec],
        out_specs=[pl.BlockSpec((None, SLAB, width), lambda i: (i, 0, 0)),
                   pl.BlockSpec((None, width, SLAB), lambda i: (i, 0, 0))],
        compiler_params=_cparams(1),
        name="s5_expand",
    )(bcomp(bb_re), bcomp(bb_im), ccomp(c_re), ccomp(-c_im))
    return bmat.reshape(ns, s, SLAB, width), cmat.reshape(ns, s, width, SLAB)


def _s5_pitch(rows):
    return rows + SUBLANES // 2


def _s5_project_in(u, bmat_ref, z, *, rows, pitch, n_slab, tiles_half):
    tiles_slab = tiles_half // n_slab
    for s in range(n_slab):
        bu = jnp.dot(u[:, s * SLAB:(s + 1) * SLAB].astype(BF16), bmat_ref[s],
                     preferred_element_type=F32)
        for half in range(2):
            for q in range(tiles_slab):
                j = half * tiles_half + s * tiles_slab + q
                col = (half * tiles_slab + q) * LANES
                z[j * pitch:j * pitch + rows, :] = bu[:, col:col + LANES]


def _s5_project_out(u, z, cmat_ref, d_ref, store, *, rows, pitch, n_slab, tiles_half):
    tiles_slab = tiles_half // n_slab
    for s in range(n_slab):
        parts = []
        for half in range(2):
            for q in range(tiles_slab):
                j = half * tiles_half + s * tiles_slab + q
                parts.append(z[j * pitch:j * pitch + rows, :].astype(BF16))
        st = jnp.concatenate(parts, axis=-1)
        y = jnp.dot(st, cmat_ref[s], preferred_element_type=F32)
        sl = slice(s * SLAB, (s + 1) * SLAB)
        y = y + d_ref[:, sl] * u[:, sl]
        store(sl, jax.nn.gelu(y))


def _s5_step(z, t, state, abar, *, pitch, tiles_half):
    nv = tiles_half // SUBLANES
    new = []
    for k in range(nv):
        idx_re = pl.ds(k * SUBLANES * pitch + t, SUBLANES, stride=pitch)
        idx_im = pl.ds((tiles_half + k * SUBLANES) * pitch + t, SUBLANES, stride=pitch)
        ar, ai = abar[k]
        sr, si = state[k]
        nr = ar * sr - ai * si + z[idx_re, :]
        ni = ar * si + ai * sr + z[idx_im, :]
        z[idx_re, :] = nr
        z[idx_im, :] = ni
        new.append((nr, ni))
    return tuple(new)


def _s5_p_body(u_ref, bmat_ref, cmat_ref, abr_ref, abi_ref, d_ref, o_ref, sre_ref, sim_ref,
               z, st_re, st_im, *, rows, tiles_per_seq, n_slab, tiles_half):
    m = pl.program_id(0)
    pitch = _s5_pitch(rows)
    nv = tiles_half // SUBLANES
    u = u_ref[...]
    _s5_project_in(u, bmat_ref, z, rows=rows, pitch=pitch, n_slab=n_slab, tiles_half=tiles_half)

    @pl.when(m % tiles_per_seq == 0)
    def _():
        st_re[...] = jnp.zeros(st_re.shape, F32)
        st_im[...] = jnp.zeros(st_im.shape, F32)

    vsl = lambda k: slice(k * SUBLANES, (k + 1) * SUBLANES)
    abar = tuple((abr_ref[vsl(k), :], abi_ref[vsl(k), :]) for k in range(nv))
    state0 = tuple((st_re[vsl(k), :], st_im[vsl(k), :]) for k in range(nv))

    def step(t, state):
        return _s5_step(z, t, state, abar, pitch=pitch, tiles_half=tiles_half)

    state = lax.fori_loop(0, rows, step, state0, unroll=4)
    for k in range(nv):
        st_re[vsl(k), :] = state[k][0]
        st_im[vsl(k), :] = state[k][1]
    sre_ref[...] = st_re[...]
    sim_ref[...] = st_im[...]

    def store(sl, val):
        o_ref[:, sl] = val.astype(o_ref.dtype)

    _s5_project_out(u, z, cmat_ref, d_ref, store, rows=rows, pitch=pitch, n_slab=n_slab,
                    tiles_half=tiles_half)


def _s5_prompt(u, bmat, cmat, abar_re, abar_im, dskip, *, layer, rows, seq_len):
    m, d = u.shape
    n_slab = bmat.shape[1]
    wsel = lambda a: pl.BlockSpec((None,) + a.shape[1:], lambda i: (layer, 0, 0, 0))
    tiles_half = abar_re.shape[0]
    pitch = _s5_pitch(rows)
    tiles_per_seq = seq_len // rows
    n_seq = m // seq_len
    st_spec = pl.BlockSpec((None, tiles_half, LANES), lambda i: (i // tiles_per_seq, 0, 0))
    full = lambda a: pl.BlockSpec(a.shape, lambda i: (0,) * a.ndim)
    return pl.pallas_call(
        functools.partial(_s5_p_body, rows=rows, tiles_per_seq=tiles_per_seq, n_slab=n_slab,
                          tiles_half=tiles_half),
        out_shape=[jax.ShapeDtypeStruct((m, d), BF16),
                   jax.ShapeDtypeStruct((n_seq, tiles_half, LANES), F32),
                   jax.ShapeDtypeStruct((n_seq, tiles_half, LANES), F32)],
        grid=(m // rows,),
        in_specs=[pl.BlockSpec((rows, d), lambda i: (i, 0)),
                  wsel(bmat), wsel(cmat), full(abar_re), full(abar_im), full(dskip)],
        out_specs=[pl.BlockSpec((rows, d), lambda i: (i, 0)), st_spec, st_spec],
        scratch_shapes=[pltpu.VMEM((2 * tiles_half * pitch, LANES), F32),
                        pltpu.VMEM((tiles_half, LANES), F32),
                        pltpu.VMEM((tiles_half, LANES), F32)],
        compiler_params=_cparams(1),
        name="s5_prompt",
    )(u, bmat, cmat, abar_re, abar_im, dskip)


def _s5_s_body(u_ref, h0r_ref, h0i_ref, bmat_ref, cmat_ref, abr_ref, abi_ref, d_ref,
               o_ref, sre_ref, sim_ref, z, zh, *, nb, seq, n_slab, tiles_half):
    rows = nb * seq
    pitch = _s5_pitch(rows)
    hp = _s5_pitch(nb)
    nv = tiles_half // SUBLANES
    d = u_ref.shape[2]
    u = u_ref[...].reshape(rows, d)
    _s5_project_in(u, bmat_ref, z, rows=rows, pitch=pitch, n_slab=n_slab, tiles_half=tiles_half)
    for j in range(tiles_half):
        cs = slice(j * LANES, (j + 1) * LANES)
        zh[j * hp:j * hp + nb, :] = h0r_ref[:, cs]
        zh[(tiles_half + j) * hp:(tiles_half + j) * hp + nb, :] = h0i_ref[:, cs]

    vsl = lambda k: slice(k * SUBLANES, (k + 1) * SUBLANES)
    abar = tuple((abr_ref[vsl(k), :], abi_ref[vsl(k), :]) for k in range(nv))

    def one_seq(b, carry):
        state = []
        for k in range(nv):
            idx_re = pl.ds(k * SUBLANES * hp + b, SUBLANES, stride=hp)
            idx_im = pl.ds((tiles_half + k * SUBLANES) * hp + b, SUBLANES, stride=hp)
            state.append((zh[idx_re, :], zh[idx_im, :]))
        state = tuple(state)
        for t in range(seq):
            state = _s5_step(z, t * nb + b, state, abar, pitch=pitch, tiles_half=tiles_half)
        for k in range(nv):
            idx_re = pl.ds(k * SUBLANES * hp + b, SUBLANES, stride=hp)
            idx_im = pl.ds((tiles_half + k * SUBLANES) * hp + b, SUBLANES, stride=hp)
            zh[idx_re, :] = state[k][0]
            zh[idx_im, :] = state[k][1]
        return carry

    lax.fori_loop(0, nb, one_seq, 0)
    for j in range(tiles_half):
        cs = slice(j * LANES, (j + 1) * LANES)
        sre_ref[:, cs] = zh[j * hp:j * hp + nb, :]
        sim_ref[:, cs] = zh[(tiles_half + j) * hp:(tiles_half + j) * hp + nb, :]

    def store(sl, val):
        o_ref[:, :, sl] = val.reshape(seq, nb, val.shape[1]).astype(o_ref.dtype)

    _s5_project_out(u, z, cmat_ref, d_ref, store, rows=rows, pitch=pitch, n_slab=n_slab,
                    tiles_half=tiles_half)


def _s5_sample(u, h0_re, h0_im, bmat, cmat, abar_re, abar_im, dskip, *, layer, nb):
    seq, b, d = u.shape
    nstate = h0_re.shape[1]
    n_slab = bmat.shape[1]
    wsel = lambda a: pl.BlockSpec((None,) + a.shape[1:], lambda i: (layer, 0, 0, 0))
    tiles_half = abar_re.shape[0]
    pitch = _s5_pitch(nb * seq)
    hp = _s5_pitch(nb)
    full = lambda a: pl.BlockSpec(a.shape, lambda i: (0,) * a.ndim)
    st_spec = pl.BlockSpec((nb, nstate), lambda i: (i, 0))
    u_spec = pl.BlockSpec((seq, nb, d), lambda i: (0, i, 0))
    return pl.pallas_call(
        functools.partial(_s5_s_body, nb=nb, seq=seq, n_slab=n_slab, tiles_half=tiles_half),
        out_shape=[jax.ShapeDtypeStruct((seq, b, d), BF16),
                   jax.ShapeDtypeStruct(h0_re.shape, F32),
                   jax.ShapeDtypeStruct(h0_im.shape, F32)],
        grid=(b // nb,),
        in_specs=[u_spec, st_spec, st_spec,
                  wsel(bmat), wsel(cmat), full(abar_re), full(abar_im), full(dskip)],
        out_specs=[u_spec, st_spec, st_spec],
        scratch_shapes=[pltpu.VMEM((2 * tiles_half * pitch, LANES), F32),
                        pltpu.VMEM((2 * tiles_half * hp, LANES), F32)],
        compiler_params=_cparams(1),
        name="s5_sample",
    )(u, h0_re, h0_im, bmat, cmat, abar_re, abar_im, dskip)


def _hn_dtype(i):
    return F32 if (i % N_MIXERS == 1) else BF16


def _run_sample(x, states, p):
    seq, nb, d = x.shape
    m = seq * nb
    depth = p["norm_mix"].shape[0]
    conv_st, ssm_re_st, ssm_im_st, ffn_st = states
    new_conv, new_re, new_im, new_ffn, wbf = [], [], [], [], []
    x = x.reshape(m, d)
    hn = _rmsnorm(x, p["norm_mix"][0], _hn_dtype(0), m)
    y = None
    for i in range(depth):
        j = i // N_MIXERS
        w = {}
        if i % N_MIXERS == 0:
            u, w["in_a"], w["in_b"] = _mm(hn, p["conv_w_in"], glu=True, bm=m, bn=512, layer=j)
            c, st = _conv_sample(conv_st, u.reshape(seq, nb, d), p["conv_dw"], p["conv_dw_b"],
                                 p["conv_ln_g"], p["conv_ln_b"], j, bc=256)
            new_conv.append(st)
            x, hn, w["out"] = _mm(c.reshape(m, d), p["conv_w_out"], glu=False, bm=m, bn=512,
                                  layer=j, resid=x, gamma=p["norm_ffn"][i], hn_dtype=BF16)
        else:
            v, s_re, s_im = _s5_sample(hn.reshape(seq, nb, d), ssm_re_st[j].reshape(nb, -1),
                                       ssm_im_st[j].reshape(nb, -1), p["s5_bmat"],
                                       p["s5_cmat"], p["s5_abar_re"][j], p["s5_abar_im"][j],
                                       p["ssm_D"][j].reshape(1, d), layer=j, nb=32)
            new_re.append(s_re.reshape(ssm_re_st.shape[1:]))
            new_im.append(s_im.reshape(ssm_im_st.shape[1:]))
            x, hn, w["glu_a"], w["glu_b"] = _mm(
                v.reshape(m, d), p["ssm_w_glu"], glu=True, bm=m, bn=512, layer=j, resid=x,
                gamma=p["norm_ffn"][i], hn_dtype=BF16)
        act, st, w["gate"], w["up"] = _ffn1_sample(hn, p["ffn_w_gate"], p["ffn_w_up"],
                                                   p["ffn_conv"], ffn_st, i, bn=512, seq=seq)
        new_ffn.append(st)
        last = i == depth - 1
        gamma = p["norm_final"] if last else p["norm_mix"][i + 1]
        outs = _mm(act, p["ffn_w_down"], glu=False, bm=m, bn=256, layer=i, resid=x, gamma=gamma,
                   hn_dtype=F32 if last else _hn_dtype(i + 1), write_x=not last)
        if last:
            y, w["down"] = outs
        else:
            x, hn, w["down"] = outs
        wbf.append(w)
    return (y.reshape(seq, nb, d), jnp.stack(new_conv), jnp.stack(new_re), jnp.stack(new_im),
            jnp.stack(new_ffn)), wbf


def _run_prompt(x, p, wbf, *, seq_len, bm):
    m, d = x.shape
    n_seq = m // seq_len
    depth = p["norm_mix"].shape[0]
    new_conv, new_re, new_im, new_ffn = [], [], [], []
    hn = None
    y = None
    for i in range(depth):
        j = i // N_MIXERS
        w = wbf[i]
        if i % N_MIXERS == 0:
            if i == 0:
                (u,) = _mm(x, (w["in_a"], w["in_b"]), glu=True, bm=bm, bn=1024,
                           prenorm=p["norm_mix"][0])
            else:
                (u,) = _mm(hn, (w["in_a"], w["in_b"]), glu=True, bm=bm, bn=1024)
            cs = p["conv_dw"].shape[1] - 1
            new_conv.append(u.reshape(n_seq, seq_len, d)[:, -cs:])
            c = _conv_prompt(u, p["conv_dw"], p["conv_dw_b"], p["conv_ln_g"], p["conv_ln_b"], j,
                             bm=256, seq_len=seq_len)
            x, hn = _mm_resident(c, (w["out"],), glu=False, bm=512, bn=512, resid=x,
                                 gamma=p["norm_ffn"][i], hn_dtype=BF16)
        else:
            v, s_re, s_im = _s5_prompt(hn, p["s5_bmat"], p["s5_cmat"], p["s5_abar_re"][j],
                                       p["s5_abar_im"][j], p["ssm_D"][j].reshape(1, d), layer=j, rows=256,
                                       seq_len=seq_len)
            g, pdim = p["ssm_A_re"].shape[1:]
            new_re.append(s_re.reshape(n_seq, g, pdim))
            new_im.append(s_im.reshape(n_seq, g, pdim))
            x, hn = _mm_resident(v, (w["glu_a"], w["glu_b"]), glu=True, bm=512, bn=512, resid=x,
                        gamma=p["norm_ffn"][i], hn_dtype=BF16)
        act, tail = _ffn1_prompt(hn, w["gate"], w["up"], p["ffn_conv"], i, bm=seq_len, bn=512,
                                 seq_len=seq_len)
        fs = p["ffn_conv"].shape[1] - 1
        new_ffn.append(tail[:, -fs:])
        last = i == depth - 1
        gamma = p["norm_final"] if last else p["norm_mix"][i + 1]
        outs = _mm_resident(act, (w["down"],), glu=False, bm=256, bn=512, resid=x, gamma=gamma,
                   hn_dtype=F32 if last else _hn_dtype(i + 1), write_x=not last)
        if last:
            (y,) = outs
        else:
            x, hn = outs
    return (y, jnp.stack(new_conv), jnp.stack(new_re), jnp.stack(new_im), jnp.stack(new_ffn))


def kernel(x_prompt, x_sample, state_conv, state_ssm_re, state_ssm_im, state_ffn, norm_mix, norm_ffn, norm_final, conv_w_in, conv_dw, conv_dw_b, conv_ln_g, conv_ln_b, conv_w_out, ssm_A_re, ssm_A_im, ssm_log_dt, ssm_B_re, ssm_B_im, ssm_C_re, ssm_C_im, ssm_D, ssm_w_glu, ffn_w_gate, ffn_w_up, ffn_conv, ffn_w_down):
    bp, sp, d = x_prompt.shape
    abr, abi, bbr, bbi = _s5_discretise(ssm_A_re, ssm_A_im, ssm_log_dt, ssm_B_re, ssm_B_im)
    ns, g, pdim = ssm_A_re.shape
    bmat, cmat = _s5_block_weights(bbr, bbi, ssm_C_re, ssm_C_im)
    tiles_half = g * pdim // LANES
    p = dict(
        norm_mix=norm_mix, norm_ffn=norm_ffn, norm_final=norm_final,
        conv_w_in=conv_w_in, conv_dw=conv_dw, conv_dw_b=conv_dw_b,
        conv_ln_g=conv_ln_g, conv_ln_b=conv_ln_b, conv_w_out=conv_w_out,
        ssm_A_re=ssm_A_re, ssm_D=ssm_D, ssm_w_glu=ssm_w_glu,
        s5_bmat=bmat, s5_cmat=cmat,
        s5_abar_re=abr.reshape(ns, tiles_half, LANES), s5_abar_im=abi.reshape(ns, tiles_half, LANES),
        ffn_w_gate=ffn_w_gate, ffn_w_up=ffn_w_up, ffn_conv=ffn_conv, ffn_w_down=ffn_w_down,
    )
    tm = lambda a: jnp.swapaxes(a, -3, -2)
    (y_s, conv_s, re_s, im_s, ffn_s), wbf = _run_sample(
        tm(x_sample), (tm(state_conv), state_ssm_re, state_ssm_im, tm(state_ffn)), p)
    y_p, conv_p, re_p, im_p, ffn_p = _run_prompt(
        x_prompt.reshape(bp * sp, d), p, wbf, seq_len=sp, bm=1024)
    return (y_p.reshape(bp, sp, d), tm(y_s), conv_p, tm(conv_s), re_p, im_p, re_s, im_s, ffn_p,
            tm(ffn_s))
```

```python
import functools

import jax
import jax.numpy as jnp
from jax import lax
from jax.experimental import pallas as pl
from jax.experimental.pallas import tpu as pltpu

F32 = jnp.float32
BF16 = jnp.bfloat16

EPS = 1e-6
N_MIXERS = 2
GROUP_SIZE = 16
LANES = 128
SUBLANES = 8
VMEM_LIMIT = 56 * 1024 * 1024
ROW_CHUNK = 256
FFN_ROW_CHUNK = 512


def _cparams(n_axes):
    return pltpu.CompilerParams(
        dimension_semantics=("arbitrary",) * n_axes, vmem_limit_bytes=VMEM_LIMIT)


def _rms_scale(ss, n):
    return lax.rsqrt(ss / n + EPS)


def _rmsnorm_body(x_ref, g_ref, o_ref):
    x = x_ref[...]
    ss = jnp.sum(x * x, axis=-1, keepdims=True)
    o_ref[...] = ((x * _rms_scale(ss, x.shape[-1])) * g_ref[...]).astype(o_ref.dtype)


def _rmsnorm(x, g, out_dtype, bm):
    m, d = x.shape
    return pl.pallas_call(
        _rmsnorm_body,
        out_shape=jax.ShapeDtypeStruct((m, d), out_dtype),
        grid=(m // bm,),
        in_specs=[pl.BlockSpec((bm, d), lambda i: (i, 0)),
                  pl.BlockSpec((1, d), lambda i: (0, 0))],
        out_specs=pl.BlockSpec((bm, d), lambda i: (i, 0)),
        compiler_params=_cparams(1),
        name="rmsnorm",
    )(x, g.reshape(1, d))


def _mm_body(*refs, glu, has_res, write_x, emit, prenorm, nn, bn, n_out):
    it = iter(refs)
    lhs_ref = next(it)
    pg_ref = next(it) if prenorm else None
    wa_ref = next(it)
    wb_ref = next(it) if glu else None
    res_ref = next(it) if has_res else None
    g_ref = next(it) if has_res else None
    xo_ref = next(it) if write_x else None
    hn_ref = next(it) if has_res else None
    wao_ref = next(it) if emit else None
    wbo_ref = next(it) if (emit and glu) else None
    rowbuf = next(it) if has_res else None
    ss_ref = next(it) if has_res else None
    hn_scr = next(it) if prenorm else None

    n = pl.program_id(1)
    bm = lhs_ref.shape[0]
    if prenorm:
        @pl.when(n == 0)
        def _():
            x = lhs_ref[...]
            ss = jnp.sum(x * x, axis=-1, keepdims=True)
            hn_scr[...] = ((x * _rms_scale(ss, x.shape[-1])) * pg_ref[...]).astype(BF16)
        lhs_ref = hn_scr
    if emit:
        wao_ref[...] = wa_ref[...].astype(BF16)
        wa_ref = wao_ref
        if glu:
            wbo_ref[...] = wb_ref[...].astype(BF16)
            wb_ref = wbo_ref
    if has_res:
        @pl.when(n == 0)
        def _():
            ss_ref[...] = jnp.zeros(ss_ref.shape, F32)
    for r0 in range(0, bm, ROW_CHUNK):
        rs = slice(r0, min(r0 + ROW_CHUNK, bm))
        x = lhs_ref[rs, :]
        val = jnp.dot(x, wa_ref[...], preferred_element_type=F32)
        if glu:
            gate = jnp.dot(x, wb_ref[...], preferred_element_type=F32)
            val = val * jax.nn.sigmoid(gate)
        if has_res:
            val = res_ref[rs, :] + val
        if write_x:
            xo_ref[rs, :] = val
        if has_res:
            rowbuf[n, rs, :] = val
            ss_ref[rs, :] = ss_ref[rs, :] + jnp.sum(val * val, axis=-1, keepdims=True)

    if has_res:
        @pl.when(n == nn - 1)
        def _():
            scale = _rms_scale(ss_ref[...], n_out)
            for j in range(nn):
                sl = slice(j * bn, (j + 1) * bn)
                hn_ref[:, sl] = ((rowbuf[j] * scale) * g_ref[:, sl]).astype(hn_ref.dtype)


def _mm(lhs, w, *, glu, bm, bn, layer=None, resid=None, gamma=None, hn_dtype=None, write_x=True,
        prenorm=None):
    m, k = lhs.shape
    emit = layer is not None
    if emit:
        assert m == bm
        n_out = w.shape[2] // (2 if glu else 1)
    else:
        n_out = w[0].shape[1]
    nn = n_out // bn
    has_res = resid is not None
    in_specs = [pl.BlockSpec((bm, k), lambda i, j: (i, 0))]
    args = [lhs]
    if prenorm is not None:
        in_specs.append(pl.BlockSpec((1, k), lambda i, j: (0, 0)))
        args.append(prenorm.reshape(1, k))
    if emit:
        in_specs.append(pl.BlockSpec((None, k, bn), lambda i, j: (layer, 0, j)))
        args.append(w)
        if glu:
            in_specs.append(pl.BlockSpec((None, k, bn), lambda i, j: (layer, 0, j + nn)))
            args.append(w)
    else:
        for wi in w:
            in_specs.append(pl.BlockSpec((k, bn), lambda i, j: (0, j)))
            args.append(wi)
    out_shape, out_specs, scratch = [], [], []
    if has_res:
        in_specs += [pl.BlockSpec((bm, bn), lambda i, j: (i, j)),
                     pl.BlockSpec((1, n_out), lambda i, j: (0, 0))]
        args += [resid, gamma.reshape(1, n_out)]
    if write_x:
        out_shape.append(jax.ShapeDtypeStruct((m, n_out), F32))
        out_specs.append(pl.BlockSpec((bm, bn), lambda i, j: (i, j)))
    if has_res:
        out_shape.append(jax.ShapeDtypeStruct((m, n_out), hn_dtype))
        out_specs.append(pl.BlockSpec((bm, n_out), lambda i, j: (i, 0)))
        scratch += [pltpu.VMEM((nn, bm, bn), F32), pltpu.VMEM((bm, 1), F32)]
    if emit:
        for _ in range(2 if glu else 1):
            out_shape.append(jax.ShapeDtypeStruct((k, n_out), BF16))
            out_specs.append(pl.BlockSpec((k, bn), lambda i, j: (0, j)))
    if prenorm is not None:
        scratch.append(pltpu.VMEM((bm, k), BF16))
    return pl.pallas_call(
        functools.partial(_mm_body, glu=glu, has_res=has_res, write_x=write_x, emit=emit,
                          prenorm=prenorm is not None, nn=nn, bn=bn, n_out=n_out),
        out_shape=out_shape,
        grid=(m // bm, nn),
        in_specs=in_specs,
        out_specs=out_specs,
        scratch_shapes=scratch,
        compiler_params=_cparams(2),
        name="mm_glu" if glu else "mm_lin",
    )(*args)


def _mmr_body(*refs, glu, write_x, bn, n_out):
    it = iter(refs)
    lhs_ref = next(it)
    wa_ref = next(it)
    wb_ref = next(it) if glu else None
    res_ref = next(it)
    g_ref = next(it)
    xo_ref = next(it) if write_x else None
    hn_ref = next(it)
    bm = lhs_ref.shape[0]
    rows = min(ROW_CHUNK, bm)
    for r0 in range(0, bm, rows):
        rs = slice(r0, r0 + rows)
        x = lhs_ref[rs, :]
        ss = jnp.zeros((rows, 1), F32)
        vals = []
        for j in range(n_out // bn):
            sl = slice(j * bn, (j + 1) * bn)
            val = jnp.dot(x, wa_ref[:, sl], preferred_element_type=F32)
            if glu:
                gate = jnp.dot(x, wb_ref[:, sl], preferred_element_type=F32)
                val = val * jax.nn.sigmoid(gate)
            val = res_ref[rs, sl] + val
            if write_x:
                xo_ref[rs, sl] = val
            ss = ss + jnp.sum(val * val, axis=-1, keepdims=True)
            vals.append(val)
        scale = _rms_scale(ss, n_out)
        for j, val in enumerate(vals):
            sl = slice(j * bn, (j + 1) * bn)
            hn_ref[rs, sl] = ((val * scale) * g_ref[:, sl]).astype(hn_ref.dtype)


def _mm_resident(lhs, w, *, glu, bm, bn, resid, gamma, hn_dtype, write_x=True):
    m, k = lhs.shape
    n_out = w[0].shape[1]
    row = lambda width: pl.BlockSpec((bm, width), lambda i: (i, 0))
    in_specs = [row(k)] + [pl.BlockSpec((k, n_out), lambda i: (0, 0)) for _ in w]
    in_specs += [row(n_out), pl.BlockSpec((1, n_out), lambda i: (0, 0))]
    out_shape, out_specs = [], []
    if write_x:
        out_shape.append(jax.ShapeDtypeStruct((m, n_out), F32))
        out_specs.append(row(n_out))
    out_shape.append(jax.ShapeDtypeStruct((m, n_out), hn_dtype))
    out_specs.append(row(n_out))
    return pl.pallas_call(
        functools.partial(_mmr_body, glu=glu, write_x=write_x, bn=bn, n_out=n_out),
        out_shape=out_shape,
        grid=(m // bm,),
        in_specs=in_specs,
        out_specs=out_specs,
        compiler_params=_cparams(1),
        name="mmr_glu" if glu else "mmr_lin",
    )(lhs, *w, resid, gamma.reshape(1, n_out))


def _shift_rows(g, tail, k):
    r = pltpu.roll(g, k, 0)
    t = pltpu.roll(tail, k, 0)
    i8 = lax.broadcasted_iota(jnp.int32, tail.shape, 0)
    head = jnp.where(i8 < k, t, r[0:SUBLANES, :])
    return jnp.concatenate([head, r[SUBLANES:, :]], axis=0)


def _ffn1_prompt_body(h_ref, wg_ref, wu_ref, cw_ref, act_ref, tail_ref, carry, *, bm, tiles_per_seq):
    m = pl.program_id(0)
    n = pl.program_id(1)
    cw = cw_ref[...]
    taps = cw.shape[0]

    @pl.when(m % tiles_per_seq == 0)
    def _():
        carry[n] = jnp.zeros(carry.shape[1:], F32)

    tail = carry[n]
    for r0 in range(0, bm, FFN_ROW_CHUNK):
        rs = slice(r0, r0 + FFN_ROW_CHUNK)
        h = h_ref[rs, :]
        g = jnp.dot(h, wg_ref[...], preferred_element_type=F32)
        up = jnp.dot(h, wu_ref[...], preferred_element_type=F32)
        gc = g * cw[taps - 1:taps, :]
        for k in range(taps - 1):
            gc = gc + _shift_rows(g, tail, taps - 1 - k) * cw[k:k + 1, :]
        act_ref[rs, :] = (jax.nn.silu(gc) * up).astype(act_ref.dtype)
        tail = g[FFN_ROW_CHUNK - SUBLANES:, :]
    carry[n] = tail
    tail_ref[...] = tail


def _ffn1_prompt(h, wg, wu, cw, layer, *, bm, bn, seq_len):
    m, k = h.shape
    f = wg.shape[1]
    nn = f // bn
    tiles_per_seq = seq_len // bm
    act, tails = pl.pallas_call(
        functools.partial(_ffn1_prompt_body, bm=bm, tiles_per_seq=tiles_per_seq),
        out_shape=[jax.ShapeDtypeStruct((m, f), BF16),
                   jax.ShapeDtypeStruct((m // bm, SUBLANES, f), F32)],
        grid=(m // bm, nn),
        in_specs=[pl.BlockSpec((bm, k), lambda i, j: (i, 0)),
                  pl.BlockSpec((k, bn), lambda i, j: (0, j)),
                  pl.BlockSpec((k, bn), lambda i, j: (0, j)),
                  pl.BlockSpec((None, cw.shape[1], bn), lambda i, j: (layer, 0, j))],
        out_specs=[pl.BlockSpec((bm, bn), lambda i, j: (i, j)),
                   pl.BlockSpec((None, SUBLANES, bn), lambda i, j: (i, 0, j))],
        scratch_shapes=[pltpu.VMEM((nn, SUBLANES, bn), F32)],
        compiler_params=_cparams(2),
        name="ffn_gate_up",
    )(h, wg, wu, cw)
    return act, tails[tiles_per_seq - 1::tiles_per_seq]


def _ffn1_sample_body(h_ref, wg_ref, wu_ref, cw_ref, c_ref, act_ref, st_ref, wgo_ref, wuo_ref, *,
                      nb, seq):
    wgo_ref[...] = wg_ref[...].astype(BF16)
    wuo_ref[...] = wu_ref[...].astype(BF16)
    h = h_ref[...]
    g = jnp.dot(h, wgo_ref[...], preferred_element_type=F32)
    up = jnp.dot(h, wuo_ref[...], preferred_element_type=F32)
    cw = cw_ref[...]
    taps = cw.shape[0]
    hist = [c_ref[r] for r in range(taps - 1)] + [g[t * nb:(t + 1) * nb, :] for t in range(seq)]
    for t in range(seq):
        gc = hist[t] * cw[0:1, :]
        for k in range(1, taps):
            gc = gc + hist[t + k] * cw[k:k + 1, :]
        act_ref[t * nb:(t + 1) * nb, :] = (
            jax.nn.silu(gc) * up[t * nb:(t + 1) * nb, :]).astype(act_ref.dtype)
    for r in range(taps - 1):
        st_ref[r] = hist[seq + r]


def _ffn1_sample(h, wg, wu, cw, cache, layer, *, bn, seq):
    m, k = h.shape
    f = wg.shape[2]
    nb = m // seq
    hist = cw.shape[1] - 1
    wspec = pl.BlockSpec((None, k, bn), lambda j: (layer, 0, j))
    wout = pl.BlockSpec((k, bn), lambda j: (0, j))
    return pl.pallas_call(
        functools.partial(_ffn1_sample_body, nb=nb, seq=seq),
        out_shape=[jax.ShapeDtypeStruct((m, f), BF16),
                   jax.ShapeDtypeStruct((hist, nb, f), F32),
                   jax.ShapeDtypeStruct((k, f), BF16),
                   jax.ShapeDtypeStruct((k, f), BF16)],
        grid=(f // bn,),
        in_specs=[pl.BlockSpec((m, k), lambda j: (0, 0)), wspec, wspec,
                  pl.BlockSpec((None, cw.shape[1], bn), lambda j: (layer, 0, j)),
                  pl.BlockSpec((None, hist, nb, bn), lambda j: (layer, 0, 0, j))],
        out_specs=[pl.BlockSpec((m, bn), lambda j: (0, j)),
                   pl.BlockSpec((hist, nb, bn), lambda j: (0, 0, j)), wout, wout],
        compiler_params=_cparams(1),
        name="ffn_gate_up_s",
    )(h, wg, wu, cw, cache)


CONV_ROWS = 32
HALO = 32


def _ln_silu(c, g, b):
    mu = jnp.mean(c, axis=-1, keepdims=True)
    d = c - mu
    var = jnp.mean(d * d, axis=-1, keepdims=True)
    y = d * lax.rsqrt(var + EPS)
    return jax.nn.silu(y * g + b)


def _conv_p_body(u_ref, dw_ref, db_ref, lg_ref, lb_ref, o_ref, ubuf, cbuf, *, bm, taps,
                 tiles_per_seq):
    m = pl.program_id(0)
    d = u_ref.shape[1]
    nc = d // LANES

    @pl.when(m % tiles_per_seq == 0)
    def _():
        ubuf[:, 0:HALO, :] = jnp.zeros((nc, HALO, LANES), F32)

    @pl.when(m % tiles_per_seq != 0)
    def _():
        ubuf[:, 0:HALO, :] = ubuf[:, bm:bm + HALO, :]

    for c in range(nc):
        ubuf[c, HALO:HALO + bm, :] = u_ref[:, c * LANES:(c + 1) * LANES]
    off = HALO - (taps - 1)

    def lane_chunk(c, carry):
        wv = dw_ref[c]
        bias = db_ref[c]
        for r0 in range(0, bm, CONV_ROWS):
            acc = jnp.zeros((CONV_ROWS, LANES), F32)
            for k in range(taps):
                acc = acc + ubuf[c, r0 + off + k:r0 + off + k + CONV_ROWS, :] * wv[k:k + 1, :]
            cbuf[c, r0:r0 + CONV_ROWS, :] = acc + bias
        return carry

    lax.fori_loop(0, nc, lane_chunk, 0)
    cfull = jnp.concatenate([cbuf[c] for c in range(nc)], axis=-1)
    o_ref[...] = _ln_silu(cfull, lg_ref[...], lb_ref[...]).astype(o_ref.dtype)


def _lane_chunked(x):
    l, r, d = x.shape
    return x.reshape(l, r, d // LANES, LANES).transpose(0, 2, 1, 3)


def _vec3(x):
    return x.reshape(x.shape[0], 1, x.shape[1])


def _conv_prompt(u, dw, db, lg, lb, layer, *, bm, seq_len):
    m, d = u.shape
    taps = dw.shape[1]
    nc = d // LANES
    vec = lambda: pl.BlockSpec((None, 1, d), lambda i: (layer, 0, 0))
    return pl.pallas_call(
        functools.partial(_conv_p_body, bm=bm, taps=taps, tiles_per_seq=seq_len // bm),
        out_shape=jax.ShapeDtypeStruct((m, d), BF16),
        grid=(m // bm,),
        in_specs=[pl.BlockSpec((bm, d), lambda i: (i, 0)),
                  pl.BlockSpec((None, nc, taps, LANES), lambda i: (layer, 0, 0, 0)),
                  pl.BlockSpec((None, nc, 1, LANES), lambda i: (layer, 0, 0, 0)),
                  vec(), vec()],
        out_specs=pl.BlockSpec((bm, d), lambda i: (i, 0)),
        scratch_shapes=[pltpu.VMEM((nc, HALO + bm, LANES), F32),
                        pltpu.VMEM((nc, bm, LANES), F32)],
        compiler_params=_cparams(1),
        name="conv_prompt",
    )(u, _lane_chunked(dw), _lane_chunked(_vec3(db)), _vec3(lg), _vec3(lb))


def _conv_s_body(cache_ref, u_ref, dw_ref, db_ref, lg_ref, lb_ref, o_ref, st_ref, cfull, *,
                 taps, seq, nsteps):
    c = pl.program_id(0)
    hist = taps - 1
    ext = lambda r: cache_ref[r] if r < hist else u_ref[r - hist]
    wv = dw_ref[...]
    bias = db_ref[...]
    for t in range(seq):
        acc = ext(t) * wv[0:1, :]
        for k in range(1, taps):
            acc = acc + ext(t + k) * wv[k:k + 1, :]
        cfull[c, t] = acc + bias
    for r in range(hist):
        st_ref[r] = ext(r + seq)

    @pl.when(c == nsteps - 1)
    def _():
        for t in range(seq):
            row = jnp.concatenate([cfull[cc, t] for cc in range(nsteps)], axis=-1)
            o_ref[t] = _ln_silu(row, lg_ref[...], lb_ref[...]).astype(o_ref.dtype)


def _conv_sample(cache, u, dw, db, lg, lb, layer, *, bc):
    seq, b, d = u.shape
    taps = dw.shape[1]
    hist = taps - 1
    nsteps = d // bc
    vec = lambda: pl.BlockSpec((None, 1, d), lambda i: (layer, 0, 0))
    return pl.pallas_call(
        functools.partial(_conv_s_body, taps=taps, seq=seq, nsteps=nsteps),
        out_shape=[jax.ShapeDtypeStruct((seq, b, d), BF16),
                   jax.ShapeDtypeStruct((hist, b, d), F32)],
        grid=(nsteps,),
        in_specs=[pl.BlockSpec((None, hist, b, bc), lambda i: (layer, 0, 0, i)),
                  pl.BlockSpec((seq, b, bc), lambda i: (0, 0, i)),
                  pl.BlockSpec((None, taps, bc), lambda i: (layer, 0, i)),
                  pl.BlockSpec((None, 1, bc), lambda i: (layer, 0, i)),
                  vec(), vec()],
        out_specs=[pl.BlockSpec((seq, b, d), lambda i: (0, 0, 0)),
                   pl.BlockSpec((hist, b, bc), lambda i: (0, 0, i))],
        scratch_shapes=[pltpu.VMEM((nsteps, seq, b, bc), F32)],
        compiler_params=_cparams(1),
        name="conv_sample",
    )(cache, u, dw, _vec3(db), _vec3(lg), _vec3(lb))


def _s5_abar(lam_re, lam_im, dt):
    mag = jnp.exp(lam_re * dt)
    ang = lam_im * dt
    return mag * jnp.cos(ang), mag * jnp.sin(ang)


def _s5_disc_body(are_ref, aim_ref, ldt_ref, arep_ref, airep_ref, bre_ref, bim_ref,
                  abr_ref, abi_ref, bbr_ref, bbi_ref):
    dt = jnp.exp(ldt_ref[...])
    abar_re, abar_im = _s5_abar(are_ref[...], aim_ref[...], dt)
    abr_ref[...] = abar_re
    abi_ref[...] = abar_im
    lam_re = arep_ref[...]
    lam_im = airep_ref[...]
    rep_re, rep_im = _s5_abar(lam_re, lam_im, dt)
    nr = rep_re - 1.0
    ni = rep_im
    den = lam_re * lam_re + lam_im * lam_im
    q_re = (nr * lam_re + ni * lam_im) / den
    q_im = (ni * lam_re - nr * lam_im) / den
    br = bre_ref[...]
    bi = bim_ref[...]
    bbr_ref[...] = q_re * br - q_im * bi
    bbi_ref[...] = q_re * bi + q_im * br


def _s5_discretise(a_re, a_im, log_dt, b_re, b_im):
    ns, g, p = a_re.shape
    rows = ns * g
    two = lambda x: x.reshape(rows, p)
    rep = lambda x: jnp.repeat(x.reshape(rows, p), GROUP_SIZE, axis=1)
    wide = lambda x: x.reshape(rows, p * GROUP_SIZE)
    outs = pl.pallas_call(
        _s5_disc_body,
        out_shape=[jax.ShapeDtypeStruct((rows, p), F32)] * 2
        + [jax.ShapeDtypeStruct((rows, p * GROUP_SIZE), F32)] * 2,
        name="s5_discretise",
    )(two(a_re), two(a_im), log_dt.reshape(rows, 1), rep(a_re), rep(a_im), wide(b_re), wide(b_im))
    abr, abi, bbr, bbi = outs
    return (abr.reshape(ns, g, p), abi.reshape(ns, g, p),
            bbr.reshape(ns, g, p, GROUP_SIZE), bbi.reshape(ns, g, p, GROUP_SIZE))


SLAB_GROUPS = 16
SLAB = SLAB_GROUPS * GROUP_SIZE


def _s5_expand_body(xbr_ref, xbi_ref, xcr_ref, xci_ref, bm_ref, cm_ref, *, p):
    gpt = LANES // p
    tiles_slab = SLAB_GROUPS // gpt
    rg = lax.broadcasted_iota(jnp.int32, (SLAB, LANES), 0) // GROUP_SIZE
    lg = lax.broadcasted_iota(jnp.int32, (SLAB, LANES), 1) // p
    for half, x_ref in enumerate((xbr_ref, xbi_ref)):
        x = x_ref[...]
        for a in range(tiles_slab):
            col = (half * tiles_slab + a) * LANES
            bm_ref[:, col:col + LANES] = jnp.where(rg == gpt * a + lg, x, 0.0).astype(BF16)
    cg = lax.broadcasted_iota(jnp.int32, (LANES, SLAB), 1) // GROUP_SIZE
    rc = lax.broadcasted_iota(jnp.int32, (LANES, SLAB), 0) // p
    for half, x_ref in enumerate((xcr_ref, xci_ref)):
        x = x_ref[...]
        for a in range(tiles_slab):
            row = (half * tiles_slab + a) * LANES
            cm_ref[row:row + LANES, :] = jnp.where(cg == gpt * a + rc, x, 0.0).astype(BF16)


def _s5_block_weights(bb_re, bb_im, c_re, c_im):
    ns, g, p, gs = bb_re.shape
    s = g // SLAB_GROUPS
    n = ns * s
    gpt = LANES // p
    bcomp = lambda x: jnp.tile(
        x.reshape(n, SLAB_GROUPS, p, gs).transpose(0, 1, 3, 2).reshape(n, SLAB, p), (1, 1, gpt))
    ccomp = lambda x: jnp.tile(
        x.reshape(n, SLAB_GROUPS, gs, p).transpose(0, 3, 1, 2).reshape(n, p, SLAB), (1, gpt, 1))
    width = 2 * SLAB_GROUPS * p
    bspec = pl.BlockSpec((None, SLAB, LANES), lambda i: (i, 0, 0))
    cspec = pl.BlockSpec((None, LANES, SLAB), lambda i: (i, 0, 0))
    bmat, cmat = pl.pallas_call(
        functools.partial(_s5_expand_body, p=p),
        out_shape=[jax.ShapeDtypeStruct((n, SLAB, width), BF16),
                   jax.ShapeDtypeStruct((n, width, SLAB), BF16)],
        grid=(n,),
        in_specs=[bspec, bspec, cspec, cspec],
        out_specs=[pl.BlockSpec((None, SLAB, width), lambda i: (i, 0, 0)),
                   pl.BlockSpec((None, width, SLAB), lambda i: (i, 0, 0))],
        compiler_params=_cparams(1),
        name="s5_expand",
    )(bcomp(bb_re), bcomp(bb_im), ccomp(c_re), ccomp(-c_im))
    return bmat.reshape(ns, s, SLAB, width), cmat.reshape(ns, s, width, SLAB)


def _s5_pitch(rows):
    return rows + SUBLANES // 2


def _s5_project_in(u, bmat_ref, z, *, rows, pitch, n_slab, tiles_half):
    tiles_slab = tiles_half // n_slab
    for s in range(n_slab):
        bu = jnp.dot(u[:, s * SLAB:(s + 1) * SLAB].astype(BF16), bmat_ref[s],
                     preferred_element_type=F32)
        for half in range(2):
            for q in range(tiles_slab):
                j = half * tiles_half + s * tiles_slab + q
                col = (half * tiles_slab + q) * LANES
                z[j * pitch:j * pitch + rows, :] = bu[:, col:col + LANES]


def _s5_project_out(u, z, cmat_ref, d_ref, store, *, rows, pitch, n_slab, tiles_half):
    tiles_slab = tiles_half // n_slab
    for s in range(n_slab):
        parts = []
        for half in range(2):
            for q in range(tiles_slab):
                j = half * tiles_half + s * tiles_slab + q
                parts.append(z[j * pitch:j * pitch + rows, :].astype(BF16))
        st = jnp.concatenate(parts, axis=-1)
        y = jnp.dot(st, cmat_ref[s], preferred_element_type=F32)
        sl = slice(s * SLAB, (s + 1) * SLAB)
        y = y + d_ref[:, sl] * u[:, sl]
        store(sl, jax.nn.gelu(y))


def _s5_step(z, t, state, abar, *, pitch, tiles_half):
    nv = tiles_half // SUBLANES
    new = []
    for k in range(nv):
        idx_re = pl.ds(k * SUBLANES * pitch + t, SUBLANES, stride=pitch)
        idx_im = pl.ds((tiles_half + k * SUBLANES) * pitch + t, SUBLANES, stride=pitch)
        ar, ai = abar[k]
        sr, si = state[k]
        nr = ar * sr - ai * si + z[idx_re, :]
        ni = ar * si + ai * sr + z[idx_im, :]
        z[idx_re, :] = nr
        z[idx_im, :] = ni
        new.append((nr, ni))
    return tuple(new)


def _s5_p_body(u_ref, bmat_ref, cmat_ref, abr_ref, abi_ref, d_ref, o_ref, sre_ref, sim_ref,
               z, st_re, st_im, *, rows, tiles_per_seq, n_slab, tiles_half):
    m = pl.program_id(0)
    pitch = _s5_pitch(rows)
    nv = tiles_half // SUBLANES
    u = u_ref[...]
    _s5_project_in(u, bmat_ref, z, rows=rows, pitch=pitch, n_slab=n_slab, tiles_half=tiles_half)

    @pl.when(m % tiles_per_seq == 0)
    def _():
        st_re[...] = jnp.zeros(st_re.shape, F32)
        st_im[...] = jnp.zeros(st_im.shape, F32)

    vsl = lambda k: slice(k * SUBLANES, (k + 1) * SUBLANES)
    abar = tuple((abr_ref[vsl(k), :], abi_ref[vsl(k), :]) for k in range(nv))
    state0 = tuple((st_re[vsl(k), :], st_im[vsl(k), :]) for k in range(nv))

    def step(t, state):
        return _s5_step(z, t, state, abar, pitch=pitch, tiles_half=tiles_half)

    state = lax.fori_loop(0, rows, step, state0, unroll=4)
    for k in range(nv):
        st_re[vsl(k), :] = state[k][0]
        st_im[vsl(k), :] = state[k][1]
    sre_ref[...] = st_re[...]
    sim_ref[...] = st_im[...]

    def store(sl, val):
        o_ref[:, sl] = val.astype(o_ref.dtype)

    _s5_project_out(u, z, cmat_ref, d_ref, store, rows=rows, pitch=pitch, n_slab=n_slab,
                    tiles_half=tiles_half)


def _s5_prompt(u, bmat, cmat, abar_re, abar_im, dskip, *, layer, rows, seq_len):
    m, d = u.shape
    n_slab = bmat.shape[1]
    wsel = lambda a: pl.BlockSpec((None,) + a.shape[1:], lambda i: (layer, 0, 0, 0))
    tiles_half = abar_re.shape[0]
    pitch = _s5_pitch(rows)
    tiles_per_seq = seq_len // rows
    n_seq = m // seq_len
    st_spec = pl.BlockSpec((None, tiles_half, LANES), lambda i: (i // tiles_per_seq, 0, 0))
    full = lambda a: pl.BlockSpec(a.shape, lambda i: (0,) * a.ndim)
    return pl.pallas_call(
        functools.partial(_s5_p_body, rows=rows, tiles_per_seq=tiles_per_seq, n_slab=n_slab,
                          tiles_half=tiles_half),
        out_shape=[jax.ShapeDtypeStruct((m, d), BF16),
                   jax.ShapeDtypeStruct((n_seq, tiles_half, LANES), F32),
                   jax.ShapeDtypeStruct((n_seq, tiles_half, LANES), F32)],
        grid=(m // rows,),
        in_specs=[pl.BlockSpec((rows, d), lambda i: (i, 0)),
                  wsel(bmat), wsel(cmat), full(abar_re), full(abar_im), full(dskip)],
        out_specs=[pl.BlockSpec((rows, d), lambda i: (i, 0)), st_spec, st_spec],
        scratch_shapes=[pltpu.VMEM((2 * tiles_half * pitch, LANES), F32),
                        pltpu.VMEM((tiles_half, LANES), F32),
                        pltpu.VMEM((tiles_half, LANES), F32)],
        compiler_params=_cparams(1),
        name="s5_prompt",
    )(u, bmat, cmat, abar_re, abar_im, dskip)


def _s5_s_body(u_ref, h0r_ref, h0i_ref, bmat_ref, cmat_ref, abr_ref, abi_ref, d_ref,
               o_ref, sre_ref, sim_ref, z, zh, *, nb, seq, n_slab, tiles_half):
    rows = nb * seq
    pitch = _s5_pitch(rows)
    hp = _s5_pitch(nb)
    nv = tiles_half // SUBLANES
    d = u_ref.shape[2]
    u = u_ref[...].reshape(rows, d)
    _s5_project_in(u, bmat_ref, z, rows=rows, pitch=pitch, n_slab=n_slab, tiles_half=tiles_half)
    for j in range(tiles_half):
        cs = slice(j * LANES, (j + 1) * LANES)
        zh[j * hp:j * hp + nb, :] = h0r_ref[:, cs]
        zh[(tiles_half + j) * hp:(tiles_half + j) * hp + nb, :] = h0i_ref[:, cs]

    vsl = lambda k: slice(k * SUBLANES, (k + 1) * SUBLANES)
    abar = tuple((abr_ref[vsl(k), :], abi_ref[vsl(k), :]) for k in range(nv))

    def one_seq(b, carry):
        state = []
        for k in range(nv):
            idx_re = pl.ds(k * SUBLANES * hp + b, SUBLANES, stride=hp)
            idx_im = pl.ds((tiles_half + k * SUBLANES) * hp + b, SUBLANES, stride=hp)
            state.append((zh[idx_re, :], zh[idx_im, :]))
        state = tuple(state)
        for t in range(seq):
            state = _s5_step(z, t * nb + b, state, abar, pitch=pitch, tiles_half=tiles_half)
        for k in range(nv):
            idx_re = pl.ds(k * SUBLANES * hp + b, SUBLANES, stride=hp)
            idx_im = pl.ds((tiles_half + k * SUBLANES) * hp + b, SUBLANES, stride=hp)
            zh[idx_re, :] = state[k][0]
            zh[idx_im, :] = state[k][1]
        return carry

    lax.fori_loop(0, nb, one_seq, 0)
    for j in range(tiles_half):
        cs = slice(j * LANES, (j + 1) * LANES)
        sre_ref[:, cs] = zh[j * hp:j * hp + nb, :]
        sim_ref[:, cs] = zh[(tiles_half + j) * hp:(tiles_half + j) * hp + nb, :]

    def store(sl, val):
        o_ref[:, :, sl] = val.reshape(seq, nb, val.shape[1]).astype(o_ref.dtype)

    _s5_project_out(u, z, cmat_ref, d_ref, store, rows=rows, pitch=pitch, n_slab=n_slab,
                    tiles_half=tiles_half)


def _s5_sample(u, h0_re, h0_im, bmat, cmat, abar_re, abar_im, dskip, *, layer, nb):
    seq, b, d = u.shape
    nstate = h0_re.shape[1]
    n_slab = bmat.shape[1]
    wsel = lambda a: pl.BlockSpec((None,) + a.shape[1:], lambda i: (layer, 0, 0, 0))
    tiles_half = abar_re.shape[0]
    pitch = _s5_pitch(nb * seq)
    hp = _s5_pitch(nb)
    full = lambda a: pl.BlockSpec(a.shape, lambda i: (0,) * a.ndim)
    st_spec = pl.BlockSpec((nb, nstate), lambda i: (i, 0))
    u_spec = pl.BlockSpec((seq, nb, d), lambda i: (0, i, 0))
    return pl.pallas_call(
        functools.partial(_s5_s_body, nb=nb, seq=seq, n_slab=n_slab, tiles_half=tiles_half),
        out_shape=[jax.ShapeDtypeStruct((seq, b, d), BF16),
                   jax.ShapeDtypeStruct(h0_re.shape, F32),
                   jax.ShapeDtypeStruct(h0_im.shape, F32)],
        grid=(b // nb,),
        in_specs=[u_spec, st_spec, st_spec,
                  wsel(bmat), wsel(cmat), full(abar_re), full(abar_im), full(dskip)],
        out_specs=[u_spec, st_spec, st_spec],
        scratch_shapes=[pltpu.VMEM((2 * tiles_half * pitch, LANES), F32),
                        pltpu.VMEM((2 * tiles_half * hp, LANES), F32)],
        compiler_params=_cparams(1),
        name="s5_sample",
    )(u, h0_re, h0_im, bmat, cmat, abar_re, abar_im, dskip)


def _hn_dtype(i):
    return F32 if (i % N_MIXERS == 1) else BF16


def _run_sample(x, states, p):
    seq, nb, d = x.shape
    m = seq * nb
    depth = p["norm_mix"].shape[0]
    conv_st, ssm_re_st, ssm_im_st, ffn_st = states
    new_conv, new_re, new_im, new_ffn, wbf = [], [], [], [], []
    x = x.reshape(m, d)
    hn = _rmsnorm(x, p["norm_mix"][0], _hn_dtype(0), m)
    y = None
    for i in range(depth):
        j = i // N_MIXERS
        w = {}
        if i % N_MIXERS == 0:
            u, w["in_a"], w["in_b"] = _mm(hn, p["conv_w_in"], glu=True, bm=m, bn=512, layer=j)
            c, st = _conv_sample(conv_st, u.reshape(seq, nb, d), p["conv_dw"], p["conv_dw_b"],
                                 p["conv_ln_g"], p["conv_ln_b"], j, bc=256)
            new_conv.append(st)
            x, hn, w["out"] = _mm(c.reshape(m, d), p["conv_w_out"], glu=False, bm=m, bn=512,
                                  layer=j, resid=x, gamma=p["norm_ffn"][i], hn_dtype=BF16)
        else:
            v, s_re, s_im = _s5_sample(hn.reshape(seq, nb, d), ssm_re_st[j].reshape(nb, -1),
                                       ssm_im_st[j].reshape(nb, -1), p["s5_bmat"],
                                       p["s5_cmat"], p["s5_abar_re"][j], p["s5_abar_im"][j],
                                       p["ssm_D"][j].reshape(1, d), layer=j, nb=32)
            new_re.append(s_re.reshape(ssm_re_st.shape[1:]))
            new_im.append(s_im.reshape(ssm_im_st.shape[1:]))
            x, hn, w["glu_a"], w["glu_b"] = _mm(
                v.reshape(m, d), p["ssm_w_glu"], glu=True, bm=m, bn=512, layer=j, resid=x,
                gamma=p["norm_ffn"][i], hn_dtype=BF16)
        act, st, w["gate"], w["up"] = _ffn1_sample(hn, p["ffn_w_gate"], p["ffn_w_up"],
                                                   p["ffn_conv"], ffn_st, i, bn=512, seq=seq)
        new_ffn.append(st)
        last = i == depth - 1
        gamma = p["norm_final"] if last else p["norm_mix"][i + 1]
        outs = _mm(act, p["ffn_w_down"], glu=False, bm=m, bn=256, layer=i, resid=x, gamma=gamma,
                   hn_dtype=F32 if last else _hn_dtype(i + 1), write_x=not last)
        if last:
            y, w["down"] = outs
        else:
            x, hn, w["down"] = outs
        wbf.append(w)
    return (y.reshape(seq, nb, d), jnp.stack(new_conv), jnp.stack(new_re), jnp.stack(new_im),
            jnp.stack(new_ffn)), wbf


def _run_prompt(x, p, wbf, *, seq_len, bm):
    m, d = x.shape
    n_seq = m // seq_len
    depth = p["norm_mix"].shape[0]
    new_conv, new_re, new_im, new_ffn = [], [], [], []
    hn = None
    y = None
    for i in range(depth):
        j = i // N_MIXERS
        w = wbf[i]
        if i % N_MIXERS == 0:
            if i == 0:
                (u,) = _mm(x, (w["in_a"], w["in_b"]), glu=True, bm=bm, bn=1024,
                           prenorm=p["norm_mix"][0])
            else:
                (u,) = _mm(hn, (w["in_a"], w["in_b"]), glu=True, bm=bm, bn=1024)
            cs = p["conv_dw"].shape[1] - 1
            new_conv.append(u.reshape(n_seq, seq_len, d)[:, -cs:])
            c = _conv_prompt(u, p["conv_dw"], p["conv_dw_b"], p["conv_ln_g"], p["conv_ln_b"], j,
                             bm=256, seq_len=seq_len)
            x, hn = _mm_resident(c, (w["out"],), glu=False, bm=512, bn=512, resid=x,
                                 gamma=p["norm_ffn"][i], hn_dtype=BF16)
        else:
            v, s_re, s_im = _s5_prompt(hn, p["s5_bmat"], p["s5_cmat"], p["s5_abar_re"][j],
                                       p["s5_abar_im"][j], p["ssm_D"][j].reshape(1, d), layer=j, rows=256,
                                       seq_len=seq_len)
            g, pdim = p["ssm_A_re"].shape[1:]
            new_re.append(s_re.reshape(n_seq, g, pdim))
            new_im.append(s_im.reshape(n_seq, g, pdim))
            x, hn = _mm_resident(v, (w["glu_a"], w["glu_b"]), glu=True, bm=512, bn=512, resid=x,
                        gamma=p["norm_ffn"][i], hn_dtype=BF16)
        act, tail = _ffn1_prompt(hn, w["gate"], w["up"], p["ffn_conv"], i, bm=seq_len, bn=512,
                                 seq_len=seq_len)
        fs = p["ffn_conv"].shape[1] - 1
        new_ffn.append(tail[:, -fs:])
        last = i == depth - 1
        gamma = p["norm_final"] if last else p["norm_mix"][i + 1]
        outs = _mm_resident(act, (w["down"],), glu=False, bm=256, bn=512, resid=x, gamma=gamma,
                   hn_dtype=F32 if last else _hn_dtype(i + 1), write_x=not last)
        if last:
            (y,) = outs
        else:
            x, hn = outs
    return (y, jnp.stack(new_conv), jnp.stack(new_re), jnp.stack(new_im), jnp.stack(new_ffn))


def kernel(x_prompt, x_sample, state_conv, state_ssm_re, state_ssm_im, state_ffn, norm_mix, norm_ffn, norm_final, conv_w_in, conv_dw, conv_dw_b, conv_ln_g, conv_ln_b, conv_w_out, ssm_A_re, ssm_A_im, ssm_log_dt, ssm_B_re, ssm_B_im, ssm_C_re, ssm_C_im, ssm_D, ssm_w_glu, ffn_w_gate, ffn_w_up, ffn_conv, ffn_w_down):
    bp, sp, d = x_prompt.shape
    abr, abi, bbr, bbi = _s5_discretise(ssm_A_re, ssm_A_im, ssm_log_dt, ssm_B_re, ssm_B_im)
    ns, g, pdim = ssm_A_re.shape
    bmat, cmat = _s5_block_weights(bbr, bbi, ssm_C_re, ssm_C_im)
    tiles_half = g * pdim // LANES
    p = dict(
        norm_mix=norm_mix, norm_ffn=norm_ffn, norm_final=norm_final,
        conv_w_in=conv_w_in, conv_dw=conv_dw, conv_dw_b=conv_dw_b,
        conv_ln_g=conv_ln_g, conv_ln_b=conv_ln_b, conv_w_out=conv_w_out,
        ssm_A_re=ssm_A_re, ssm_D=ssm_D, ssm_w_glu=ssm_w_glu,
        s5_bmat=bmat, s5_cmat=cmat,
        s5_abar_re=abr.reshape(ns, tiles_half, LANES), s5_abar_im=abi.reshape(ns, tiles_half, LANES),
        ffn_w_gate=ffn_w_gate, ffn_w_up=ffn_w_up, ffn_conv=ffn_conv, ffn_w_down=ffn_w_down,
    )
    tm = lambda a: jnp.swapaxes(a, -3, -2)
    (y_s, conv_s, re_s, im_s, ffn_s), wbf = _run_sample(
        tm(x_sample), (tm(state_conv), state_ssm_re, state_ssm_im, tm(state_ffn)), p)
    y_p, conv_p, re_p, im_p, ffn_p = _run_prompt(
        x_prompt.reshape(bp * sp, d), p, wbf, seq_len=sp, bm=1024)
    return (y_p.reshape(bp, sp, d), tm(y_s), conv_p, tm(conv_s), re_p, im_p, re_s, im_s, ffn_p,
            tm(ffn_s))
```

```python
import functools

import jax
import jax.numpy as jnp
from jax import lax
from jax.experimental import pallas as pl
from jax.experimental.pallas import tpu as pltpu

F32 = jnp.float32
BF16 = jnp.bfloat16

EPS = 1e-6
N_MIXERS = 2
GROUP_SIZE = 16
LANES = 128
SUBLANES = 8
VMEM_LIMIT = 56 * 1024 * 1024
ROW_CHUNK = 256
FFN_ROW_CHUNK = 1024


def _cparams(n_axes):
    return pltpu.CompilerParams(
        dimension_semantics=("arbitrary",) * n_axes, vmem_limit_bytes=VMEM_LIMIT)


def _rms_scale(ss, n):
    return lax.rsqrt(ss / n + EPS)


def _rmsnorm_body(x_ref, g_ref, o_ref):
    x = x_ref[...]
    ss = jnp.sum(x * x, axis=-1, keepdims=True)
    o_ref[...] = ((x * _rms_scale(ss, x.shape[-1])) * g_ref[...]).astype(o_ref.dtype)


def _rmsnorm(x, g, out_dtype, bm):
    m, d = x.shape
    return pl.pallas_call(
        _rmsnorm_body,
        out_shape=jax.ShapeDtypeStruct((m, d), out_dtype),
        grid=(m // bm,),
        in_specs=[pl.BlockSpec((bm, d), lambda i: (i, 0)),
                  pl.BlockSpec((1, d), lambda i: (0, 0))],
        out_specs=pl.BlockSpec((bm, d), lambda i: (i, 0)),
        compiler_params=_cparams(1),
        name="rmsnorm",
    )(x, g.reshape(1, d))


def _mm_body(*refs, glu, has_res, write_x, emit, prenorm, nn, bn, n_out):
    it = iter(refs)
    lhs_ref = next(it)
    pg_ref = next(it) if prenorm else None
    wa_ref = next(it)
    wb_ref = next(it) if glu else None
    res_ref = next(it) if has_res else None
    g_ref = next(it) if has_res else None
    xo_ref = next(it) if write_x else None
    hn_ref = next(it) if has_res else None
    wao_ref = next(it) if emit else None
    wbo_ref = next(it) if (emit and glu) else None
    rowbuf = next(it) if has_res else None
    ss_ref = next(it) if has_res else None
    hn_scr = next(it) if prenorm else None

    n = pl.program_id(1)
    bm = lhs_ref.shape[0]
    if prenorm:
        @pl.when(n == 0)
        def _():
            x = lhs_ref[...]
            ss = jnp.sum(x * x, axis=-1, keepdims=True)
            hn_scr[...] = ((x * _rms_scale(ss, x.shape[-1])) * pg_ref[...]).astype(BF16)
        lhs_ref = hn_scr
    if emit:
        wao_ref[...] = wa_ref[...].astype(BF16)
        wa_ref = wao_ref
        if glu:
            wbo_ref[...] = wb_ref[...].astype(BF16)
            wb_ref = wbo_ref
    if has_res:
        @pl.when(n == 0)
        def _():
            ss_ref[...] = jnp.zeros(ss_ref.shape, F32)
    for r0 in range(0, bm, ROW_CHUNK):
        rs = slice(r0, min(r0 + ROW_CHUNK, bm))
        x = lhs_ref[rs, :]
        val = jnp.dot(x, wa_ref[...], preferred_element_type=F32)
        if glu:
            gate = jnp.dot(x, wb_ref[...], preferred_element_type=F32)
            val = val * jax.nn.sigmoid(gate)
        if has_res:
            val = res_ref[rs, :] + val
        if write_x:
            xo_ref[rs, :] = val
        if has_res:
            rowbuf[n, rs, :] = val
            ss_ref[rs, :] = ss_ref[rs, :] + jnp.sum(val * val, axis=-1, keepdims=True)

    if has_res:
        @pl.when(n == nn - 1)
        def _():
            scale = _rms_scale(ss_ref[...], n_out)
            for j in range(nn):
                sl = slice(j * bn, (j + 1) * bn)
                hn_ref[:, sl] = ((rowbuf[j] * scale) * g_ref[:, sl]).astype(hn_ref.dtype)


def _mm(lhs, w, *, glu, bm, bn, layer=None, resid=None, gamma=None, hn_dtype=None, write_x=True,
        prenorm=None):
    m, k = lhs.shape
    emit = layer is not None
    if emit:
        assert m == bm
        n_out = w.shape[2] // (2 if glu else 1)
    else:
        n_out = w[0].shape[1]
    nn = n_out // bn
    has_res = resid is not None
    in_specs = [pl.BlockSpec((bm, k), lambda i, j: (i, 0))]
    args = [lhs]
    if prenorm is not None:
        in_specs.append(pl.BlockSpec((1, k), lambda i, j: (0, 0)))
        args.append(prenorm.reshape(1, k))
    if emit:
        in_specs.append(pl.BlockSpec((None, k, bn), lambda i, j: (layer, 0, j)))
        args.append(w)
        if glu:
            in_specs.append(pl.BlockSpec((None, k, bn), lambda i, j: (layer, 0, j + nn)))
            args.append(w)
    else:
        for wi in w:
            in_specs.append(pl.BlockSpec((k, bn), lambda i, j: (0, j)))
            args.append(wi)
    out_shape, out_specs, scratch = [], [], []
    if has_res:
        in_specs += [pl.BlockSpec((bm, bn), lambda i, j: (i, j)),
                     pl.BlockSpec((1, n_out), lambda i, j: (0, 0))]
        args += [resid, gamma.reshape(1, n_out)]
    if write_x:
        out_shape.append(jax.ShapeDtypeStruct((m, n_out), F32))
        out_specs.append(pl.BlockSpec((bm, bn), lambda i, j: (i, j)))
    if has_res:
        out_shape.append(jax.ShapeDtypeStruct((m, n_out), hn_dtype))
        out_specs.append(pl.BlockSpec((bm, n_out), lambda i, j: (i, 0)))
        scratch += [pltpu.VMEM((nn, bm, bn), F32), pltpu.VMEM((bm, 1), F32)]
    if emit:
        for _ in range(2 if glu else 1):
            out_shape.append(jax.ShapeDtypeStruct((k, n_out), BF16))
            out_specs.append(pl.BlockSpec((k, bn), lambda i, j: (0, j)))
    if prenorm is not None:
        scratch.append(pltpu.VMEM((bm, k), BF16))
    return pl.pallas_call(
        functools.partial(_mm_body, glu=glu, has_res=has_res, write_x=write_x, emit=emit,
                          prenorm=prenorm is not None, nn=nn, bn=bn, n_out=n_out),
        out_shape=out_shape,
        grid=(m // bm, nn),
        in_specs=in_specs,
        out_specs=out_specs,
        scratch_shapes=scratch,
        compiler_params=_cparams(2),
        name="mm_glu" if glu else "mm_lin",
    )(*args)


def _mmr_body(*refs, glu, write_x, bn, n_out):
    it = iter(refs)
    lhs_ref = next(it)
    wa_ref = next(it)
    wb_ref = next(it) if glu else None
    res_ref = next(it)
    g_ref = next(it)
    xo_ref = next(it) if write_x else None
    hn_ref = next(it)
    bm = lhs_ref.shape[0]
    rows = min(ROW_CHUNK, bm)
    for r0 in range(0, bm, rows):
        rs = slice(r0, r0 + rows)
        x = lhs_ref[rs, :]
        ss = jnp.zeros((rows, 1), F32)
        vals = []
        for j in range(n_out // bn):
            sl = slice(j * bn, (j + 1) * bn)
            val = jnp.dot(x, wa_ref[:, sl], preferred_element_type=F32)
            if glu:
                gate = jnp.dot(x, wb_ref[:, sl], preferred_element_type=F32)
                val = val * jax.nn.sigmoid(gate)
            val = res_ref[rs, sl] + val
            if write_x:
                xo_ref[rs, sl] = val
            ss = ss + jnp.sum(val * val, axis=-1, keepdims=True)
            vals.append(val)
        scale = _rms_scale(ss, n_out)
        for j, val in enumerate(vals):
            sl = slice(j * bn, (j + 1) * bn)
            hn_ref[rs, sl] = ((val * scale) * g_ref[:, sl]).astype(hn_ref.dtype)


def _mm_resident(lhs, w, *, glu, bm, bn, resid, gamma, hn_dtype, write_x=True):
    m, k = lhs.shape
    n_out = w[0].shape[1]
    row = lambda width: pl.BlockSpec((bm, width), lambda i: (i, 0))
    in_specs = [row(k)] + [pl.BlockSpec((k, n_out), lambda i: (0, 0)) for _ in w]
    in_specs += [row(n_out), pl.BlockSpec((1, n_out), lambda i: (0, 0))]
    out_shape, out_specs = [], []
    if write_x:
        out_shape.append(jax.ShapeDtypeStruct((m, n_out), F32))
        out_specs.append(row(n_out))
    out_shape.append(jax.ShapeDtypeStruct((m, n_out), hn_dtype))
    out_specs.append(row(n_out))
    return pl.pallas_call(
        functools.partial(_mmr_body, glu=glu, write_x=write_x, bn=bn, n_out=n_out),
        out_shape=out_shape,
        grid=(m // bm,),
        in_specs=in_specs,
        out_specs=out_specs,
        compiler_params=_cparams(1),
        name="mmr_glu" if glu else "mmr_lin",
    )(lhs, *w, resid, gamma.reshape(1, n_out))


def _shift_rows(g, tail, k):
    r = pltpu.roll(g, k, 0)
    t = pltpu.roll(tail, k, 0)
    i8 = lax.broadcasted_iota(jnp.int32, tail.shape, 0)
    head = jnp.where(i8 < k, t, r[0:SUBLANES, :])
    return jnp.concatenate([head, r[SUBLANES:, :]], axis=0)


def _ffn1_prompt_body(h_ref, wg_ref, wu_ref, cw_ref, act_ref, tail_ref, carry, *, bm, tiles_per_seq):
    m = pl.program_id(0)
    n = pl.program_id(1)
    cw = cw_ref[...]
    taps = cw.shape[0]

    @pl.when(m % tiles_per_seq == 0)
    def _():
        carry[n] = jnp.zeros(carry.shape[1:], F32)

    tail = carry[n]
    for r0 in range(0, bm, FFN_ROW_CHUNK):
        rs = slice(r0, r0 + FFN_ROW_CHUNK)
        h = h_ref[rs, :]
        g = jnp.dot(h, wg_ref[...], preferred_element_type=F32)
        up = jnp.dot(h, wu_ref[...], preferred_element_type=F32)
        gc = g * cw[taps - 1:taps, :]
        for k in range(taps - 1):
            gc = gc + _shift_rows(g, tail, taps - 1 - k) * cw[k:k + 1, :]
        act_ref[rs, :] = (jax.nn.silu(gc) * up).astype(act_ref.dtype)
        tail = g[FFN_ROW_CHUNK - SUBLANES:, :]
    carry[n] = tail
    tail_ref[...] = tail


def _ffn1_prompt(h, wg, wu, cw, layer, *, bm, bn, seq_len):
    m, k = h.shape
    f = wg.shape[1]
    nn = f // bn
    tiles_per_seq = seq_len // bm
    act, tails = pl.pallas_call(
        functools.partial(_ffn1_prompt_body, bm=bm, tiles_per_seq=tiles_per_seq),
        out_shape=[jax.ShapeDtypeStruct((m, f), BF16),
                   jax.ShapeDtypeStruct((m // bm, SUBLANES, f), F32)],
        grid=(m // bm, nn),
        in_specs=[pl.BlockSpec((bm, k), lambda i, j: (i, 0)),
                  pl.BlockSpec((k, bn), lambda i, j: (0, j)),
                  pl.BlockSpec((k, bn), lambda i, j: (0, j)),
                  pl.BlockSpec((None, cw.shape[1], bn), lambda i, j: (layer, 0, j))],
        out_specs=[pl.BlockSpec((bm, bn), lambda i, j: (i, j)),
                   pl.BlockSpec((None, SUBLANES, bn), lambda i, j: (i, 0, j))],
        scratch_shapes=[pltpu.VMEM((nn, SUBLANES, bn), F32)],
        compiler_params=_cparams(2),
        name="ffn_gate_up",
    )(h, wg, wu, cw)
    return act, tails[tiles_per_seq - 1::tiles_per_seq]


def _ffn1_sample_body(h_ref, wg_ref, wu_ref, cw_ref, c_ref, act_ref, st_ref, wgo_ref, wuo_ref, *,
                      nb, seq):
    wgo_ref[...] = wg_ref[...].astype(BF16)
    wuo_ref[...] = wu_ref[...].astype(BF16)
    h = h_ref[...]
    g = jnp.dot(h, wgo_ref[...], preferred_element_type=F32)
    up = jnp.dot(h, wuo_ref[...], preferred_element_type=F32)
    cw = cw_ref[...]
    taps = cw.shape[0]
    hist = [c_ref[r] for r in range(taps - 1)] + [g[t * nb:(t + 1) * nb, :] for t in range(seq)]
    for t in range(seq):
        gc = hist[t] * cw[0:1, :]
        for k in range(1, taps):
            gc = gc + hist[t + k] * cw[k:k + 1, :]
        act_ref[t * nb:(t + 1) * nb, :] = (
            jax.nn.silu(gc) * up[t * nb:(t + 1) * nb, :]).astype(act_ref.dtype)
    for r in range(taps - 1):
        st_ref[r] = hist[seq + r]


def _ffn1_sample(h, wg, wu, cw, cache, layer, *, bn, seq):
    m, k = h.shape
    f = wg.shape[2]
    nb = m // seq
    hist = cw.shape[1] - 1
    wspec = pl.BlockSpec((None, k, bn), lambda j: (layer, 0, j))
    wout = pl.BlockSpec((k, bn), lambda j: (0, j))
    return pl.pallas_call(
        functools.partial(_ffn1_sample_body, nb=nb, seq=seq),
        out_shape=[jax.ShapeDtypeStruct((m, f), BF16),
                   jax.ShapeDtypeStruct((hist, nb, f), F32),
                   jax.ShapeDtypeStruct((k, f), BF16),
                   jax.ShapeDtypeStruct((k, f), BF16)],
        grid=(f // bn,),
        in_specs=[pl.BlockSpec((m, k), lambda j: (0, 0)), wspec, wspec,
                  pl.BlockSpec((None, cw.shape[1], bn), lambda j: (layer, 0, j)),
                  pl.BlockSpec((None, hist, nb, bn), lambda j: (layer, 0, 0, j))],
        out_specs=[pl.BlockSpec((m, bn), lambda j: (0, j)),
                   pl.BlockSpec((hist, nb, bn), lambda j: (0, 0, j)), wout, wout],
        compiler_params=_cparams(1),
        name="ffn_gate_up_s",
    )(h, wg, wu, cw, cache)


CONV_ROWS = 32
HALO = 32


def _ln_silu(c, g, b):
    mu = jnp.mean(c, axis=-1, keepdims=True)
    d = c - mu
    var = jnp.mean(d * d, axis=-1, keepdims=True)
    y = d * lax.rsqrt(var + EPS)
    return jax.nn.silu(y * g + b)


def _conv_p_body(u_ref, dw_ref, db_ref, lg_ref, lb_ref, o_ref, ubuf, cbuf, *, bm, taps,
                 tiles_per_seq):
    m = pl.program_id(0)
    d = u_ref.shape[1]
    nc = d // LANES

    @pl.when(m % tiles_per_seq == 0)
    def _():
        ubuf[:, 0:HALO, :] = jnp.zeros((nc, HALO, LANES), F32)

    @pl.when(m % tiles_per_seq != 0)
    def _():
        ubuf[:, 0:HALO, :] = ubuf[:, bm:bm + HALO, :]

    for c in range(nc):
        ubuf[c, HALO:HALO + bm, :] = u_ref[:, c * LANES:(c + 1) * LANES]
    off = HALO - (taps - 1)

    def lane_chunk(c, carry):
        wv = dw_ref[c]
        bias = db_ref[c]
        for r0 in range(0, bm, CONV_ROWS):
            acc = jnp.zeros((CONV_ROWS, LANES), F32)
            for k in range(taps):
                acc = acc + ubuf[c, r0 + off + k:r0 + off + k + CONV_ROWS, :] * wv[k:k + 1, :]
            cbuf[c, r0:r0 + CONV_ROWS, :] = acc + bias
        return carry

    lax.fori_loop(0, nc, lane_chunk, 0)
    cfull = jnp.concatenate([cbuf[c] for c in range(nc)], axis=-1)
    o_ref[...] = _ln_silu(cfull, lg_ref[...], lb_ref[...]).astype(o_ref.dtype)


def _lane_chunked(x):
    l, r, d = x.shape
    return x.reshape(l, r, d // LANES, LANES).transpose(0, 2, 1, 3)


def _vec3(x):
    return x.reshape(x.shape[0], 1, x.shape[1])


def _conv_prompt(u, dw, db, lg, lb, layer, *, bm, seq_len):
    m, d = u.shape
    taps = dw.shape[1]
    nc = d // LANES
    vec = lambda: pl.BlockSpec((None, 1, d), lambda i: (layer, 0, 0))
    return pl.pallas_call(
        functools.partial(_conv_p_body, bm=bm, taps=taps, tiles_per_seq=seq_len // bm),
        out_shape=jax.ShapeDtypeStruct((m, d), BF16),
        grid=(m // bm,),
        in_specs=[pl.BlockSpec((bm, d), lambda i: (i, 0)),
                  pl.BlockSpec((None, nc, taps, LANES), lambda i: (layer, 0, 0, 0)),
                  pl.BlockSpec((None, nc, 1, LANES), lambda i: (layer, 0, 0, 0)),
                  vec(), vec()],
        out_specs=pl.BlockSpec((bm, d), lambda i: (i, 0)),
        scratch_shapes=[pltpu.VMEM((nc, HALO + bm, LANES), F32),
                        pltpu.VMEM((nc, bm, LANES), F32)],
        compiler_params=_cparams(1),
        name="conv_prompt",
    )(u, _lane_chunked(dw), _lane_chunked(_vec3(db)), _vec3(lg), _vec3(lb))


def _conv_s_body(cache_ref, u_ref, dw_ref, db_ref, lg_ref, lb_ref, o_ref, st_ref, cfull, *,
                 taps, seq, nsteps):
    c = pl.program_id(0)
    hist = taps - 1
    ext = lambda r: cache_ref[r] if r < hist else u_ref[r - hist]
    wv = dw_ref[...]
    bias = db_ref[...]
    for t in range(seq):
        acc = ext(t) * wv[0:1, :]
        for k in range(1, taps):
            acc = acc + ext(t + k) * wv[k:k + 1, :]
        cfull[c, t] = acc + bias
    for r in range(hist):
        st_ref[r] = ext(r + seq)

    @pl.when(c == nsteps - 1)
    def _():
        for t in range(seq):
            row = jnp.concatenate([cfull[cc, t] for cc in range(nsteps)], axis=-1)
            o_ref[t] = _ln_silu(row, lg_ref[...], lb_ref[...]).astype(o_ref.dtype)


def _conv_sample(cache, u, dw, db, lg, lb, layer, *, bc):
    seq, b, d = u.shape
    taps = dw.shape[1]
    hist = taps - 1
    nsteps = d // bc
    vec = lambda: pl.BlockSpec((None, 1, d), lambda i: (layer, 0, 0))
    return pl.pallas_call(
        functools.partial(_conv_s_body, taps=taps, seq=seq, nsteps=nsteps),
        out_shape=[jax.ShapeDtypeStruct((seq, b, d), BF16),
                   jax.ShapeDtypeStruct((hist, b, d), F32)],
        grid=(nsteps,),
        in_specs=[pl.BlockSpec((None, hist, b, bc), lambda i: (layer, 0, 0, i)),
                  pl.BlockSpec((seq, b, bc), lambda i: (0, 0, i)),
                  pl.BlockSpec((None, taps, bc), lambda i: (layer, 0, i)),
                  pl.BlockSpec((None, 1, bc), lambda i: (layer, 0, i)),
                  vec(), vec()],
        out_specs=[pl.BlockSpec((seq, b, d), lambda i: (0, 0, 0)),
                   pl.BlockSpec((hist, b, bc), lambda i: (0, 0, i))],
        scratch_shapes=[pltpu.VMEM((nsteps, seq, b, bc), F32)],
        compiler_params=_cparams(1),
        name="conv_sample",
    )(cache, u, dw, _vec3(db), _vec3(lg), _vec3(lb))


def _s5_abar(lam_re, lam_im, dt):
    mag = jnp.exp(lam_re * dt)
    ang = lam_im * dt
    return mag * jnp.cos(ang), mag * jnp.sin(ang)


def _s5_disc_body(are_ref, aim_ref, ldt_ref, arep_ref, airep_ref, bre_ref, bim_ref,
                  abr_ref, abi_ref, bbr_ref, bbi_ref):
    dt = jnp.exp(ldt_ref[...])
    abar_re, abar_im = _s5_abar(are_ref[...], aim_ref[...], dt)
    abr_ref[...] = abar_re
    abi_ref[...] = abar_im
    lam_re = arep_ref[...]
    lam_im = airep_ref[...]
    rep_re, rep_im = _s5_abar(lam_re, lam_im, dt)
    nr = rep_re - 1.0
    ni = rep_im
    den = lam_re * lam_re + lam_im * lam_im
    q_re = (nr * lam_re + ni * lam_im) / den
    q_im = (ni * lam_re - nr * lam_im) / den
    br = bre_ref[...]
    bi = bim_ref[...]
    bbr_ref[...] = q_re * br - q_im * bi
    bbi_ref[...] = q_re * bi + q_im * br


def _s5_discretise(a_re, a_im, log_dt, b_re, b_im):
    ns, g, p = a_re.shape
    rows = ns * g
    two = lambda x: x.reshape(rows, p)
    rep = lambda x: jnp.repeat(x.reshape(rows, p), GROUP_SIZE, axis=1)
    wide = lambda x: x.reshape(rows, p * GROUP_SIZE)
    outs = pl.pallas_call(
        _s5_disc_body,
        out_shape=[jax.ShapeDtypeStruct((rows, p), F32)] * 2
        + [jax.ShapeDtypeStruct((rows, p * GROUP_SIZE), F32)] * 2,
        name="s5_discretise",
    )(two(a_re), two(a_im), log_dt.reshape(rows, 1), rep(a_re), rep(a_im), wide(b_re), wide(b_im))
    abr, abi, bbr, bbi = outs
    return (abr.reshape(ns, g, p), abi.reshape(ns, g, p),
            bbr.reshape(ns, g, p, GROUP_SIZE), bbi.reshape(ns, g, p, GROUP_SIZE))


SLAB_GROUPS = 16
SLAB = SLAB_GROUPS * GROUP_SIZE


def _s5_expand_body(xbr_ref, xbi_ref, xcr_ref, xci_ref, bm_ref, cm_ref, *, p):
    gpt = LANES // p
    tiles_slab = SLAB_GROUPS // gpt
    rg = lax.broadcasted_iota(jnp.int32, (SLAB, LANES), 0) // GROUP_SIZE
    lg = lax.broadcasted_iota(jnp.int32, (SLAB, LANES), 1) // p
    for half, x_ref in enumerate((xbr_ref, xbi_ref)):
        x = x_ref[...]
        for a in range(tiles_slab):
            col = (half * tiles_slab + a) * LANES
            bm_ref[:, col:col + LANES] = jnp.where(rg == gpt * a + lg, x, 0.0).astype(BF16)
    cg = lax.broadcasted_iota(jnp.int32, (LANES, SLAB), 1) // GROUP_SIZE
    rc = lax.broadcasted_iota(jnp.int32, (LANES, SLAB), 0) // p
    for half, x_ref in enumerate((xcr_ref, xci_ref)):
        x = x_ref[...]
        for a in range(tiles_slab):
            row = (half * tiles_slab + a) * LANES
            cm_ref[row:row + LANES, :] = jnp.where(cg == gpt * a + rc, x, 0.0).astype(BF16)


def _s5_block_weights(bb_re, bb_im, c_re, c_im):
    ns, g, p, gs = bb_re.shape
    s = g // SLAB_GROUPS
    n = ns * s
    gpt = LANES // p
    bcomp = lambda x: jnp.tile(
        x.reshape(n, SLAB_GROUPS, p, gs).transpose(0, 1, 3, 2).reshape(n, SLAB, p), (1, 1, gpt))
    ccomp = lambda x: jnp.tile(
        x.reshape(n, SLAB_GROUPS, gs, p).transpose(0, 3, 1, 2).reshape(n, p, SLAB), (1, gpt, 1))
    width = 2 * SLAB_GROUPS * p
    bspec = pl.BlockSpec((None, SLAB, LANES), lambda i: (i, 0, 0))
    cspec = pl.BlockSpec((None, LANES, SLAB), lambda i: (i, 0, 0))
    bmat, cmat = pl.pallas_call(
        functools.partial(_s5_expand_body, p=p),
        out_shape=[jax.ShapeDtypeStruct((n, SLAB, width), BF16),
                   jax.ShapeDtypeStruct((n, width, SLAB), BF16)],
        grid=(n,),
        in_specs=[bspec, bspec, cspec, cspec],
        out_specs=[pl.BlockSpec((None, SLAB, width), lambda i: (i, 0, 0)),
                   pl.BlockSpec((None, width, SLAB), lambda i: (i, 0, 0))],
        compiler_params=_cparams(1),
        name="s5_expand",
    )(bcomp(bb_re), bcomp(bb_im), ccomp(c_re), ccomp(-c_im))
    return bmat.reshape(ns, s, SLAB, width), cmat.reshape(ns, s, width, SLAB)


def _s5_pitch(rows):
    return rows + SUBLANES // 2


def _s5_project_in(u, bmat_ref, z, *, rows, pitch, n_slab, tiles_half):
    tiles_slab = tiles_half // n_slab
    for s in range(n_slab):
        bu = jnp.dot(u[:, s * SLAB:(s + 1) * SLAB].astype(BF16), bmat_ref[s],
                     preferred_element_type=F32)
        for half in range(2):
            for q in range(tiles_slab):
                j = half * tiles_half + s * tiles_slab + q
                col = (half * tiles_slab + q) * LANES
                z[j * pitch:j * pitch + rows, :] = bu[:, col:col + LANES]


def _s5_project_out(u, z, cmat_ref, d_ref, store, *, rows, pitch, n_slab, tiles_half):
    tiles_slab = tiles_half // n_slab
    for s in range(n_slab):
        parts = []
        for half in range(2):
            for q in range(tiles_slab):
                j = half * tiles_half + s * tiles_slab + q
                parts.append(z[j * pitch:j * pitch + rows, :].astype(BF16))
        st = jnp.concatenate(parts, axis=-1)
        y = jnp.dot(st, cmat_ref[s], preferred_element_type=F32)
        sl = slice(s * SLAB, (s + 1) * SLAB)
        y = y + d_ref[:, sl] * u[:, sl]
        store(sl, jax.nn.gelu(y))


def _s5_step(z, t, state, abar, *, pitch, tiles_half):
    nv = tiles_half // SUBLANES
    new = []
    for k in range(nv):
        idx_re = pl.ds(k * SUBLANES * pitch + t, SUBLANES, stride=pitch)
        idx_im = pl.ds((tiles_half + k * SUBLANES) * pitch + t, SUBLANES, stride=pitch)
        ar, ai = abar[k]
        sr, si = state[k]
        nr = ar * sr - ai * si + z[idx_re, :]
        ni = ar * si + ai * sr + z[idx_im, :]
        z[idx_re, :] = nr
        z[idx_im, :] = ni
        new.append((nr, ni))
    return tuple(new)


def _s5_p_body(u_ref, bmat_ref, cmat_ref, abr_ref, abi_ref, d_ref, o_ref, sre_ref, sim_ref,
               z, st_re, st_im, *, rows, tiles_per_seq, n_slab, tiles_half):
    m = pl.program_id(0)
    pitch = _s5_pitch(rows)
    nv = tiles_half // SUBLANES
    u = u_ref[...]
    _s5_project_in(u, bmat_ref, z, rows=rows, pitch=pitch, n_slab=n_slab, tiles_half=tiles_half)

    @pl.when(m % tiles_per_seq == 0)
    def _():
        st_re[...] = jnp.zeros(st_re.shape, F32)
        st_im[...] = jnp.zeros(st_im.shape, F32)

    vsl = lambda k: slice(k * SUBLANES, (k + 1) * SUBLANES)
    abar = tuple((abr_ref[vsl(k), :], abi_ref[vsl(k), :]) for k in range(nv))
    state0 = tuple((st_re[vsl(k), :], st_im[vsl(k), :]) for k in range(nv))

    def step(t, state):
        return _s5_step(z, t, state, abar, pitch=pitch, tiles_half=tiles_half)

    state = lax.fori_loop(0, rows, step, state0, unroll=8)
    for k in range(nv):
        st_re[vsl(k), :] = state[k][0]
        st_im[vsl(k), :] = state[k][1]
    sre_ref[...] = st_re[...]
    sim_ref[...] = st_im[...]

    def store(sl, val):
        o_ref[:, sl] = val.astype(o_ref.dtype)

    _s5_project_out(u, z, cmat_ref, d_ref, store, rows=rows, pitch=pitch, n_slab=n_slab,
                    tiles_half=tiles_half)


def _s5_prompt(u, bmat, cmat, abar_re, abar_im, dskip, *, layer, rows, seq_len):
    m, d = u.shape
    n_slab = bmat.shape[1]
    wsel = lambda a: pl.BlockSpec((None,) + a.shape[1:], lambda i: (layer, 0, 0, 0))
    tiles_half = abar_re.shape[0]
    pitch = _s5_pitch(rows)
    tiles_per_seq = seq_len // rows
    n_seq = m // seq_len
    st_spec = pl.BlockSpec((None, tiles_half, LANES), lambda i: (i // tiles_per_seq, 0, 0))
    full = lambda a: pl.BlockSpec(a.shape, lambda i: (0,) * a.ndim)
    return pl.pallas_call(
        functools.partial(_s5_p_body, rows=rows, tiles_per_seq=tiles_per_seq, n_slab=n_slab,
                          tiles_half=tiles_half),
        out_shape=[jax.ShapeDtypeStruct((m, d), BF16),
                   jax.ShapeDtypeStruct((n_seq, tiles_half, LANES), F32),
                   jax.ShapeDtypeStruct((n_seq, tiles_half, LANES), F32)],
        grid=(m // rows,),
        in_specs=[pl.BlockSpec((rows, d), lambda i: (i, 0)),
                  wsel(bmat), wsel(cmat), full(abar_re), full(abar_im), full(dskip)],
        out_specs=[pl.BlockSpec((rows, d), lambda i: (i, 0)), st_spec, st_spec],
        scratch_shapes=[pltpu.VMEM((2 * tiles_half * pitch, LANES), F32),
                        pltpu.VMEM((tiles_half, LANES), F32),
                        pltpu.VMEM((tiles_half, LANES), F32)],
        compiler_params=_cparams(1),
        name="s5_prompt",
    )(u, bmat, cmat, abar_re, abar_im, dskip)


def _s5_s_body(u_ref, h0r_ref, h0i_ref, bmat_ref, cmat_ref, abr_ref, abi_ref, d_ref,
               o_ref, sre_ref, sim_ref, z, zh, *, nb, seq, n_slab, tiles_half):
    rows = nb * seq
    pitch = _s5_pitch(rows)
    hp = _s5_pitch(nb)
    nv = tiles_half // SUBLANES
    d = u_ref.shape[2]
    u = u_ref[...].reshape(rows, d)
    _s5_project_in(u, bmat_ref, z, rows=rows, pitch=pitch, n_slab=n_slab, tiles_half=tiles_half)
    for j in range(tiles_half):
        cs = slice(j * LANES, (j + 1) * LANES)
        zh[j * hp:j * hp + nb, :] = h0r_ref[:, cs]
        zh[(tiles_half + j) * hp:(tiles_half + j) * hp + nb, :] = h0i_ref[:, cs]

    vsl = lambda k: slice(k * SUBLANES, (k + 1) * SUBLANES)
    abar = tuple((abr_ref[vsl(k), :], abi_ref[vsl(k), :]) for k in range(nv))

    def one_seq(b, carry):
        state = []
        for k in range(nv):
            idx_re = pl.ds(k * SUBLANES * hp + b, SUBLANES, stride=hp)
            idx_im = pl.ds((tiles_half + k * SUBLANES) * hp + b, SUBLANES, stride=hp)
            state.append((zh[idx_re, :], zh[idx_im, :]))
        state = tuple(state)
        for t in range(seq):
            state = _s5_step(z, t * nb + b, state, abar, pitch=pitch, tiles_half=tiles_half)
        for k in range(nv):
            idx_re = pl.ds(k * SUBLANES * hp + b, SUBLANES, stride=hp)
            idx_im = pl.ds((tiles_half + k * SUBLANES) * hp + b, SUBLANES, stride=hp)
            zh[idx_re, :] = state[k][0]
            zh[idx_im, :] = state[k][1]
        return carry

    lax.fori_loop(0, nb, one_seq, 0)
    for j in range(tiles_half):
        cs = slice(j * LANES, (j + 1) * LANES)
        sre_ref[:, cs] = zh[j * hp:j * hp + nb, :]
        sim_ref[:, cs] = zh[(tiles_half + j) * hp:(tiles_half + j) * hp + nb, :]

    def store(sl, val):
        o_ref[:, :, sl] = val.reshape(seq, nb, val.shape[1]).astype(o_ref.dtype)

    _s5_project_out(u, z, cmat_ref, d_ref, store, rows=rows, pitch=pitch, n_slab=n_slab,
                    tiles_half=tiles_half)


def _s5_sample(u, h0_re, h0_im, bmat, cmat, abar_re, abar_im, dskip, *, layer, nb):
    seq, b, d = u.shape
    nstate = h0_re.shape[1]
    n_slab = bmat.shape[1]
    wsel = lambda a: pl.BlockSpec((None,) + a.shape[1:], lambda i: (layer, 0, 0, 0))
    tiles_half = abar_re.shape[0]
    pitch = _s5_pitch(nb * seq)
    hp = _s5_pitch(nb)
    full = lambda a: pl.BlockSpec(a.shape, lambda i: (0,) * a.ndim)
    st_spec = pl.BlockSpec((nb, nstate), lambda i: (i, 0))
    u_spec = pl.BlockSpec((seq, nb, d), lambda i: (0, i, 0))
    return pl.pallas_call(
        functools.partial(_s5_s_body, nb=nb, seq=seq, n_slab=n_slab, tiles_half=tiles_half),
        out_shape=[jax.ShapeDtypeStruct((seq, b, d), BF16),
                   jax.ShapeDtypeStruct(h0_re.shape, F32),
                   jax.ShapeDtypeStruct(h0_im.shape, F32)],
        grid=(b // nb,),
        in_specs=[u_spec, st_spec, st_spec,
                  wsel(bmat), wsel(cmat), full(abar_re), full(abar_im), full(dskip)],
        out_specs=[u_spec, st_spec, st_spec],
        scratch_shapes=[pltpu.VMEM((2 * tiles_half * pitch, LANES), F32),
                        pltpu.VMEM((2 * tiles_half * hp, LANES), F32)],
        compiler_params=_cparams(1),
        name="s5_sample",
    )(u, h0_re, h0_im, bmat, cmat, abar_re, abar_im, dskip)


def _hn_dtype(i):
    return F32 if (i % N_MIXERS == 1) else BF16


def _run_sample(x, states, p):
    seq, nb, d = x.shape
    m = seq * nb
    depth = p["norm_mix"].shape[0]
    conv_st, ssm_re_st, ssm_im_st, ffn_st = states
    new_conv, new_re, new_im, new_ffn, wbf = [], [], [], [], []
    x = x.reshape(m, d)
    hn = _rmsnorm(x, p["norm_mix"][0], _hn_dtype(0), m)
    y = None
    for i in range(depth):
        j = i // N_MIXERS
        w = {}
        if i % N_MIXERS == 0:
            u, w["in_a"], w["in_b"] = _mm(hn, p["conv_w_in"], glu=True, bm=m, bn=512, layer=j)
            c, st = _conv_sample(conv_st, u.reshape(seq, nb, d), p["conv_dw"], p["conv_dw_b"],
                                 p["conv_ln_g"], p["conv_ln_b"], j, bc=256)
            new_conv.append(st)
            x, hn, w["out"] = _mm(c.reshape(m, d), p["conv_w_out"], glu=False, bm=m, bn=512,
                                  layer=j, resid=x, gamma=p["norm_ffn"][i], hn_dtype=BF16)
        else:
            v, s_re, s_im = _s5_sample(hn.reshape(seq, nb, d), ssm_re_st[j].reshape(nb, -1),
                                       ssm_im_st[j].reshape(nb, -1), p["s5_bmat"],
                                       p["s5_cmat"], p["s5_abar_re"][j], p["s5_abar_im"][j],
                                       p["ssm_D"][j].reshape(1, d), layer=j, nb=32)
            new_re.append(s_re.reshape(ssm_re_st.shape[1:]))
            new_im.append(s_im.reshape(ssm_im_st.shape[1:]))
            x, hn, w["glu_a"], w["glu_b"] = _mm(
                v.reshape(m, d), p["ssm_w_glu"], glu=True, bm=m, bn=512, layer=j, resid=x,
                gamma=p["norm_ffn"][i], hn_dtype=BF16)
        act, st, w["gate"], w["up"] = _ffn1_sample(hn, p["ffn_w_gate"], p["ffn_w_up"],
                                                   p["ffn_conv"], ffn_st, i, bn=512, seq=seq)
        new_ffn.append(st)
        last = i == depth - 1
        gamma = p["norm_final"] if last else p["norm_mix"][i + 1]
        outs = _mm(act, p["ffn_w_down"], glu=False, bm=m, bn=256, layer=i, resid=x, gamma=gamma,
                   hn_dtype=F32 if last else _hn_dtype(i + 1), write_x=not last)
        if last:
            y, w["down"] = outs
        else:
            x, hn, w["down"] = outs
        wbf.append(w)
    return (y.reshape(seq, nb, d), jnp.stack(new_conv), jnp.stack(new_re), jnp.stack(new_im),
            jnp.stack(new_ffn)), wbf


def _run_prompt(x, p, wbf, *, seq_len, bm):
    m, d = x.shape
    n_seq = m // seq_len
    depth = p["norm_mix"].shape[0]
    new_conv, new_re, new_im, new_ffn = [], [], [], []
    hn = None
    y = None
    for i in range(depth):
        j = i // N_MIXERS
        w = wbf[i]
        if i % N_MIXERS == 0:
            if i == 0:
                (u,) = _mm(x, (w["in_a"], w["in_b"]), glu=True, bm=bm, bn=1024,
                           prenorm=p["norm_mix"][0])
            else:
                (u,) = _mm(hn, (w["in_a"], w["in_b"]), glu=True, bm=bm, bn=1024)
            cs = p["conv_dw"].shape[1] - 1
            new_conv.append(u.reshape(n_seq, seq_len, d)[:, -cs:])
            c = _conv_prompt(u, p["conv_dw"], p["conv_dw_b"], p["conv_ln_g"], p["conv_ln_b"], j,
                             bm=256, seq_len=seq_len)
            x, hn = _mm_resident(c, (w["out"],), glu=False, bm=512, bn=512, resid=x,
                                 gamma=p["norm_ffn"][i], hn_dtype=BF16)
        else:
            v, s_re, s_im = _s5_prompt(hn, p["s5_bmat"], p["s5_cmat"], p["s5_abar_re"][j],
                                       p["s5_abar_im"][j], p["ssm_D"][j].reshape(1, d), layer=j, rows=256,
                                       seq_len=seq_len)
            g, pdim = p["ssm_A_re"].shape[1:]
            new_re.append(s_re.reshape(n_seq, g, pdim))
            new_im.append(s_im.reshape(n_seq, g, pdim))
            x, hn = _mm_resident(v, (w["glu_a"], w["glu_b"]), glu=True, bm=512, bn=512, resid=x,
                        gamma=p["norm_ffn"][i], hn_dtype=BF16)
        act, tail = _ffn1_prompt(hn, w["gate"], w["up"], p["ffn_conv"], i, bm=seq_len, bn=512,
                                 seq_len=seq_len)
        fs = p["ffn_conv"].shape[1] - 1
        new_ffn.append(tail[:, -fs:])
        last = i == depth - 1
        gamma = p["norm_final"] if last else p["norm_mix"][i + 1]
        outs = _mm_resident(act, (w["down"],), glu=False, bm=256, bn=512, resid=x, gamma=gamma,
                   hn_dtype=F32 if last else _hn_dtype(i + 1), write_x=not last)
        if last:
            (y,) = outs
        else:
            x, hn = outs
    return (y, jnp.stack(new_conv), jnp.stack(new_re), jnp.stack(new_im), jnp.stack(new_ffn))


def kernel(x_prompt, x_sample, state_conv, state_ssm_re, state_ssm_im, state_ffn, norm_mix, norm_ffn, norm_final, conv_w_in, conv_dw, conv_dw_b, conv_ln_g, conv_ln_b, conv_w_out, ssm_A_re, ssm_A_im, ssm_log_dt, ssm_B_re, ssm_B_im, ssm_C_re, ssm_C_im, ssm_D, ssm_w_glu, ffn_w_gate, ffn_w_up, ffn_conv, ffn_w_down):
    bp, sp, d = x_prompt.shape
    abr, abi, bbr, bbi = _s5_discretise(ssm_A_re, ssm_A_im, ssm_log_dt, ssm_B_re, ssm_B_im)
    ns, g, pdim = ssm_A_re.shape
    bmat, cmat = _s5_block_weights(bbr, bbi, ssm_C_re, ssm_C_im)
    tiles_half = g * pdim // LANES
    p = dict(
        norm_mix=norm_mix, norm_ffn=norm_ffn, norm_final=norm_final,
        conv_w_in=conv_w_in, conv_dw=conv_dw, conv_dw_b=conv_dw_b,
        conv_ln_g=conv_ln_g, conv_ln_b=conv_ln_b, conv_w_out=conv_w_out,
        ssm_A_re=ssm_A_re, ssm_D=ssm_D, ssm_w_glu=ssm_w_glu,
        s5_bmat=bmat, s5_cmat=cmat,
        s5_abar_re=abr.reshape(ns, tiles_half, LANES), s5_abar_im=abi.reshape(ns, tiles_half, LANES),
        ffn_w_gate=ffn_w_gate, ffn_w_up=ffn_w_up, ffn_conv=ffn_conv, ffn_w_down=ffn_w_down,
    )
    tm = lambda a: jnp.swapaxes(a, -3, -2)
    (y_s, conv_s, re_s, im_s, ffn_s), wbf = _run_sample(
        tm(x_sample), (tm(state_conv), state_ssm_re, state_ssm_im, tm(state_ffn)), p)
    y_p, conv_p, re_p, im_p, ffn_p = _run_prompt(
        x_prompt.reshape(bp * sp, d), p, wbf, seq_len=sp, bm=1024)
    return (y_p.reshape(bp, sp, d), tm(y_s), conv_p, tm(conv_s), re_p, im_p, re_s, im_s, ffn_p,
            tm(ffn_s))
```

```python
import functools

import jax
import jax.numpy as jnp
from jax import lax
from jax.experimental import pallas as pl
from jax.experimental.pallas import tpu as pltpu

F32 = jnp.float32
BF16 = jnp.bfloat16

EPS = 1e-6
N_MIXERS = 2
GROUP_SIZE = 16
LANES = 128
SUBLANES = 8
VMEM_LIMIT = 56 * 1024 * 1024
ROW_CHUNK = 256
FFN_ROW_CHUNK = 1024


def _cparams(n_axes):
    return pltpu.CompilerParams(
        dimension_semantics=("arbitrary",) * n_axes, vmem_limit_bytes=VMEM_LIMIT)


def _rms_scale(ss, n):
    return lax.rsqrt(ss / n + EPS)


def _rmsnorm_body(x_ref, g_ref, o_ref):
    x = x_ref[...]
    ss = jnp.sum(x * x, axis=-1, keepdims=True)
    o_ref[...] = ((x * _rms_scale(ss, x.shape[-1])) * g_ref[...]).astype(o_ref.dtype)


def _rmsnorm(x, g, out_dtype, bm):
    m, d = x.shape
    return pl.pallas_call(
        _rmsnorm_body,
        out_shape=jax.ShapeDtypeStruct((m, d), out_dtype),
        grid=(m // bm,),
        in_specs=[pl.BlockSpec((bm, d), lambda i: (i, 0)),
                  pl.BlockSpec((1, d), lambda i: (0, 0))],
        out_specs=pl.BlockSpec((bm, d), lambda i: (i, 0)),
        compiler_params=_cparams(1),
        name="rmsnorm",
    )(x, g.reshape(1, d))


def _mm_body(*refs, glu, has_res, write_x, emit, prenorm, nn, bn, n_out):
    it = iter(refs)
    lhs_ref = next(it)
    pg_ref = next(it) if prenorm else None
    wa_ref = next(it)
    wb_ref = next(it) if glu else None
    res_ref = next(it) if has_res else None
    g_ref = next(it) if has_res else None
    xo_ref = next(it) if write_x else None
    hn_ref = next(it) if has_res else None
    wao_ref = next(it) if emit else None
    wbo_ref = next(it) if (emit and glu) else None
    rowbuf = next(it) if has_res else None
    ss_ref = next(it) if has_res else None
    hn_scr = next(it) if prenorm else None

    n = pl.program_id(1)
    bm = lhs_ref.shape[0]
    if prenorm:
        @pl.when(n == 0)
        def _():
            x = lhs_ref[...]
            ss = jnp.sum(x * x, axis=-1, keepdims=True)
            hn_scr[...] = ((x * _rms_scale(ss, x.shape[-1])) * pg_ref[...]).astype(BF16)
        lhs_ref = hn_scr
    if emit:
        wao_ref[...] = wa_ref[...].astype(BF16)
        wa_ref = wao_ref
        if glu:
            wbo_ref[...] = wb_ref[...].astype(BF16)
            wb_ref = wbo_ref
    if has_res:
        @pl.when(n == 0)
        def _():
            ss_ref[...] = jnp.zeros(ss_ref.shape, F32)
    for r0 in range(0, bm, ROW_CHUNK):
        rs = slice(r0, min(r0 + ROW_CHUNK, bm))
        x = lhs_ref[rs, :]
        val = jnp.dot(x, wa_ref[...], preferred_element_type=F32)
        if glu:
            gate = jnp.dot(x, wb_ref[...], preferred_element_type=F32)
            val = val * jax.nn.sigmoid(gate)
        if has_res:
            val = res_ref[rs, :] + val
        if write_x:
            xo_ref[rs, :] = val
        if has_res:
            rowbuf[n, rs, :] = val
            ss_ref[rs, :] = ss_ref[rs, :] + jnp.sum(val * val, axis=-1, keepdims=True)

    if has_res:
        @pl.when(n == nn - 1)
        def _():
            scale = _rms_scale(ss_ref[...], n_out)
            for j in range(nn):
                sl = slice(j * bn, (j + 1) * bn)
                hn_ref[:, sl] = ((rowbuf[j] * scale) * g_ref[:, sl]).astype(hn_ref.dtype)


def _mm(lhs, w, *, glu, bm, bn, layer=None, resid=None, gamma=None, hn_dtype=None, write_x=True,
        prenorm=None):
    m, k = lhs.shape
    emit = layer is not None
    if emit:
        assert m == bm
        n_out = w.shape[2] // (2 if glu else 1)
    else:
        n_out = w[0].shape[1]
    nn = n_out // bn
    has_res = resid is not None
    in_specs = [pl.BlockSpec((bm, k), lambda i, j: (i, 0))]
    args = [lhs]
    if prenorm is not None:
        in_specs.append(pl.BlockSpec((1, k), lambda i, j: (0, 0)))
        args.append(prenorm.reshape(1, k))
    if emit:
        in_specs.append(pl.BlockSpec((None, k, bn), lambda i, j: (layer, 0, j)))
        args.append(w)
        if glu:
            in_specs.append(pl.BlockSpec((None, k, bn), lambda i, j: (layer, 0, j + nn)))
            args.append(w)
    else:
        for wi in w:
            in_specs.append(pl.BlockSpec((k, bn), lambda i, j: (0, j)))
            args.append(wi)
    out_shape, out_specs, scratch = [], [], []
    if has_res:
        in_specs += [pl.BlockSpec((bm, bn), lambda i, j: (i, j)),
                     pl.BlockSpec((1, n_out), lambda i, j: (0, 0))]
        args += [resid, gamma.reshape(1, n_out)]
    if write_x:
        out_shape.append(jax.ShapeDtypeStruct((m, n_out), F32))
        out_specs.append(pl.BlockSpec((bm, bn), lambda i, j: (i, j)))
    if has_res:
        out_shape.append(jax.ShapeDtypeStruct((m, n_out), hn_dtype))
        out_specs.append(pl.BlockSpec((bm, n_out), lambda i, j: (i, 0)))
        scratch += [pltpu.VMEM((nn, bm, bn), F32), pltpu.VMEM((bm, 1), F32)]
    if emit:
        for _ in range(2 if glu else 1):
            out_shape.append(jax.ShapeDtypeStruct((k, n_out), BF16))
            out_specs.append(pl.BlockSpec((k, bn), lambda i, j: (0, j)))
    if prenorm is not None:
        scratch.append(pltpu.VMEM((bm, k), BF16))
    return pl.pallas_call(
        functools.partial(_mm_body, glu=glu, has_res=has_res, write_x=write_x, emit=emit,
                          prenorm=prenorm is not None, nn=nn, bn=bn, n_out=n_out),
        out_shape=out_shape,
        grid=(m // bm, nn),
        in_specs=in_specs,
        out_specs=out_specs,
        scratch_shapes=scratch,
        compiler_params=_cparams(2),
        name="mm_glu" if glu else "mm_lin",
    )(*args)


def _mmr_body(*refs, glu, write_x, bn, n_out):
    it = iter(refs)
    lhs_ref = next(it)
    wa_ref = next(it)
    wb_ref = next(it) if glu else None
    res_ref = next(it)
    g_ref = next(it)
    xo_ref = next(it) if write_x else None
    hn_ref = next(it)
    bm = lhs_ref.shape[0]
    rows = min(ROW_CHUNK, bm)
    for r0 in range(0, bm, rows):
        rs = slice(r0, r0 + rows)
        x = lhs_ref[rs, :]
        ss = jnp.zeros((rows, 1), F32)
        vals = []
        for j in range(n_out // bn):
            sl = slice(j * bn, (j + 1) * bn)
            val = jnp.dot(x, wa_ref[:, sl], preferred_element_type=F32)
            if glu:
                gate = jnp.dot(x, wb_ref[:, sl], preferred_element_type=F32)
                val = val * jax.nn.sigmoid(gate)
            val = res_ref[rs, sl] + val
            if write_x:
                xo_ref[rs, sl] = val
            ss = ss + jnp.sum(val * val, axis=-1, keepdims=True)
            vals.append(val)
        scale = _rms_scale(ss, n_out)
        for j, val in enumerate(vals):
            sl = slice(j * bn, (j + 1) * bn)
            hn_ref[rs, sl] = ((val * scale) * g_ref[:, sl]).astype(hn_ref.dtype)


def _mm_resident(lhs, w, *, glu, bm, bn, resid, gamma, hn_dtype, write_x=True):
    m, k = lhs.shape
    n_out = w[0].shape[1]
    row = lambda width: pl.BlockSpec((bm, width), lambda i: (i, 0))
    in_specs = [row(k)] + [pl.BlockSpec((k, n_out), lambda i: (0, 0)) for _ in w]
    in_specs += [row(n_out), pl.BlockSpec((1, n_out), lambda i: (0, 0))]
    out_shape, out_specs = [], []
    if write_x:
        out_shape.append(jax.ShapeDtypeStruct((m, n_out), F32))
        out_specs.append(row(n_out))
    out_shape.append(jax.ShapeDtypeStruct((m, n_out), hn_dtype))
    out_specs.append(row(n_out))
    return pl.pallas_call(
        functools.partial(_mmr_body, glu=glu, write_x=write_x, bn=bn, n_out=n_out),
        out_shape=out_shape,
        grid=(m // bm,),
        in_specs=in_specs,
        out_specs=out_specs,
        compiler_params=_cparams(1),
        name="mmr_glu" if glu else "mmr_lin",
    )(lhs, *w, resid, gamma.reshape(1, n_out))


def _shift_rows(g, tail, k):
    r = pltpu.roll(g, k, 0)
    t = pltpu.roll(tail, k, 0)
    i8 = lax.broadcasted_iota(jnp.int32, tail.shape, 0)
    head = jnp.where(i8 < k, t, r[0:SUBLANES, :])
    return jnp.concatenate([head, r[SUBLANES:, :]], axis=0)


def _ffn1_prompt_body(h_ref, wg_ref, wu_ref, cw_ref, act_ref, tail_ref, carry, *, bm, tiles_per_seq):
    m = pl.program_id(0)
    n = pl.program_id(1)
    cw = cw_ref[...]
    taps = cw.shape[0]

    @pl.when(m % tiles_per_seq == 0)
    def _():
        carry[n] = jnp.zeros(carry.shape[1:], F32)

    tail = carry[n]
    for r0 in range(0, bm, FFN_ROW_CHUNK):
        rs = slice(r0, r0 + FFN_ROW_CHUNK)
        h = h_ref[rs, :]
        g = jnp.dot(h, wg_ref[...], preferred_element_type=F32)
        up = jnp.dot(h, wu_ref[...], preferred_element_type=F32)
        gc = g * cw[taps - 1:taps, :]
        for k in range(taps - 1):
            gc = gc + _shift_rows(g, tail, taps - 1 - k) * cw[k:k + 1, :]
        act_ref[rs, :] = (jax.nn.silu(gc) * up).astype(act_ref.dtype)
        tail = g[FFN_ROW_CHUNK - SUBLANES:, :]
    carry[n] = tail
    tail_ref[...] = tail


def _ffn1_prompt(h, wg, wu, cw, layer, *, bm, bn, seq_len):
    m, k = h.shape
    f = wg.shape[1]
    nn = f // bn
    tiles_per_seq = seq_len // bm
    act, tails = pl.pallas_call(
        functools.partial(_ffn1_prompt_body, bm=bm, tiles_per_seq=tiles_per_seq),
        out_shape=[jax.ShapeDtypeStruct((m, f), BF16),
                   jax.ShapeDtypeStruct((m // bm, SUBLANES, f), F32)],
        grid=(m // bm, nn),
        in_specs=[pl.BlockSpec((bm, k), lambda i, j: (i, 0)),
                  pl.BlockSpec((k, bn), lambda i, j: (0, j)),
                  pl.BlockSpec((k, bn), lambda i, j: (0, j)),
                  pl.BlockSpec((None, cw.shape[1], bn), lambda i, j: (layer, 0, j))],
        out_specs=[pl.BlockSpec((bm, bn), lambda i, j: (i, j)),
                   pl.BlockSpec((None, SUBLANES, bn), lambda i, j: (i, 0, j))],
        scratch_shapes=[pltpu.VMEM((nn, SUBLANES, bn), F32)],
        compiler_params=_cparams(2),
        name="ffn_gate_up",
    )(h, wg, wu, cw)
    return act, tails[tiles_per_seq - 1::tiles_per_seq]


def _ffn1_sample_body(h_ref, wg_ref, wu_ref, cw_ref, c_ref, act_ref, st_ref, wgo_ref, wuo_ref, *,
                      nb, seq):
    wgo_ref[...] = wg_ref[...].astype(BF16)
    wuo_ref[...] = wu_ref[...].astype(BF16)
    h = h_ref[...]
    g = jnp.dot(h, wgo_ref[...], preferred_element_type=F32)
    up = jnp.dot(h, wuo_ref[...], preferred_element_type=F32)
    cw = cw_ref[...]
    taps = cw.shape[0]
    hist = [c_ref[r] for r in range(taps - 1)] + [g[t * nb:(t + 1) * nb, :] for t in range(seq)]
    for t in range(seq):
        gc = hist[t] * cw[0:1, :]
        for k in range(1, taps):
            gc = gc + hist[t + k] * cw[k:k + 1, :]
        act_ref[t * nb:(t + 1) * nb, :] = (
            jax.nn.silu(gc) * up[t * nb:(t + 1) * nb, :]).astype(act_ref.dtype)
    for r in range(taps - 1):
        st_ref[r] = hist[seq + r]


def _ffn1_sample(h, wg, wu, cw, cache, layer, *, bn, seq):
    m, k = h.shape
    f = wg.shape[2]
    nb = m // seq
    hist = cw.shape[1] - 1
    wspec = pl.BlockSpec((None, k, bn), lambda j: (layer, 0, j))
    wout = pl.BlockSpec((k, bn), lambda j: (0, j))
    return pl.pallas_call(
        functools.partial(_ffn1_sample_body, nb=nb, seq=seq),
        out_shape=[jax.ShapeDtypeStruct((m, f), BF16),
                   jax.ShapeDtypeStruct((hist, nb, f), F32),
                   jax.ShapeDtypeStruct((k, f), BF16),
                   jax.ShapeDtypeStruct((k, f), BF16)],
        grid=(f // bn,),
        in_specs=[pl.BlockSpec((m, k), lambda j: (0, 0)), wspec, wspec,
                  pl.BlockSpec((None, cw.shape[1], bn), lambda j: (layer, 0, j)),
                  pl.BlockSpec((None, hist, nb, bn), lambda j: (layer, 0, 0, j))],
        out_specs=[pl.BlockSpec((m, bn), lambda j: (0, j)),
                   pl.BlockSpec((hist, nb, bn), lambda j: (0, 0, j)), wout, wout],
        compiler_params=_cparams(1),
        name="ffn_gate_up_s",
    )(h, wg, wu, cw, cache)


CONV_ROWS = 32
HALO = 32


def _ln_silu(c, g, b):
    mu = jnp.mean(c, axis=-1, keepdims=True)
    d = c - mu
    var = jnp.mean(d * d, axis=-1, keepdims=True)
    y = d * lax.rsqrt(var + EPS)
    return jax.nn.silu(y * g + b)


def _conv_p_body(u_ref, dw_ref, db_ref, lg_ref, lb_ref, o_ref, ubuf, cbuf, *, bm, taps,
                 tiles_per_seq):
    m = pl.program_id(0)
    d = u_ref.shape[1]
    nc = d // LANES

    @pl.when(m % tiles_per_seq == 0)
    def _():
        ubuf[:, 0:HALO, :] = jnp.zeros((nc, HALO, LANES), F32)

    @pl.when(m % tiles_per_seq != 0)
    def _():
        ubuf[:, 0:HALO, :] = ubuf[:, bm:bm + HALO, :]

    for c in range(nc):
        ubuf[c, HALO:HALO + bm, :] = u_ref[:, c * LANES:(c + 1) * LANES]
    off = HALO - (taps - 1)

    def lane_chunk(c, carry):
        wv = dw_ref[c]
        bias = db_ref[c]
        for r0 in range(0, bm, CONV_ROWS):
            acc = jnp.zeros((CONV_ROWS, LANES), F32)
            for k in range(taps):
                acc = acc + ubuf[c, r0 + off + k:r0 + off + k + CONV_ROWS, :] * wv[k:k + 1, :]
            cbuf[c, r0:r0 + CONV_ROWS, :] = acc + bias
        return carry

    lax.fori_loop(0, nc, lane_chunk, 0)
    cfull = jnp.concatenate([cbuf[c] for c in range(nc)], axis=-1)
    o_ref[...] = _ln_silu(cfull, lg_ref[...], lb_ref[...]).astype(o_ref.dtype)


def _lane_chunked(x):
    l, r, d = x.shape
    return x.reshape(l, r, d // LANES, LANES).transpose(0, 2, 1, 3)


def _vec3(x):
    return x.reshape(x.shape[0], 1, x.shape[1])


def _conv_prompt(u, dw, db, lg, lb, layer, *, bm, seq_len):
    m, d = u.shape
    taps = dw.shape[1]
    nc = d // LANES
    vec = lambda: pl.BlockSpec((None, 1, d), lambda i: (layer, 0, 0))
    return pl.pallas_call(
        functools.partial(_conv_p_body, bm=bm, taps=taps, tiles_per_seq=seq_len // bm),
        out_shape=jax.ShapeDtypeStruct((m, d), BF16),
        grid=(m // bm,),
        in_specs=[pl.BlockSpec((bm, d), lambda i: (i, 0)),
                  pl.BlockSpec((None, nc, taps, LANES), lambda i: (layer, 0, 0, 0)),
                  pl.BlockSpec((None, nc, 1, LANES), lambda i: (layer, 0, 0, 0)),
                  vec(), vec()],
        out_specs=pl.BlockSpec((bm, d), lambda i: (i, 0)),
        scratch_shapes=[pltpu.VMEM((nc, HALO + bm, LANES), F32),
                        pltpu.VMEM((nc, bm, LANES), F32)],
        compiler_params=_cparams(1),
        name="conv_prompt",
    )(u, _lane_chunked(dw), _lane_chunked(_vec3(db)), _vec3(lg), _vec3(lb))


def _conv_s_body(cache_ref, u_ref, dw_ref, db_ref, lg_ref, lb_ref, o_ref, st_ref, cfull, *,
                 taps, seq, nsteps):
    c = pl.program_id(0)
    hist = taps - 1
    ext = lambda r: cache_ref[r] if r < hist else u_ref[r - hist]
    wv = dw_ref[...]
    bias = db_ref[...]
    for t in range(seq):
        acc = ext(t) * wv[0:1, :]
        for k in range(1, taps):
            acc = acc + ext(t + k) * wv[k:k + 1, :]
        cfull[c, t] = acc + bias
    for r in range(hist):
        st_ref[r] = ext(r + seq)

    @pl.when(c == nsteps - 1)
    def _():
        for t in range(seq):
            row = jnp.concatenate([cfull[cc, t] for cc in range(nsteps)], axis=-1)
            o_ref[t] = _ln_silu(row, lg_ref[...], lb_ref[...]).astype(o_ref.dtype)


def _conv_sample(cache, u, dw, db, lg, lb, layer, *, bc):
    seq, b, d = u.shape
    taps = dw.shape[1]
    hist = taps - 1
    nsteps = d // bc
    vec = lambda: pl.BlockSpec((None, 1, d), lambda i: (layer, 0, 0))
    return pl.pallas_call(
        functools.partial(_conv_s_body, taps=taps, seq=seq, nsteps=nsteps),
        out_shape=[jax.ShapeDtypeStruct((seq, b, d), BF16),
                   jax.ShapeDtypeStruct((hist, b, d), F32)],
        grid=(nsteps,),
        in_specs=[pl.BlockSpec((None, hist, b, bc), lambda i: (layer, 0, 0, i)),
                  pl.BlockSpec((seq, b, bc), lambda i: (0, 0, i)),
                  pl.BlockSpec((None, taps, bc), lambda i: (layer, 0, i)),
                  pl.BlockSpec((None, 1, bc), lambda i: (layer, 0, i)),
                  vec(), vec()],
        out_specs=[pl.BlockSpec((seq, b, d), lambda i: (0, 0, 0)),
                   pl.BlockSpec((hist, b, bc), lambda i: (0, 0, i))],
        scratch_shapes=[pltpu.VMEM((nsteps, seq, b, bc), F32)],
        compiler_params=_cparams(1),
        name="conv_sample",
    )(cache, u, dw, _vec3(db), _vec3(lg), _vec3(lb))


def _s5_abar(lam_re, lam_im, dt):
    mag = jnp.exp(lam_re * dt)
    ang = lam_im * dt
    return mag * jnp.cos(ang), mag * jnp.sin(ang)


def _s5_disc_body(are_ref, aim_ref, ldt_ref, arep_ref, airep_ref, bre_ref, bim_ref,
                  abr_ref, abi_ref, bbr_ref, bbi_ref):
    dt = jnp.exp(ldt_ref[...])
    abar_re, abar_im = _s5_abar(are_ref[...], aim_ref[...], dt)
    abr_ref[...] = abar_re
    abi_ref[...] = abar_im
    lam_re = arep_ref[...]
    lam_im = airep_ref[...]
    rep_re, rep_im = _s5_abar(lam_re, lam_im, dt)
    nr = rep_re - 1.0
    ni = rep_im
    den = lam_re * lam_re + lam_im * lam_im
    q_re = (nr * lam_re + ni * lam_im) / den
    q_im = (ni * lam_re - nr * lam_im) / den
    br = bre_ref[...]
    bi = bim_ref[...]
    bbr_ref[...] = q_re * br - q_im * bi
    bbi_ref[...] = q_re * bi + q_im * br


def _s5_discretise(a_re, a_im, log_dt, b_re, b_im):
    ns, g, p = a_re.shape
    rows = ns * g
    two = lambda x: x.reshape(rows, p)
    rep = lambda x: jnp.repeat(x.reshape(rows, p), GROUP_SIZE, axis=1)
    wide = lambda x: x.reshape(rows, p * GROUP_SIZE)
    outs = pl.pallas_call(
        _s5_disc_body,
        out_shape=[jax.ShapeDtypeStruct((rows, p), F32)] * 2
        + [jax.ShapeDtypeStruct((rows, p * GROUP_SIZE), F32)] * 2,
        name="s5_discretise",
    )(two(a_re), two(a_im), log_dt.reshape(rows, 1), rep(a_re), rep(a_im), wide(b_re), wide(b_im))
    abr, abi, bbr, bbi = outs
    return (abr.reshape(ns, g, p), abi.reshape(ns, g, p),
            bbr.reshape(ns, g, p, GROUP_SIZE), bbi.reshape(ns, g, p, GROUP_SIZE))


SLAB_GROUPS = 16
SLAB = SLAB_GROUPS * GROUP_SIZE


def _s5_expand_body(xbr_ref, xbi_ref, xcr_ref, xci_ref, bm_ref, cm_ref, *, p):
    gpt = LANES // p
    tiles_slab = SLAB_GROUPS // gpt
    rg = lax.broadcasted_iota(jnp.int32, (SLAB, LANES), 0) // GROUP_SIZE
    lg = lax.broadcasted_iota(jnp.int32, (SLAB, LANES), 1) // p
    for half, x_ref in enumerate((xbr_ref, xbi_ref)):
        x = x_ref[...]
        for a in range(tiles_slab):
            col = (half * tiles_slab + a) * LANES
            bm_ref[:, col:col + LANES] = jnp.where(rg == gpt * a + lg, x, 0.0).astype(BF16)
    cg = lax.broadcasted_iota(jnp.int32, (LANES, SLAB), 1) // GROUP_SIZE
    rc = lax.broadcasted_iota(jnp.int32, (LANES, SLAB), 0) // p
    for half, x_ref in enumerate((xcr_ref, xci_ref)):
        x = x_ref[...]
        for a in range(tiles_slab):
            row = (half * tiles_slab + a) * LANES
            cm_ref[row:row + LANES, :] = jnp.where(cg == gpt * a + rc, x, 0.0).astype(BF16)


def _s5_block_weights(bb_re, bb_im, c_re, c_im):
    ns, g, p, gs = bb_re.shape
    s = g // SLAB_GROUPS
    n = ns * s
    gpt = LANES // p
    bcomp = lambda x: jnp.tile(
        x.reshape(n, SLAB_GROUPS, p, gs).transpose(0, 1, 3, 2).reshape(n, SLAB, p), (1, 1, gpt))
    ccomp = lambda x: jnp.tile(
        x.reshape(n, SLAB_GROUPS, gs, p).transpose(0, 3, 1, 2).reshape(n, p, SLAB), (1, gpt, 1))
    width = 2 * SLAB_GROUPS * p
    bspec = pl.BlockSpec((None, SLAB, LANES), lambda i: (i, 0, 0))
    cspec = pl.BlockSpec((None, LANES, SLAB), lambda i: (i, 0, 0))
    bmat, cmat = pl.pallas_call(
        functools.partial(_s5_expand_body, p=p),
        out_shape=[jax.ShapeDtypeStruct((n, SLAB, width), BF16),
                   jax.ShapeDtypeStruct((n, width, SLAB), BF16)],
        grid=(n,),
        in_specs=[bspec, bspec, cspec, cspec],
        out_specs=[pl.BlockSpec((None, SLAB, width), lambda i: (i, 0, 0)),
                   pl.BlockSpec((None, width, SLAB), lambda i: (i, 0, 0))],
        compiler_params=_cparams(1),
        name="s5_expand",
    )(bcomp(bb_re), bcomp(bb_im), ccomp(c_re), ccomp(-c_im))
    return bmat.reshape(ns, s, SLAB, width), cmat.reshape(ns, s, width, SLAB)


def _s5_pitch(rows):
    return rows + SUBLANES // 2


def _s5_project_in(u, bmat_ref, z, *, rows, pitch, n_slab, tiles_half):
    tiles_slab = tiles_half // n_slab
    for s in range(n_slab):
        bu = jnp.dot(u[:, s * SLAB:(s + 1) * SLAB].astype(BF16), bmat_ref[s],
                     preferred_element_type=F32)
        for half in range(2):
            for q in range(tiles_slab):
                j = half * tiles_half + s * tiles_slab + q
                col = (half * tiles_slab + q) * LANES
                z[j * pitch:j * pitch + rows, :] = bu[:, col:col + LANES]


def _s5_project_out(u, z, cmat_ref, d_ref, store, *, rows, pitch, n_slab, tiles_half):
    tiles_slab = tiles_half // n_slab
    for s in range(n_slab):
        parts = []
        for half in range(2):
            for q in range(tiles_slab):
                j = half * tiles_half + s * tiles_slab + q
                parts.append(z[j * pitch:j * pitch + rows, :].astype(BF16))
        st = jnp.concatenate(parts, axis=-1)
        y = jnp.dot(st, cmat_ref[s], preferred_element_type=F32)
        sl = slice(s * SLAB, (s + 1) * SLAB)
        y = y + d_ref[:, sl] * u[:, sl]
        store(sl, jax.nn.gelu(y))


def _s5_step(z, t, state, abar, *, pitch, tiles_half):
    nv = tiles_half // SUBLANES
    new = []
    for k in range(nv):
        idx_re = pl.ds(k * SUBLANES * pitch + t, SUBLANES, stride=pitch)
        idx_im = pl.ds((tiles_half + k * SUBLANES) * pitch + t, SUBLANES, stride=pitch)
        ar, ai = abar[k]
        sr, si = state[k]
        nr = ar * sr - ai * si + z[idx_re, :]
        ni = ar * si + ai * sr + z[idx_im, :]
        z[idx_re, :] = nr
        z[idx_im, :] = ni
        new.append((nr, ni))
    return tuple(new)


def _s5_p_body(u_ref, bmat_ref, cmat_ref, abr_ref, abi_ref, d_ref, o_ref, sre_ref, sim_ref,
               z, st_re, st_im, *, rows, tiles_per_seq, n_slab, tiles_half):
    m = pl.program_id(0)
    pitch = _s5_pitch(rows)
    nv = tiles_half // SUBLANES
    u = u_ref[...]
    _s5_project_in(u, bmat_ref, z, rows=rows, pitch=pitch, n_slab=n_slab, tiles_half=tiles_half)

    @pl.when(m % tiles_per_seq == 0)
    def _():
        st_re[...] = jnp.zeros(st_re.shape, F32)
        st_im[...] = jnp.zeros(st_im.shape, F32)

    vsl = lambda k: slice(k * SUBLANES, (k + 1) * SUBLANES)
    abar = tuple((abr_ref[vsl(k), :], abi_ref[vsl(k), :]) for k in range(nv))
    state0 = tuple((st_re[vsl(k), :], st_im[vsl(k), :]) for k in range(nv))

    def step(t, state):
        return _s5_step(z, t, state, abar, pitch=pitch, tiles_half=tiles_half)

    state = lax.fori_loop(0, rows, step, state0, unroll=8)
    for k in range(nv):
        st_re[vsl(k), :] = state[k][0]
        st_im[vsl(k), :] = state[k][1]
    sre_ref[...] = st_re[...]
    sim_ref[...] = st_im[...]

    def store(sl, val):
        o_ref[:, sl] = val.astype(o_ref.dtype)

    _s5_project_out(u, z, cmat_ref, d_ref, store, rows=rows, pitch=pitch, n_slab=n_slab,
                    tiles_half=tiles_half)


def _s5_prompt(u, bmat, cmat, abar_re, abar_im, dskip, *, layer, rows, seq_len):
    m, d = u.shape
    n_slab = bmat.shape[1]
    wsel = lambda a: pl.BlockSpec((None,) + a.shape[1:], lambda i: (layer, 0, 0, 0))
    tiles_half = abar_re.shape[0]
    pitch = _s5_pitch(rows)
    tiles_per_seq = seq_len // rows
    n_seq = m // seq_len
    st_spec = pl.BlockSpec((None, tiles_half, LANES), lambda i: (i // tiles_per_seq, 0, 0))
    full = lambda a: pl.BlockSpec(a.shape, lambda i: (0,) * a.ndim)
    return pl.pallas_call(
        functools.partial(_s5_p_body, rows=rows, tiles_per_seq=tiles_per_seq, n_slab=n_slab,
                          tiles_half=tiles_half),
        out_shape=[jax.ShapeDtypeStruct((m, d), BF16),
                   jax.ShapeDtypeStruct((n_seq, tiles_half, LANES), F32),
                   jax.ShapeDtypeStruct((n_seq, tiles_half, LANES), F32)],
        grid=(m // rows,),
        in_specs=[pl.BlockSpec((rows, d), lambda i: (i, 0)),
                  wsel(bmat), wsel(cmat), full(abar_re), full(abar_im), full(dskip)],
        out_specs=[pl.BlockSpec((rows, d), lambda i: (i, 0)), st_spec, st_spec],
        scratch_shapes=[pltpu.VMEM((2 * tiles_half * pitch, LANES), F32),
                        pltpu.VMEM((tiles_half, LANES), F32),
                        pltpu.VMEM((tiles_half, LANES), F32)],
        compiler_params=_cparams(1),
        name="s5_prompt",
    )(u, bmat, cmat, abar_re, abar_im, dskip)


def _s5_s_body(u_ref, h0r_ref, h0i_ref, bmat_ref, cmat_ref, abr_ref, abi_ref, d_ref,
               o_ref, sre_ref, sim_ref, z, zh, *, nb, seq, n_slab, tiles_half):
    rows = nb * seq
    pitch = _s5_pitch(rows)
    hp = _s5_pitch(nb)
    nv = tiles_half // SUBLANES
    d = u_ref.shape[2]
    u = u_ref[...].reshape(rows, d)
    _s5_project_in(u, bmat_ref, z, rows=rows, pitch=pitch, n_slab=n_slab, tiles_half=tiles_half)
    for j in range(tiles_half):
        cs = slice(j * LANES, (j + 1) * LANES)
        zh[j * hp:j * hp + nb, :] = h0r_ref[:, cs]
        zh[(tiles_half + j) * hp:(tiles_half + j) * hp + nb, :] = h0i_ref[:, cs]

    vsl = lambda k: slice(k * SUBLANES, (k + 1) * SUBLANES)
    abar = tuple((abr_ref[vsl(k), :], abi_ref[vsl(k), :]) for k in range(nv))

    def one_seq(b, carry):
        state = []
        for k in range(nv):
            idx_re = pl.ds(k * SUBLANES * hp + b, SUBLANES, stride=hp)
            idx_im = pl.ds((tiles_half + k * SUBLANES) * hp + b, SUBLANES, stride=hp)
            state.append((zh[idx_re, :], zh[idx_im, :]))
        state = tuple(state)
        for t in range(seq):
            state = _s5_step(z, t * nb + b, state, abar, pitch=pitch, tiles_half=tiles_half)
        for k in range(nv):
            idx_re = pl.ds(k * SUBLANES * hp + b, SUBLANES, stride=hp)
            idx_im = pl.ds((tiles_half + k * SUBLANES) * hp + b, SUBLANES, stride=hp)
            zh[idx_re, :] = state[k][0]
            zh[idx_im, :] = state[k][1]
        return carry

    lax.fori_loop(0, nb, one_seq, 0)
    for j in range(tiles_half):
        cs = slice(j * LANES, (j + 1) * LANES)
        sre_ref[:, cs] = zh[j * hp:j * hp + nb, :]
        sim_ref[:, cs] = zh[(tiles_half + j) * hp:(tiles_half + j) * hp + nb, :]

    def store(sl, val):
        o_ref[:, :, sl] = val.reshape(seq, nb, val.shape[1]).astype(o_ref.dtype)

    _s5_project_out(u, z, cmat_ref, d_ref, store, rows=rows, pitch=pitch, n_slab=n_slab,
                    tiles_half=tiles_half)


def _s5_sample(u, h0_re, h0_im, bmat, cmat, abar_re, abar_im, dskip, *, layer, nb):
    seq, b, d = u.shape
    nstate = h0_re.shape[1]
    n_slab = bmat.shape[1]
    wsel = lambda a: pl.BlockSpec((None,) + a.shape[1:], lambda i: (layer, 0, 0, 0))
    tiles_half = abar_re.shape[0]
    pitch = _s5_pitch(nb * seq)
    hp = _s5_pitch(nb)
    full = lambda a: pl.BlockSpec(a.shape, lambda i: (0,) * a.ndim)
    st_spec = pl.BlockSpec((nb, nstate), lambda i: (i, 0))
    u_spec = pl.BlockSpec((seq, nb, d), lambda i: (0, i, 0))
    return pl.pallas_call(
        functools.partial(_s5_s_body, nb=nb, seq=seq, n_slab=n_slab, tiles_half=tiles_half),
        out_shape=[jax.ShapeDtypeStruct((seq, b, d), BF16),
                   jax.ShapeDtypeStruct(h0_re.shape, F32),
                   jax.ShapeDtypeStruct(h0_im.shape, F32)],
        grid=(b // nb,),
        in_specs=[u_spec, st_spec, st_spec,
                  wsel(bmat), wsel(cmat), full(abar_re), full(abar_im), full(dskip)],
        out_specs=[u_spec, st_spec, st_spec],
        scratch_shapes=[pltpu.VMEM((2 * tiles_half * pitch, LANES), F32),
                        pltpu.VMEM((2 * tiles_half * hp, LANES), F32)],
        compiler_params=_cparams(1),
        name="s5_sample",
    )(u, h0_re, h0_im, bmat, cmat, abar_re, abar_im, dskip)


def _hn_dtype(i):
    return F32 if (i % N_MIXERS == 1) else BF16


def _run_sample(x, states, p):
    seq, nb, d = x.shape
    m = seq * nb
    depth = p["norm_mix"].shape[0]
    conv_st, ssm_re_st, ssm_im_st, ffn_st = states
    new_conv, new_re, new_im, new_ffn, wbf = [], [], [], [], []
    x = x.reshape(m, d)
    hn = _rmsnorm(x, p["norm_mix"][0], _hn_dtype(0), m)
    y = None
    for i in range(depth):
        j = i // N_MIXERS
        w = {}
        if i % N_MIXERS == 0:
            u, w["in_a"], w["in_b"] = _mm(hn, p["conv_w_in"], glu=True, bm=m, bn=512, layer=j)
            c, st = _conv_sample(conv_st, u.reshape(seq, nb, d), p["conv_dw"], p["conv_dw_b"],
                                 p["conv_ln_g"], p["conv_ln_b"], j, bc=256)
            new_conv.append(st)
            x, hn, w["out"] = _mm(c.reshape(m, d), p["conv_w_out"], glu=False, bm=m, bn=512,
                                  layer=j, resid=x, gamma=p["norm_ffn"][i], hn_dtype=BF16)
        else:
            v, s_re, s_im = _s5_sample(hn.reshape(seq, nb, d), ssm_re_st[j].reshape(nb, -1),
                                       ssm_im_st[j].reshape(nb, -1), p["s5_bmat"],
                                       p["s5_cmat"], p["s5_abar_re"][j], p["s5_abar_im"][j],
                                       p["ssm_D"][j].reshape(1, d), layer=j, nb=32)
            new_re.append(s_re.reshape(ssm_re_st.shape[1:]))
            new_im.append(s_im.reshape(ssm_im_st.shape[1:]))
            x, hn, w["glu_a"], w["glu_b"] = _mm(
                v.reshape(m, d), p["ssm_w_glu"], glu=True, bm=m, bn=512, layer=j, resid=x,
                gamma=p["norm_ffn"][i], hn_dtype=BF16)
        act, st, w["gate"], w["up"] = _ffn1_sample(hn, p["ffn_w_gate"], p["ffn_w_up"],
                                                   p["ffn_conv"], ffn_st, i, bn=512, seq=seq)
        new_ffn.append(st)
        last = i == depth - 1
        gamma = p["norm_final"] if last else p["norm_mix"][i + 1]
        outs = _mm(act, p["ffn_w_down"], glu=False, bm=m, bn=512, layer=i, resid=x, gamma=gamma,
                   hn_dtype=F32 if last else _hn_dtype(i + 1), write_x=not last)
        if last:
            y, w["down"] = outs
        else:
            x, hn, w["down"] = outs
        wbf.append(w)
    return (y.reshape(seq, nb, d), jnp.stack(new_conv), jnp.stack(new_re), jnp.stack(new_im),
            jnp.stack(new_ffn)), wbf


def _run_prompt(x, p, wbf, *, seq_len, bm):
    m, d = x.shape
    n_seq = m // seq_len
    depth = p["norm_mix"].shape[0]
    new_conv, new_re, new_im, new_ffn = [], [], [], []
    hn = None
    y = None
    for i in range(depth):
        j = i // N_MIXERS
        w = wbf[i]
        if i % N_MIXERS == 0:
            if i == 0:
                (u,) = _mm(x, (w["in_a"], w["in_b"]), glu=True, bm=bm, bn=1024,
                           prenorm=p["norm_mix"][0])
            else:
                (u,) = _mm(hn, (w["in_a"], w["in_b"]), glu=True, bm=bm, bn=1024)
            cs = p["conv_dw"].shape[1] - 1
            new_conv.append(u.reshape(n_seq, seq_len, d)[:, -cs:])
            c = _conv_prompt(u, p["conv_dw"], p["conv_dw_b"], p["conv_ln_g"], p["conv_ln_b"], j,
                             bm=512, seq_len=seq_len)
            x, hn = _mm_resident(c, (w["out"],), glu=False, bm=512, bn=512, resid=x,
                                 gamma=p["norm_ffn"][i], hn_dtype=BF16)
        else:
            v, s_re, s_im = _s5_prompt(hn, p["s5_bmat"], p["s5_cmat"], p["s5_abar_re"][j],
                                       p["s5_abar_im"][j], p["ssm_D"][j].reshape(1, d), layer=j, rows=256,
                                       seq_len=seq_len)
            g, pdim = p["ssm_A_re"].shape[1:]
            new_re.append(s_re.reshape(n_seq, g, pdim))
            new_im.append(s_im.reshape(n_seq, g, pdim))
            x, hn = _mm_resident(v, (w["glu_a"], w["glu_b"]), glu=True, bm=512, bn=512, resid=x,
                        gamma=p["norm_ffn"][i], hn_dtype=BF16)
        act, tail = _ffn1_prompt(hn, w["gate"], w["up"], p["ffn_conv"], i, bm=seq_len, bn=512,
                                 seq_len=seq_len)
        fs = p["ffn_conv"].shape[1] - 1
        new_ffn.append(tail[:, -fs:])
        last = i == depth - 1
        gamma = p["norm_final"] if last else p["norm_mix"][i + 1]
        outs = _mm_resident(act, (w["down"],), glu=False, bm=256, bn=512, resid=x, gamma=gamma,
                   hn_dtype=F32 if last else _hn_dtype(i + 1), write_x=not last)
        if last:
            (y,) = outs
        else:
            x, hn = outs
    return (y, jnp.stack(new_conv), jnp.stack(new_re), jnp.stack(new_im), jnp.stack(new_ffn))


def kernel(x_prompt, x_sample, state_conv, state_ssm_re, state_ssm_im, state_ffn, norm_mix, norm_ffn, norm_final, conv_w_in, conv_dw, conv_dw_b, conv_ln_g, conv_ln_b, conv_w_out, ssm_A_re, ssm_A_im, ssm_log_dt, ssm_B_re, ssm_B_im, ssm_C_re, ssm_C_im, ssm_D, ssm_w_glu, ffn_w_gate, ffn_w_up, ffn_conv, ffn_w_down):
    bp, sp, d = x_prompt.shape
    abr, abi, bbr, bbi = _s5_discretise(ssm_A_re, ssm_A_im, ssm_log_dt, ssm_B_re, ssm_B_im)
    ns, g, pdim = ssm_A_re.shape
    bmat, cmat = _s5_block_weights(bbr, bbi, ssm_C_re, ssm_C_im)
    tiles_half = g * pdim // LANES
    p = dict(
        norm_mix=norm_mix, norm_ffn=norm_ffn, norm_final=norm_final,
        conv_w_in=conv_w_in, conv_dw=conv_dw, conv_dw_b=conv_dw_b,
        conv_ln_g=conv_ln_g, conv_ln_b=conv_ln_b, conv_w_out=conv_w_out,
        ssm_A_re=ssm_A_re, ssm_D=ssm_D, ssm_w_glu=ssm_w_glu,
        s5_bmat=bmat, s5_cmat=cmat,
        s5_abar_re=abr.reshape(ns, tiles_half, LANES), s5_abar_im=abi.reshape(ns, tiles_half, LANES),
        ffn_w_gate=ffn_w_gate, ffn_w_up=ffn_w_up, ffn_conv=ffn_conv, ffn_w_down=ffn_w_down,
    )
    tm = lambda a: jnp.swapaxes(a, -3, -2)
    (y_s, conv_s, re_s, im_s, ffn_s), wbf = _run_sample(
        tm(x_sample), (tm(state_conv), state_ssm_re, state_ssm_im, tm(state_ffn)), p)
    y_p, conv_p, re_p, im_p, ffn_p = _run_prompt(
        x_prompt.reshape(bp * sp, d), p, wbf, seq_len=sp, bm=1024)
    return (y_p.reshape(bp, sp, d), tm(y_s), conv_p, tm(conv_s), re_p, im_p, re_s, im_s, ffn_p,
            tm(ffn_s))
```

```python
import functools

import jax
import jax.numpy as jnp
from jax import lax
from jax.experimental import pallas as pl
from jax.experimental.pallas import tpu as pltpu

F32 = jnp.float32
BF16 = jnp.bfloat16

EPS = 1e-6
N_MIXERS = 2
GROUP_SIZE = 16
LANES = 128
SUBLANES = 8
VMEM_LIMIT = 56 * 1024 * 1024
ROW_CHUNK = 256
FFN_ROW_CHUNK = 1024


def _cparams(n_axes):
    return pltpu.CompilerParams(
        dimension_semantics=("arbitrary",) * n_axes, vmem_limit_bytes=VMEM_LIMIT)


def _rms_scale(ss, n):
    return lax.rsqrt(ss / n + EPS)


def _rmsnorm_body(x_ref, g_ref, o_ref):
    x = x_ref[...]
    ss = jnp.sum(x * x, axis=-1, keepdims=True)
    o_ref[...] = ((x * _rms_scale(ss, x.shape[-1])) * g_ref[...]).astype(o_ref.dtype)


def _rmsnorm(x, g, out_dtype, bm):
    m, d = x.shape
    return pl.pallas_call(
        _rmsnorm_body,
        out_shape=jax.ShapeDtypeStruct((m, d), out_dtype),
        grid=(m // bm,),
        in_specs=[pl.BlockSpec((bm, d), lambda i: (i, 0)),
                  pl.BlockSpec((1, d), lambda i: (0, 0))],
        out_specs=pl.BlockSpec((bm, d), lambda i: (i, 0)),
        compiler_params=_cparams(1),
        name="rmsnorm",
    )(x, g.reshape(1, d))


def _mm_body(*refs, glu, has_res, write_x, emit, prenorm, nn, bn, n_out):
    it = iter(refs)
    lhs_ref = next(it)
    pg_ref = next(it) if prenorm else None
    wa_ref = next(it)
    wb_ref = next(it) if glu else None
    res_ref = next(it) if has_res else None
    g_ref = next(it) if has_res else None
    xo_ref = next(it) if write_x else None
    hn_ref = next(it) if has_res else None
    wao_ref = next(it) if emit else None
    wbo_ref = next(it) if (emit and glu) else None
    rowbuf = next(it) if has_res else None
    ss_ref = next(it) if has_res else None
    hn_scr = next(it) if prenorm else None

    n = pl.program_id(1)
    bm = lhs_ref.shape[0]
    if prenorm:
        @pl.when(n == 0)
        def _():
            x = lhs_ref[...]
            ss = jnp.sum(x * x, axis=-1, keepdims=True)
            hn_scr[...] = ((x * _rms_scale(ss, x.shape[-1])) * pg_ref[...]).astype(BF16)
        lhs_ref = hn_scr
    if emit:
        wao_ref[...] = wa_ref[...].astype(BF16)
        wa_ref = wao_ref
        if glu:
            wbo_ref[...] = wb_ref[...].astype(BF16)
            wb_ref = wbo_ref
    if has_res:
        @pl.when(n == 0)
        def _():
            ss_ref[...] = jnp.zeros(ss_ref.shape, F32)
    for r0 in range(0, bm, ROW_CHUNK):
        rs = slice(r0, min(r0 + ROW_CHUNK, bm))
        x = lhs_ref[rs, :]
        val = jnp.dot(x, wa_ref[...], preferred_element_type=F32)
        if glu:
            gate = jnp.dot(x, wb_ref[...], preferred_element_type=F32)
            val = val * jax.nn.sigmoid(gate)
        if has_res:
            val = res_ref[rs, :] + val
        if write_x:
            xo_ref[rs, :] = val
        if has_res:
            rowbuf[n, rs, :] = val
            ss_ref[rs, :] = ss_ref[rs, :] + jnp.sum(val * val, axis=-1, keepdims=True)

    if has_res:
        @pl.when(n == nn - 1)
        def _():
            scale = _rms_scale(ss_ref[...], n_out)
            for j in range(nn):
                sl = slice(j * bn, (j + 1) * bn)
                hn_ref[:, sl] = ((rowbuf[j] * scale) * g_ref[:, sl]).astype(hn_ref.dtype)


def _mm(lhs, w, *, glu, bm, bn, layer=None, resid=None, gamma=None, hn_dtype=None, write_x=True,
        prenorm=None):
    m, k = lhs.shape
    emit = layer is not None
    if emit:
        assert m == bm
        n_out = w.shape[2] // (2 if glu else 1)
    else:
        n_out = w[0].shape[1]
    nn = n_out // bn
    has_res = resid is not None
    in_specs = [pl.BlockSpec((bm, k), lambda i, j: (i, 0))]
    args = [lhs]
    if prenorm is not None:
        in_specs.append(pl.BlockSpec((1, k), lambda i, j: (0, 0)))
        args.append(prenorm.reshape(1, k))
    if emit:
        in_specs.append(pl.BlockSpec((None, k, bn), lambda i, j: (layer, 0, j)))
        args.append(w)
        if glu:
            in_specs.append(pl.BlockSpec((None, k, bn), lambda i, j: (layer, 0, j + nn)))
            args.append(w)
    else:
        for wi in w:
            in_specs.append(pl.BlockSpec((k, bn), lambda i, j: (0, j)))
            args.append(wi)
    out_shape, out_specs, scratch = [], [], []
    if has_res:
        in_specs += [pl.BlockSpec((bm, bn), lambda i, j: (i, j)),
                     pl.BlockSpec((1, n_out), lambda i, j: (0, 0))]
        args += [resid, gamma.reshape(1, n_out)]
    if write_x:
        out_shape.append(jax.ShapeDtypeStruct((m, n_out), F32))
        out_specs.append(pl.BlockSpec((bm, bn), lambda i, j: (i, j)))
    if has_res:
        out_shape.append(jax.ShapeDtypeStruct((m, n_out), hn_dtype))
        out_specs.append(pl.BlockSpec((bm, n_out), lambda i, j: (i, 0)))
        scratch += [pltpu.VMEM((nn, bm, bn), F32), pltpu.VMEM((bm, 1), F32)]
    if emit:
        for _ in range(2 if glu else 1):
            out_shape.append(jax.ShapeDtypeStruct((k, n_out), BF16))
            out_specs.append(pl.BlockSpec((k, bn), lambda i, j: (0, j)))
    if prenorm is not None:
        scratch.append(pltpu.VMEM((bm, k), BF16))
    return pl.pallas_call(
        functools.partial(_mm_body, glu=glu, has_res=has_res, write_x=write_x, emit=emit,
                          prenorm=prenorm is not None, nn=nn, bn=bn, n_out=n_out),
        out_shape=out_shape,
        grid=(m // bm, nn),
        in_specs=in_specs,
        out_specs=out_specs,
        scratch_shapes=scratch,
        compiler_params=_cparams(2),
        name="mm_glu" if glu else "mm_lin",
    )(*args)


def _mmr_body(*refs, glu, write_x, bn, n_out):
    it = iter(refs)
    lhs_ref = next(it)
    wa_ref = next(it)
    wb_ref = next(it) if glu else None
    res_ref = next(it)
    g_ref = next(it)
    xo_ref = next(it) if write_x else None
    hn_ref = next(it)
    bm = lhs_ref.shape[0]
    rows = min(ROW_CHUNK, bm)
    for r0 in range(0, bm, rows):
        rs = slice(r0, r0 + rows)
        x = lhs_ref[rs, :]
        ss = jnp.zeros((rows, 1), F32)
        vals = []
        for j in range(n_out // bn):
            sl = slice(j * bn, (j + 1) * bn)
            val = jnp.dot(x, wa_ref[:, sl], preferred_element_type=F32)
            if glu:
                gate = jnp.dot(x, wb_ref[:, sl], preferred_element_type=F32)
                val = val * jax.nn.sigmoid(gate)
            val = res_ref[rs, sl] + val
            if write_x:
                xo_ref[rs, sl] = val
            ss = ss + jnp.sum(val * val, axis=-1, keepdims=True)
            vals.append(val)
        scale = _rms_scale(ss, n_out)
        for j, val in enumerate(vals):
            sl = slice(j * bn, (j + 1) * bn)
            hn_ref[rs, sl] = ((val * scale) * g_ref[:, sl]).astype(hn_ref.dtype)


def _mm_resident(lhs, w, *, glu, bm, bn, resid, gamma, hn_dtype, write_x=True):
    m, k = lhs.shape
    n_out = w[0].shape[1]
    row = lambda width: pl.BlockSpec((bm, width), lambda i: (i, 0))
    in_specs = [row(k)] + [pl.BlockSpec((k, n_out), lambda i: (0, 0)) for _ in w]
    in_specs += [row(n_out), pl.BlockSpec((1, n_out), lambda i: (0, 0))]
    out_shape, out_specs = [], []
    if write_x:
        out_shape.append(jax.ShapeDtypeStruct((m, n_out), F32))
        out_specs.append(row(n_out))
    out_shape.append(jax.ShapeDtypeStruct((m, n_out), hn_dtype))
    out_specs.append(row(n_out))
    return pl.pallas_call(
        functools.partial(_mmr_body, glu=glu, write_x=write_x, bn=bn, n_out=n_out),
        out_shape=out_shape,
        grid=(m // bm,),
        in_specs=in_specs,
        out_specs=out_specs,
        compiler_params=_cparams(1),
        name="mmr_glu" if glu else "mmr_lin",
    )(lhs, *w, resid, gamma.reshape(1, n_out))


def _shift_rows(g, tail, k):
    r = pltpu.roll(g, k, 0)
    t = pltpu.roll(tail, k, 0)
    i8 = lax.broadcasted_iota(jnp.int32, tail.shape, 0)
    head = jnp.where(i8 < k, t, r[0:SUBLANES, :])
    return jnp.concatenate([head, r[SUBLANES:, :]], axis=0)


def _ffn1_prompt_body(h_ref, wg_ref, wu_ref, cw_ref, act_ref, tail_ref, carry, *, bm, tiles_per_seq):
    m = pl.program_id(0)
    n = pl.program_id(1)
    cw = cw_ref[...]
    taps = cw.shape[0]

    @pl.when(m % tiles_per_seq == 0)
    def _():
        carry[n] = jnp.zeros(carry.shape[1:], F32)

    tail = carry[n]
    for r0 in range(0, bm, FFN_ROW_CHUNK):
        rs = slice(r0, r0 + FFN_ROW_CHUNK)
        h = h_ref[rs, :]
        g = jnp.dot(h, wg_ref[...], preferred_element_type=F32)
        up = jnp.dot(h, wu_ref[...], preferred_element_type=F32)
        gc = g * cw[taps - 1:taps, :]
        for k in range(taps - 1):
            gc = gc + _shift_rows(g, tail, taps - 1 - k) * cw[k:k + 1, :]
        act_ref[rs, :] = (jax.nn.silu(gc) * up).astype(act_ref.dtype)
        tail = g[FFN_ROW_CHUNK - SUBLANES:, :]
    carry[n] = tail
    tail_ref[...] = tail


def _ffn1_prompt(h, wg, wu, cw, layer, *, bm, bn, seq_len):
    m, k = h.shape
    f = wg.shape[1]
    nn = f // bn
    tiles_per_seq = seq_len // bm
    act, tails = pl.pallas_call(
        functools.partial(_ffn1_prompt_body, bm=bm, tiles_per_seq=tiles_per_seq),
        out_shape=[jax.ShapeDtypeStruct((m, f), BF16),
                   jax.ShapeDtypeStruct((m // bm, SUBLANES, f), F32)],
        grid=(m // bm, nn),
        in_specs=[pl.BlockSpec((bm, k), lambda i, j: (i, 0)),
                  pl.BlockSpec((k, bn), lambda i, j: (0, j)),
                  pl.BlockSpec((k, bn), lambda i, j: (0, j)),
                  pl.BlockSpec((None, cw.shape[1], bn), lambda i, j: (layer, 0, j))],
        out_specs=[pl.BlockSpec((bm, bn), lambda i, j: (i, j)),
                   pl.BlockSpec((None, SUBLANES, bn), lambda i, j: (i, 0, j))],
        scratch_shapes=[pltpu.VMEM((nn, SUBLANES, bn), F32)],
        compiler_params=_cparams(2),
        name="ffn_gate_up",
    )(h, wg, wu, cw)
    return act, tails[tiles_per_seq - 1::tiles_per_seq]


def _ffn1_sample_body(h_ref, wg_ref, wu_ref, cw_ref, c_ref, act_ref, st_ref, wgo_ref, wuo_ref, *,
                      nb, seq):
    wgo_ref[...] = wg_ref[...].astype(BF16)
    wuo_ref[...] = wu_ref[...].astype(BF16)
    h = h_ref[...]
    g = jnp.dot(h, wgo_ref[...], preferred_element_type=F32)
    up = jnp.dot(h, wuo_ref[...], preferred_element_type=F32)
    cw = cw_ref[...]
    taps = cw.shape[0]
    hist = [c_ref[r] for r in range(taps - 1)] + [g[t * nb:(t + 1) * nb, :] for t in range(seq)]
    for t in range(seq):
        gc = hist[t] * cw[0:1, :]
        for k in range(1, taps):
            gc = gc + hist[t + k] * cw[k:k + 1, :]
        act_ref[t * nb:(t + 1) * nb, :] = (
            jax.nn.silu(gc) * up[t * nb:(t + 1) * nb, :]).astype(act_ref.dtype)
    for r in range(taps - 1):
        st_ref[r] = hist[seq + r]


def _ffn1_sample(h, wg, wu, cw, cache, layer, *, bn, seq):
    m, k = h.shape
    f = wg.shape[2]
    nb = m // seq
    hist = cw.shape[1] - 1
    wspec = pl.BlockSpec((None, k, bn), lambda j: (layer, 0, j))
    wout = pl.BlockSpec((k, bn), lambda j: (0, j))
    return pl.pallas_call(
        functools.partial(_ffn1_sample_body, nb=nb, seq=seq),
        out_shape=[jax.ShapeDtypeStruct((m, f), BF16),
                   jax.ShapeDtypeStruct((hist, nb, f), F32),
                   jax.ShapeDtypeStruct((k, f), BF16),
                   jax.ShapeDtypeStruct((k, f), BF16)],
        grid=(f // bn,),
        in_specs=[pl.BlockSpec((m, k), lambda j: (0, 0)), wspec, wspec,
                  pl.BlockSpec((None, cw.shape[1], bn), lambda j: (layer, 0, j)),
                  pl.BlockSpec((None, hist, nb, bn), lambda j: (layer, 0, 0, j))],
        out_specs=[pl.BlockSpec((m, bn), lambda j: (0, j)),
                   pl.BlockSpec((hist, nb, bn), lambda j: (0, 0, j)), wout, wout],
        compiler_params=_cparams(1),
        name="ffn_gate_up_s",
    )(h, wg, wu, cw, cache)


CONV_ROWS = 32
HALO = 32


def _ln_silu(c, g, b):
    mu = jnp.mean(c, axis=-1, keepdims=True)
    d = c - mu
    var = jnp.mean(d * d, axis=-1, keepdims=True)
    y = d * lax.rsqrt(var + EPS)
    return jax.nn.silu(y * g + b)


def _conv_p_body(u_ref, dw_ref, db_ref, lg_ref, lb_ref, o_ref, ubuf, cbuf, *, bm, taps,
                 tiles_per_seq):
    m = pl.program_id(0)
    d = u_ref.shape[1]
    nc = d // LANES

    @pl.when(m % tiles_per_seq == 0)
    def _():
        ubuf[:, 0:HALO, :] = jnp.zeros((nc, HALO, LANES), F32)

    @pl.when(m % tiles_per_seq != 0)
    def _():
        ubuf[:, 0:HALO, :] = ubuf[:, bm:bm + HALO, :]

    for c in range(nc):
        ubuf[c, HALO:HALO + bm, :] = u_ref[:, c * LANES:(c + 1) * LANES]
    off = HALO - (taps - 1)

    def lane_chunk(c, carry):
        wv = dw_ref[c]
        bias = db_ref[c]
        for r0 in range(0, bm, CONV_ROWS):
            acc = jnp.zeros((CONV_ROWS, LANES), F32)
            for k in range(taps):
                acc = acc + ubuf[c, r0 + off + k:r0 + off + k + CONV_ROWS, :] * wv[k:k + 1, :]
            cbuf[c, r0:r0 + CONV_ROWS, :] = acc + bias
        return carry

    lax.fori_loop(0, nc, lane_chunk, 0)
    cfull = jnp.concatenate([cbuf[c] for c in range(nc)], axis=-1)
    o_ref[...] = _ln_silu(cfull, lg_ref[...], lb_ref[...]).astype(o_ref.dtype)


def _lane_chunked(x):
    l, r, d = x.shape
    return x.reshape(l, r, d // LANES, LANES).transpose(0, 2, 1, 3)


def _vec3(x):
    return x.reshape(x.shape[0], 1, x.shape[1])


def _conv_prompt(u, dw, db, lg, lb, layer, *, bm, seq_len):
    m, d = u.shape
    taps = dw.shape[1]
    nc = d // LANES
    vec = lambda: pl.BlockSpec((None, 1, d), lambda i: (layer, 0, 0))
    return pl.pallas_call(
        functools.partial(_conv_p_body, bm=bm, taps=taps, tiles_per_seq=seq_len // bm),
        out_shape=jax.ShapeDtypeStruct((m, d), BF16),
        grid=(m // bm,),
        in_specs=[pl.BlockSpec((bm, d), lambda i: (i, 0)),
                  pl.BlockSpec((None, nc, taps, LANES), lambda i: (layer, 0, 0, 0)),
                  pl.BlockSpec((None, nc, 1, LANES), lambda i: (layer, 0, 0, 0)),
                  vec(), vec()],
        out_specs=pl.BlockSpec((bm, d), lambda i: (i, 0)),
        scratch_shapes=[pltpu.VMEM((nc, HALO + bm, LANES), F32),
                        pltpu.VMEM((nc, bm, LANES), F32)],
        compiler_params=_cparams(1),
        name="conv_prompt",
    )(u, _lane_chunked(dw), _lane_chunked(_vec3(db)), _vec3(lg), _vec3(lb))


def _conv_s_body(cache_ref, u_ref, dw_ref, db_ref, lg_ref, lb_ref, *rest, taps, seq, nsteps):
    o_ref, st_ref, cfull = rest[-3:]
    c = pl.program_id(0)
    hist = taps - 1
    ext = lambda r: cache_ref[r] if r < hist else u_ref[r - hist]
    wv = dw_ref[...]
    bias = db_ref[...]
    for t in range(seq):
        acc = ext(t) * wv[0:1, :]
        for k in range(1, taps):
            acc = acc + ext(t + k) * wv[k:k + 1, :]
        cfull[c, t] = acc + bias
    for r in range(hist):
        st_ref[r] = ext(r + seq)

    @pl.when(c == nsteps - 1)
    def _():
        for t in range(seq):
            row = jnp.concatenate([cfull[cc, t] for cc in range(nsteps)], axis=-1)
            o_ref[t] = _ln_silu(row, lg_ref[...], lb_ref[...]).astype(o_ref.dtype)


def _conv_sample(cache, u, dw, db, lg, lb, layer, *, bc, stacked=None):
    seq, b, d = u.shape
    taps = dw.shape[1]
    hist = taps - 1
    nsteps = d // bc
    vec = lambda: pl.BlockSpec((None, 1, d), lambda i: (layer, 0, 0))
    in_specs = [pl.BlockSpec((None, hist, b, bc), lambda i: (layer, 0, 0, i)),
                pl.BlockSpec((seq, b, bc), lambda i: (0, 0, i)),
                pl.BlockSpec((None, taps, bc), lambda i: (layer, 0, i)),
                pl.BlockSpec((None, 1, bc), lambda i: (layer, 0, i)),
                vec(), vec()]
    args = [cache, u, dw, _vec3(db), _vec3(lg), _vec3(lb)]
    aliases = {}
    if stacked is not None:
        in_specs.append(pl.BlockSpec(memory_space=pl.ANY))
        args.append(stacked)
        aliases = {len(args) - 1: 1}
    return pl.pallas_call(
        functools.partial(_conv_s_body, taps=taps, seq=seq, nsteps=nsteps),
        out_shape=[jax.ShapeDtypeStruct((seq, b, d), BF16),
                   jax.ShapeDtypeStruct(cache.shape, F32)],
        grid=(nsteps,),
        in_specs=in_specs,
        out_specs=[pl.BlockSpec((seq, b, d), lambda i: (0, 0, 0)),
                   pl.BlockSpec((None, hist, b, bc), lambda i: (layer, 0, 0, i))],
        scratch_shapes=[pltpu.VMEM((nsteps, seq, b, bc), F32)],
        input_output_aliases=aliases,
        compiler_params=_cparams(1),
        name="conv_sample",
    )(*args)


def _s5_abar(lam_re, lam_im, dt):
    mag = jnp.exp(lam_re * dt)
    ang = lam_im * dt
    return mag * jnp.cos(ang), mag * jnp.sin(ang)


def _s5_disc_body(are_ref, aim_ref, ldt_ref, arep_ref, airep_ref, bre_ref, bim_ref,
                  abr_ref, abi_ref, bbr_ref, bbi_ref):
    dt = jnp.exp(ldt_ref[...])
    abar_re, abar_im = _s5_abar(are_ref[...], aim_ref[...], dt)
    abr_ref[...] = abar_re
    abi_ref[...] = abar_im
    lam_re = arep_ref[...]
    lam_im = airep_ref[...]
    rep_re, rep_im = _s5_abar(lam_re, lam_im, dt)
    nr = rep_re - 1.0
    ni = rep_im
    den = lam_re * lam_re + lam_im * lam_im
    q_re = (nr * lam_re + ni * lam_im) / den
    q_im = (ni * lam_re - nr * lam_im) / den
    br = bre_ref[...]
    bi = bim_ref[...]
    bbr_ref[...] = q_re * br - q_im * bi
    bbi_ref[...] = q_re * bi + q_im * br


def _s5_discretise(a_re, a_im, log_dt, b_re, b_im):
    ns, g, p = a_re.shape
    rows = ns * g
    two = lambda x: x.reshape(rows, p)
    rep = lambda x: jnp.repeat(x.reshape(rows, p), GROUP_SIZE, axis=1)
    wide = lambda x: x.reshape(rows, p * GROUP_SIZE)
    outs = pl.pallas_call(
        _s5_disc_body,
        out_shape=[jax.ShapeDtypeStruct((rows, p), F32)] * 2
        + [jax.ShapeDtypeStruct((rows, p * GROUP_SIZE), F32)] * 2,
        name="s5_discretise",
    )(two(a_re), two(a_im), log_dt.reshape(rows, 1), rep(a_re), rep(a_im), wide(b_re), wide(b_im))
    abr, abi, bbr, bbi = outs
    return (abr.reshape(ns, g, p), abi.reshape(ns, g, p),
            bbr.reshape(ns, g, p, GROUP_SIZE), bbi.reshape(ns, g, p, GROUP_SIZE))


SLAB_GROUPS = 16
SLAB = SLAB_GROUPS * GROUP_SIZE


def _s5_expand_body(xbr_ref, xbi_ref, xcr_ref, xci_ref, bm_ref, cm_ref, *, p):
    gpt = LANES // p
    tiles_slab = SLAB_GROUPS // gpt
    rg = lax.broadcasted_iota(jnp.int32, (SLAB, LANES), 0) // GROUP_SIZE
    lg = lax.broadcasted_iota(jnp.int32, (SLAB, LANES), 1) // p
    for half, x_ref in enumerate((xbr_ref, xbi_ref)):
        x = x_ref[...]
        for a in range(tiles_slab):
            col = (half * tiles_slab + a) * LANES
            bm_ref[:, col:col + LANES] = jnp.where(rg == gpt * a + lg, x, 0.0).astype(BF16)
    cg = lax.broadcasted_iota(jnp.int32, (LANES, SLAB), 1) // GROUP_SIZE
    rc = lax.broadcasted_iota(jnp.int32, (LANES, SLAB), 0) // p
    for half, x_ref in enumerate((xcr_ref, xci_ref)):
        x = x_ref[...]
        for a in range(tiles_slab):
            row = (half * tiles_slab + a) * LANES
            cm_ref[row:row + LANES, :] = jnp.where(cg == gpt * a + rc, x, 0.0).astype(BF16)


def _s5_block_weights(bb_re, bb_im, c_re, c_im):
    ns, g, p, gs = bb_re.shape
    s = g // SLAB_GROUPS
    n = ns * s
    gpt = LANES // p
    bcomp = lambda x: jnp.tile(
        x.reshape(n, SLAB_GROUPS, p, gs).transpose(0, 1, 3, 2).reshape(n, SLAB, p), (1, 1, gpt))
    ccomp = lambda x: jnp.tile(
        x.reshape(n, SLAB_GROUPS, gs, p).transpose(0, 3, 1, 2).reshape(n, p, SLAB), (1, gpt, 1))
    width = 2 * SLAB_GROUPS * p
    bspec = pl.BlockSpec((None, SLAB, LANES), lambda i: (i, 0, 0))
    cspec = pl.BlockSpec((None, LANES, SLAB), lambda i: (i, 0, 0))
    bmat, cmat = pl.pallas_call(
        functools.partial(_s5_expand_body, p=p),
        out_shape=[jax.ShapeDtypeStruct((n, SLAB, width), BF16),
                   jax.ShapeDtypeStruct((n, width, SLAB), BF16)],
        grid=(n,),
        in_specs=[bspec, bspec, cspec, cspec],
        out_specs=[pl.BlockSpec((None, SLAB, width), lambda i: (i, 0, 0)),
                   pl.BlockSpec((None, width, SLAB), lambda i: (i, 0, 0))],
        compiler_params=_cparams(1),
        name="s5_expand",
    )(bcomp(bb_re), bcomp(bb_im), ccomp(c_re), ccomp(-c_im))
    return bmat.reshape(ns, s, SLAB, width), cmat.reshape(ns, s, width, SLAB)


def _s5_pitch(rows):
    return rows + SUBLANES // 2


def _s5_project_in(u, bmat_ref, z, *, rows, pitch, n_slab, tiles_half):
    tiles_slab = tiles_half // n_slab
    for s in range(n_slab):
        bu = jnp.dot(u[:, s * SLAB:(s + 1) * SLAB].astype(BF16), bmat_ref[s],
                     preferred_element_type=F32)
        for half in range(2):
            for q in range(tiles_slab):
                j = half * tiles_half + s * tiles_slab + q
                col = (half * tiles_slab + q) * LANES
                z[j * pitch:j * pitch + rows, :] = bu[:, col:col + LANES]


def _s5_project_out(u, z, cmat_ref, d_ref, store, *, rows, pitch, n_slab, tiles_half):
    tiles_slab = tiles_half // n_slab
    for s in range(n_slab):
        parts = []
        for half in range(2):
            for q in range(tiles_slab):
                j = half * tiles_half + s * tiles_slab + q
                parts.append(z[j * pitch:j * pitch + rows, :].astype(BF16))
        st = jnp.concatenate(parts, axis=-1)
        y = jnp.dot(st, cmat_ref[s], preferred_element_type=F32)
        sl = slice(s * SLAB, (s + 1) * SLAB)
        y = y + d_ref[:, sl] * u[:, sl]
        store(sl, jax.nn.gelu(y))


def _s5_step(z, t, state, abar, *, pitch, tiles_half):
    nv = tiles_half // SUBLANES
    new = []
    for k in range(nv):
        idx_re = pl.ds(k * SUBLANES * pitch + t, SUBLANES, stride=pitch)
        idx_im = pl.ds((tiles_half + k * SUBLANES) * pitch + t, SUBLANES, stride=pitch)
        ar, ai = abar[k]
        sr, si = state[k]
        nr = ar * sr - ai * si + z[idx_re, :]
        ni = ar * si + ai * sr + z[idx_im, :]
        z[idx_re, :] = nr
        z[idx_im, :] = ni
        new.append((nr, ni))
    return tuple(new)


def _s5_p_body(u_ref, bmat_ref, cmat_ref, abr_ref, abi_ref, d_ref, o_ref, sre_ref, sim_ref,
               z, st_re, st_im, *, rows, tiles_per_seq, n_slab, tiles_half):
    m = pl.program_id(0)
    pitch = _s5_pitch(rows)
    nv = tiles_half // SUBLANES
    u = u_ref[...]
    _s5_project_in(u, bmat_ref, z, rows=rows, pitch=pitch, n_slab=n_slab, tiles_half=tiles_half)

    @pl.when(m % tiles_per_seq == 0)
    def _():
        st_re[...] = jnp.zeros(st_re.shape, F32)
        st_im[...] = jnp.zeros(st_im.shape, F32)

    vsl = lambda k: slice(k * SUBLANES, (k + 1) * SUBLANES)
    abar = tuple((abr_ref[vsl(k), :], abi_ref[vsl(k), :]) for k in range(nv))
    state0 = tuple((st_re[vsl(k), :], st_im[vsl(k), :]) for k in range(nv))

    def step(t, state):
        return _s5_step(z, t, state, abar, pitch=pitch, tiles_half=tiles_half)

    state = lax.fori_loop(0, rows, step, state0, unroll=8)
    for k in range(nv):
        st_re[vsl(k), :] = state[k][0]
        st_im[vsl(k), :] = state[k][1]
    sre_ref[...] = st_re[...]
    sim_ref[...] = st_im[...]

    def store(sl, val):
        o_ref[:, sl] = val.astype(o_ref.dtype)

    _s5_project_out(u, z, cmat_ref, d_ref, store, rows=rows, pitch=pitch, n_slab=n_slab,
                    tiles_half=tiles_half)


def _s5_prompt(u, bmat, cmat, abar_re, abar_im, dskip, *, layer, rows, seq_len):
    m, d = u.shape
    n_slab = bmat.shape[1]
    wsel = lambda a: pl.BlockSpec((None,) + a.shape[1:], lambda i: (layer, 0, 0, 0))
    tiles_half = abar_re.shape[0]
    pitch = _s5_pitch(rows)
    tiles_per_seq = seq_len // rows
    n_seq = m // seq_len
    st_spec = pl.BlockSpec((None, tiles_half, LANES), lambda i: (i // tiles_per_seq, 0, 0))
    full = lambda a: pl.BlockSpec(a.shape, lambda i: (0,) * a.ndim)
    return pl.pallas_call(
        functools.partial(_s5_p_body, rows=rows, tiles_per_seq=tiles_per_seq, n_slab=n_slab,
                          tiles_half=tiles_half),
        out_shape=[jax.ShapeDtypeStruct((m, d), BF16),
                   jax.ShapeDtypeStruct((n_seq, tiles_half, LANES), F32),
                   jax.ShapeDtypeStruct((n_seq, tiles_half, LANES), F32)],
        grid=(m // rows,),
        in_specs=[pl.BlockSpec((rows, d), lambda i: (i, 0)),
                  wsel(bmat), wsel(cmat), full(abar_re), full(abar_im), full(dskip)],
        out_specs=[pl.BlockSpec((rows, d), lambda i: (i, 0)), st_spec, st_spec],
        scratch_shapes=[pltpu.VMEM((2 * tiles_half * pitch, LANES), F32),
                        pltpu.VMEM((tiles_half, LANES), F32),
                        pltpu.VMEM((tiles_half, LANES), F32)],
        compiler_params=_cparams(1),
        name="s5_prompt",
    )(u, bmat, cmat, abar_re, abar_im, dskip)


def _s5_s_body(u_ref, h0r_ref, h0i_ref, bmat_ref, cmat_ref, abr_ref, abi_ref, d_ref,
               o_ref, sre_ref, sim_ref, z, zh, *, nb, seq, n_slab, tiles_half):
    rows = nb * seq
    pitch = _s5_pitch(rows)
    hp = _s5_pitch(nb)
    nv = tiles_half // SUBLANES
    d = u_ref.shape[2]
    u = u_ref[...].reshape(rows, d)
    _s5_project_in(u, bmat_ref, z, rows=rows, pitch=pitch, n_slab=n_slab, tiles_half=tiles_half)
    for j in range(tiles_half):
        cs = slice(j * LANES, (j + 1) * LANES)
        zh[j * hp:j * hp + nb, :] = h0r_ref[:, cs]
        zh[(tiles_half + j) * hp:(tiles_half + j) * hp + nb, :] = h0i_ref[:, cs]

    vsl = lambda k: slice(k * SUBLANES, (k + 1) * SUBLANES)
    abar = tuple((abr_ref[vsl(k), :], abi_ref[vsl(k), :]) for k in range(nv))

    def one_seq(b, carry):
        state = []
        for k in range(nv):
            idx_re = pl.ds(k * SUBLANES * hp + b, SUBLANES, stride=hp)
            idx_im = pl.ds((tiles_half + k * SUBLANES) * hp + b, SUBLANES, stride=hp)
            state.append((zh[idx_re, :], zh[idx_im, :]))
        state = tuple(state)
        for t in range(seq):
            state = _s5_step(z, t * nb + b, state, abar, pitch=pitch, tiles_half=tiles_half)
        for k in range(nv):
            idx_re = pl.ds(k * SUBLANES * hp + b, SUBLANES, stride=hp)
            idx_im = pl.ds((tiles_half + k * SUBLANES) * hp + b, SUBLANES, stride=hp)
            zh[idx_re, :] = state[k][0]
            zh[idx_im, :] = state[k][1]
        return carry

    lax.fori_loop(0, nb, one_seq, 0)
    for j in range(tiles_half):
        cs = slice(j * LANES, (j + 1) * LANES)
        sre_ref[:, cs] = zh[j * hp:j * hp + nb, :]
        sim_ref[:, cs] = zh[(tiles_half + j) * hp:(tiles_half + j) * hp + nb, :]

    def store(sl, val):
        o_ref[:, :, sl] = val.reshape(seq, nb, val.shape[1]).astype(o_ref.dtype)

    _s5_project_out(u, z, cmat_ref, d_ref, store, rows=rows, pitch=pitch, n_slab=n_slab,
                    tiles_half=tiles_half)


def _s5_sample(u, h0_re, h0_im, bmat, cmat, abar_re, abar_im, dskip, *, layer, nb):
    seq, b, d = u.shape
    nstate = h0_re.shape[1]
    n_slab = bmat.shape[1]
    wsel = lambda a: pl.BlockSpec((None,) + a.shape[1:], lambda i: (layer, 0, 0, 0))
    tiles_half = abar_re.shape[0]
    pitch = _s5_pitch(nb * seq)
    hp = _s5_pitch(nb)
    full = lambda a: pl.BlockSpec(a.shape, lambda i: (0,) * a.ndim)
    st_spec = pl.BlockSpec((nb, nstate), lambda i: (i, 0))
    u_spec = pl.BlockSpec((seq, nb, d), lambda i: (0, i, 0))
    return pl.pallas_call(
        functools.partial(_s5_s_body, nb=nb, seq=seq, n_slab=n_slab, tiles_half=tiles_half),
        out_shape=[jax.ShapeDtypeStruct((seq, b, d), BF16),
                   jax.ShapeDtypeStruct(h0_re.shape, F32),
                   jax.ShapeDtypeStruct(h0_im.shape, F32)],
        grid=(b // nb,),
        in_specs=[u_spec, st_spec, st_spec,
                  wsel(bmat), wsel(cmat), full(abar_re), full(abar_im), full(dskip)],
        out_specs=[u_spec, st_spec, st_spec],
        scratch_shapes=[pltpu.VMEM((2 * tiles_half * pitch, LANES), F32),
                        pltpu.VMEM((2 * tiles_half * hp, LANES), F32)],
        compiler_params=_cparams(1),
        name="s5_sample",
    )(u, h0_re, h0_im, bmat, cmat, abar_re, abar_im, dskip)


def _hn_dtype(i):
    return F32 if (i % N_MIXERS == 1) else BF16


def _run_sample(x, states, p):
    seq, nb, d = x.shape
    m = seq * nb
    depth = p["norm_mix"].shape[0]
    conv_st, ssm_re_st, ssm_im_st, ffn_st = states
    new_conv, new_re, new_im, new_ffn, wbf = None, [], [], [], []
    x = x.reshape(m, d)
    hn = _rmsnorm(x, p["norm_mix"][0], _hn_dtype(0), m)
    y = None
    for i in range(depth):
        j = i // N_MIXERS
        w = {}
        if i % N_MIXERS == 0:
            u, w["in_a"], w["in_b"] = _mm(hn, p["conv_w_in"], glu=True, bm=m, bn=512, layer=j)
            c, new_conv = _conv_sample(conv_st, u.reshape(seq, nb, d), p["conv_dw"],
                                       p["conv_dw_b"], p["conv_ln_g"], p["conv_ln_b"], j, bc=256,
                                       stacked=new_conv)
            x, hn, w["out"] = _mm(c.reshape(m, d), p["conv_w_out"], glu=False, bm=m, bn=512,
                                  layer=j, resid=x, gamma=p["norm_ffn"][i], hn_dtype=BF16)
        else:
            v, s_re, s_im = _s5_sample(hn.reshape(seq, nb, d), ssm_re_st[j].reshape(nb, -1),
                                       ssm_im_st[j].reshape(nb, -1), p["s5_bmat"],
                                       p["s5_cmat"], p["s5_abar_re"][j], p["s5_abar_im"][j],
                                       p["ssm_D"][j].reshape(1, d), layer=j, nb=32)
            new_re.append(s_re.reshape(ssm_re_st.shape[1:]))
            new_im.append(s_im.reshape(ssm_im_st.shape[1:]))
            x, hn, w["glu_a"], w["glu_b"] = _mm(
                v.reshape(m, d), p["ssm_w_glu"], glu=True, bm=m, bn=512, layer=j, resid=x,
                gamma=p["norm_ffn"][i], hn_dtype=BF16)
        act, st, w["gate"], w["up"] = _ffn1_sample(hn, p["ffn_w_gate"], p["ffn_w_up"],
                                                   p["ffn_conv"], ffn_st, i, bn=512, seq=seq)
        new_ffn.append(st)
        last = i == depth - 1
        gamma = p["norm_final"] if last else p["norm_mix"][i + 1]
        outs = _mm(act, p["ffn_w_down"], glu=False, bm=m, bn=512, layer=i, resid=x, gamma=gamma,
                   hn_dtype=F32 if last else _hn_dtype(i + 1), write_x=not last)
        if last:
            y, w["down"] = outs
        else:
            x, hn, w["down"] = outs
        wbf.append(w)
    return (y.reshape(seq, nb, d), new_conv, jnp.stack(new_re), jnp.stack(new_im),
            jnp.stack(new_ffn)), wbf


def _run_prompt(x, p, wbf, *, seq_len, bm):
    m, d = x.shape
    n_seq = m // seq_len
    depth = p["norm_mix"].shape[0]
    new_conv, new_re, new_im, new_ffn = [], [], [], []
    hn = None
    y = None
    for i in range(depth):
        j = i // N_MIXERS
        w = wbf[i]
        if i % N_MIXERS == 0:
            if i == 0:
                (u,) = _mm(x, (w["in_a"], w["in_b"]), glu=True, bm=bm, bn=1024,
                           prenorm=p["norm_mix"][0])
            else:
                (u,) = _mm(hn, (w["in_a"], w["in_b"]), glu=True, bm=bm, bn=1024)
            cs = p["conv_dw"].shape[1] - 1
            new_conv.append(u.reshape(n_seq, seq_len, d)[:, -cs:])
            c = _conv_prompt(u, p["conv_dw"], p["conv_dw_b"], p["conv_ln_g"], p["conv_ln_b"], j,
                             bm=512, seq_len=seq_len)
            x, hn = _mm_resident(c, (w["out"],), glu=False, bm=512, bn=512, resid=x,
                                 gamma=p["norm_ffn"][i], hn_dtype=BF16)
        else:
            v, s_re, s_im = _s5_prompt(hn, p["s5_bmat"], p["s5_cmat"], p["s5_abar_re"][j],
                                       p["s5_abar_im"][j], p["ssm_D"][j].reshape(1, d), layer=j, rows=256,
                                       seq_len=seq_len)
            g, pdim = p["ssm_A_re"].shape[1:]
            new_re.append(s_re.reshape(n_seq, g, pdim))
            new_im.append(s_im.reshape(n_seq, g, pdim))
            x, hn = _mm_resident(v, (w["glu_a"], w["glu_b"]), glu=True, bm=512, bn=512, resid=x,
                        gamma=p["norm_ffn"][i], hn_dtype=BF16)
        act, tail = _ffn1_prompt(hn, w["gate"], w["up"], p["ffn_conv"], i, bm=seq_len, bn=512,
                                 seq_len=seq_len)
        fs = p["ffn_conv"].shape[1] - 1
        new_ffn.append(tail[:, -fs:])
        last = i == depth - 1
        gamma = p["norm_final"] if last else p["norm_mix"][i + 1]
        outs = _mm_resident(act, (w["down"],), glu=False, bm=256, bn=512, resid=x, gamma=gamma,
                   hn_dtype=F32 if last else _hn_dtype(i + 1), write_x=not last)
        if last:
            (y,) = outs
        else:
            x, hn = outs
    return (y, jnp.stack(new_conv), jnp.stack(new_re), jnp.stack(new_im), jnp.stack(new_ffn))


def kernel(x_prompt, x_sample, state_conv, state_ssm_re, state_ssm_im, state_ffn, norm_mix, norm_ffn, norm_final, conv_w_in, conv_dw, conv_dw_b, conv_ln_g, conv_ln_b, conv_w_out, ssm_A_re, ssm_A_im, ssm_log_dt, ssm_B_re, ssm_B_im, ssm_C_re, ssm_C_im, ssm_D, ssm_w_glu, ffn_w_gate, ffn_w_up, ffn_conv, ffn_w_down):
    bp, sp, d = x_prompt.shape
    abr, abi, bbr, bbi = _s5_discretise(ssm_A_re, ssm_A_im, ssm_log_dt, ssm_B_re, ssm_B_im)
    ns, g, pdim = ssm_A_re.shape
    bmat, cmat = _s5_block_weights(bbr, bbi, ssm_C_re, ssm_C_im)
    tiles_half = g * pdim // LANES
    p = dict(
        norm_mix=norm_mix, norm_ffn=norm_ffn, norm_final=norm_final,
        conv_w_in=conv_w_in, conv_dw=conv_dw, conv_dw_b=conv_dw_b,
        conv_ln_g=conv_ln_g, conv_ln_b=conv_ln_b, conv_w_out=conv_w_out,
        ssm_A_re=ssm_A_re, ssm_D=ssm_D, ssm_w_glu=ssm_w_glu,
        s5_bmat=bmat, s5_cmat=cmat,
        s5_abar_re=abr.reshape(ns, tiles_half, LANES), s5_abar_im=abi.reshape(ns, tiles_half, LANES),
        ffn_w_gate=ffn_w_gate, ffn_w_up=ffn_w_up, ffn_conv=ffn_conv, ffn_w_down=ffn_w_down,
    )
    tm = lambda a: jnp.swapaxes(a, -3, -2)
    (y_s, conv_s, re_s, im_s, ffn_s), wbf = _run_sample(
        tm(x_sample), (tm(state_conv), state_ssm_re, state_ssm_im, tm(state_ffn)), p)
    y_p, conv_p, re_p, im_p, ffn_p = _run_prompt(
        x_prompt.reshape(bp * sp, d), p, wbf, seq_len=sp, bm=1024)
    return (y_p.reshape(bp, sp, d), tm(y_s), conv_p, tm(conv_s), re_p, im_p, re_s, im_s, ffn_p,
            tm(ffn_s))
```

```python
import functools

import jax
import jax.numpy as jnp
from jax import lax
from jax.experimental import pallas as pl
from jax.experimental.pallas import tpu as pltpu

F32 = jnp.float32
BF16 = jnp.bfloat16

EPS = 1e-6
N_MIXERS = 2
GROUP_SIZE = 16
LANES = 128
SUBLANES = 8
VMEM_LIMIT = 56 * 1024 * 1024
ROW_CHUNK = 256
FFN_ROW_CHUNK = 1024


def _cparams(n_axes):
    return pltpu.CompilerParams(
        dimension_semantics=("arbitrary",) * n_axes, vmem_limit_bytes=VMEM_LIMIT)


def _rms_scale(ss, n):
    return lax.rsqrt(ss / n + EPS)


def _rmsnorm_body(x_ref, g_ref, o_ref):
    x = x_ref[...]
    ss = jnp.sum(x * x, axis=-1, keepdims=True)
    o_ref[...] = ((x * _rms_scale(ss, x.shape[-1])) * g_ref[...]).astype(o_ref.dtype)


def _rmsnorm(x, g, out_dtype, bm):
    m, d = x.shape
    return pl.pallas_call(
        _rmsnorm_body,
        out_shape=jax.ShapeDtypeStruct((m, d), out_dtype),
        grid=(m // bm,),
        in_specs=[pl.BlockSpec((bm, d), lambda i: (i, 0)),
                  pl.BlockSpec((1, d), lambda i: (0, 0))],
        out_specs=pl.BlockSpec((bm, d), lambda i: (i, 0)),
        compiler_params=_cparams(1),
        name="rmsnorm",
    )(x, g.reshape(1, d))


def _mm_body(*refs, glu, has_res, write_x, emit, prenorm, nn, bn, n_out):
    it = iter(refs)
    lhs_ref = next(it)
    pg_ref = next(it) if prenorm else None
    wa_ref = next(it)
    wb_ref = next(it) if glu else None
    res_ref = next(it) if has_res else None
    g_ref = next(it) if has_res else None
    xo_ref = next(it) if write_x else None
    hn_ref = next(it) if has_res else None
    wao_ref = next(it) if emit else None
    wbo_ref = next(it) if (emit and glu) else None
    rowbuf = next(it) if has_res else None
    ss_ref = next(it) if has_res else None
    hn_scr = next(it) if prenorm else None

    n = pl.program_id(1)
    bm = lhs_ref.shape[0]
    if prenorm:
        @pl.when(n == 0)
        def _():
            x = lhs_ref[...]
            ss = jnp.sum(x * x, axis=-1, keepdims=True)
            hn_scr[...] = ((x * _rms_scale(ss, x.shape[-1])) * pg_ref[...]).astype(BF16)
        lhs_ref = hn_scr
    if emit:
        wao_ref[...] = wa_ref[...].astype(BF16)
        wa_ref = wao_ref
        if glu:
            wbo_ref[...] = wb_ref[...].astype(BF16)
            wb_ref = wbo_ref
    if has_res:
        @pl.when(n == 0)
        def _():
            ss_ref[...] = jnp.zeros(ss_ref.shape, F32)
    for r0 in range(0, bm, ROW_CHUNK):
        rs = slice(r0, min(r0 + ROW_CHUNK, bm))
        x = lhs_ref[rs, :]
        val = jnp.dot(x, wa_ref[...], preferred_element_type=F32)
        if glu:
            gate = jnp.dot(x, wb_ref[...], preferred_element_type=F32)
            val = val * jax.nn.sigmoid(gate)
        if has_res:
            val = res_ref[rs, :] + val
        if write_x:
            xo_ref[rs, :] = val
        if has_res:
            rowbuf[n, rs, :] = val
            ss_ref[rs, :] = ss_ref[rs, :] + jnp.sum(val * val, axis=-1, keepdims=True)

    if has_res:
        @pl.when(n == nn - 1)
        def _():
            scale = _rms_scale(ss_ref[...], n_out)
            for j in range(nn):
                sl = slice(j * bn, (j + 1) * bn)
                hn_ref[:, sl] = ((rowbuf[j] * scale) * g_ref[:, sl]).astype(hn_ref.dtype)


def _mm(lhs, w, *, glu, bm, bn, layer=None, resid=None, gamma=None, hn_dtype=None, write_x=True,
        prenorm=None):
    m, k = lhs.shape
    emit = layer is not None
    if emit:
        assert m == bm
        n_out = w.shape[2] // (2 if glu else 1)
    else:
        n_out = w[0].shape[1]
    nn = n_out // bn
    has_res = resid is not None
    in_specs = [pl.BlockSpec((bm, k), lambda i, j: (i, 0))]
    args = [lhs]
    if prenorm is not None:
        in_specs.append(pl.BlockSpec((1, k), lambda i, j: (0, 0)))
        args.append(prenorm.reshape(1, k))
    if emit:
        in_specs.append(pl.BlockSpec((None, k, bn), lambda i, j: (layer, 0, j)))
        args.append(w)
        if glu:
            in_specs.append(pl.BlockSpec((None, k, bn), lambda i, j: (layer, 0, j + nn)))
            args.append(w)
    else:
        for wi in w:
            in_specs.append(pl.BlockSpec((k, bn), lambda i, j: (0, j)))
            args.append(wi)
    out_shape, out_specs, scratch = [], [], []
    if has_res:
        in_specs += [pl.BlockSpec((bm, bn), lambda i, j: (i, j)),
                     pl.BlockSpec((1, n_out), lambda i, j: (0, 0))]
        args += [resid, gamma.reshape(1, n_out)]
    if write_x:
        out_shape.append(jax.ShapeDtypeStruct((m, n_out), F32))
        out_specs.append(pl.BlockSpec((bm, bn), lambda i, j: (i, j)))
    if has_res:
        out_shape.append(jax.ShapeDtypeStruct((m, n_out), hn_dtype))
        out_specs.append(pl.BlockSpec((bm, n_out), lambda i, j: (i, 0)))
        scratch += [pltpu.VMEM((nn, bm, bn), F32), pltpu.VMEM((bm, 1), F32)]
    if emit:
        for _ in range(2 if glu else 1):
            out_shape.append(jax.ShapeDtypeStruct((k, n_out), BF16))
            out_specs.append(pl.BlockSpec((k, bn), lambda i, j: (0, j)))
    if prenorm is not None:
        scratch.append(pltpu.VMEM((bm, k), BF16))
    return pl.pallas_call(
        functools.partial(_mm_body, glu=glu, has_res=has_res, write_x=write_x, emit=emit,
                          prenorm=prenorm is not None, nn=nn, bn=bn, n_out=n_out),
        out_shape=out_shape,
        grid=(m // bm, nn),
        in_specs=in_specs,
        out_specs=out_specs,
        scratch_shapes=scratch,
        compiler_params=_cparams(2),
        name="mm_glu" if glu else "mm_lin",
    )(*args)


def _mmr_body(*refs, glu, write_x, bn, n_out):
    it = iter(refs)
    lhs_ref = next(it)
    wa_ref = next(it)
    wb_ref = next(it) if glu else None
    res_ref = next(it)
    g_ref = next(it)
    xo_ref = next(it) if write_x else None
    hn_ref = next(it)
    bm = lhs_ref.shape[0]
    rows = min(ROW_CHUNK, bm)
    for r0 in range(0, bm, rows):
        rs = slice(r0, r0 + rows)
        x = lhs_ref[rs, :]
        ss = jnp.zeros((rows, 1), F32)
        vals = []
        for j in range(n_out // bn):
            sl = slice(j * bn, (j + 1) * bn)
            val = jnp.dot(x, wa_ref[:, sl], preferred_element_type=F32)
            if glu:
                gate = jnp.dot(x, wb_ref[:, sl], preferred_element_type=F32)
                val = val * jax.nn.sigmoid(gate)
            val = res_ref[rs, sl] + val
            if write_x:
                xo_ref[rs, sl] = val
            ss = ss + jnp.sum(val * val, axis=-1, keepdims=True)
            vals.append(val)
        scale = _rms_scale(ss, n_out)
        for j, val in enumerate(vals):
            sl = slice(j * bn, (j + 1) * bn)
            hn_ref[rs, sl] = ((val * scale) * g_ref[:, sl]).astype(hn_ref.dtype)


def _mm_resident(lhs, w, *, glu, bm, bn, resid, gamma, hn_dtype, write_x=True):
    m, k = lhs.shape
    n_out = w[0].shape[1]
    row = lambda width: pl.BlockSpec((bm, width), lambda i: (i, 0))
    in_specs = [row(k)] + [pl.BlockSpec((k, n_out), lambda i: (0, 0)) for _ in w]
    in_specs += [row(n_out), pl.BlockSpec((1, n_out), lambda i: (0, 0))]
    out_shape, out_specs = [], []
    if write_x:
        out_shape.append(jax.ShapeDtypeStruct((m, n_out), F32))
        out_specs.append(row(n_out))
    out_shape.append(jax.ShapeDtypeStruct((m, n_out), hn_dtype))
    out_specs.append(row(n_out))
    return pl.pallas_call(
        functools.partial(_mmr_body, glu=glu, write_x=write_x, bn=bn, n_out=n_out),
        out_shape=out_shape,
        grid=(m // bm,),
        in_specs=in_specs,
        out_specs=out_specs,
        compiler_params=_cparams(1),
        name="mmr_glu" if glu else "mmr_lin",
    )(lhs, *w, resid, gamma.reshape(1, n_out))


def _shift_rows(g, tail, k):
    r = pltpu.roll(g, k, 0)
    t = pltpu.roll(tail, k, 0)
    i8 = lax.broadcasted_iota(jnp.int32, tail.shape, 0)
    head = jnp.where(i8 < k, t, r[0:SUBLANES, :])
    return jnp.concatenate([head, r[SUBLANES:, :]], axis=0)


def _ffn1_prompt_body(h_ref, wg_ref, wu_ref, cw_ref, act_ref, tail_ref, carry, *, bm, tiles_per_seq):
    m = pl.program_id(0)
    n = pl.program_id(1)
    cw = cw_ref[...]
    taps = cw.shape[0]

    @pl.when(m % tiles_per_seq == 0)
    def _():
        carry[n] = jnp.zeros(carry.shape[1:], F32)

    tail = carry[n]
    for r0 in range(0, bm, FFN_ROW_CHUNK):
        rs = slice(r0, r0 + FFN_ROW_CHUNK)
        h = h_ref[rs, :]
        g = jnp.dot(h, wg_ref[...], preferred_element_type=F32)
        up = jnp.dot(h, wu_ref[...], preferred_element_type=F32)
        gc = g * cw[taps - 1:taps, :]
        for k in range(taps - 1):
            gc = gc + _shift_rows(g, tail, taps - 1 - k) * cw[k:k + 1, :]
        act_ref[rs, :] = (jax.nn.silu(gc) * up).astype(act_ref.dtype)
        tail = g[FFN_ROW_CHUNK - SUBLANES:, :]
    carry[n] = tail
    tail_ref[...] = tail


def _ffn1_prompt(h, wg, wu, cw, layer, *, bm, bn, seq_len):
    m, k = h.shape
    f = wg.shape[1]
    nn = f // bn
    tiles_per_seq = seq_len // bm
    act, tails = pl.pallas_call(
        functools.partial(_ffn1_prompt_body, bm=bm, tiles_per_seq=tiles_per_seq),
        out_shape=[jax.ShapeDtypeStruct((m, f), BF16),
                   jax.ShapeDtypeStruct((m // bm, SUBLANES, f), F32)],
        grid=(m // bm, nn),
        in_specs=[pl.BlockSpec((bm, k), lambda i, j: (i, 0)),
                  pl.BlockSpec((k, bn), lambda i, j: (0, j)),
                  pl.BlockSpec((k, bn), lambda i, j: (0, j)),
                  pl.BlockSpec((None, cw.shape[1], bn), lambda i, j: (layer, 0, j))],
        out_specs=[pl.BlockSpec((bm, bn), lambda i, j: (i, j)),
                   pl.BlockSpec((None, SUBLANES, bn), lambda i, j: (i, 0, j))],
        scratch_shapes=[pltpu.VMEM((nn, SUBLANES, bn), F32)],
        compiler_params=_cparams(2),
        name="ffn_gate_up",
    )(h, wg, wu, cw)
    return act, tails[tiles_per_seq - 1::tiles_per_seq]


def _ffn1_sample_body(h_ref, wg_ref, wu_ref, cw_ref, c_ref, *rest, nb, seq):
    act_ref, st_ref, wgo_ref, wuo_ref = rest[-4:]
    wgo_ref[...] = wg_ref[...].astype(BF16)
    wuo_ref[...] = wu_ref[...].astype(BF16)
    h = h_ref[...]
    g = jnp.dot(h, wgo_ref[...], preferred_element_type=F32)
    up = jnp.dot(h, wuo_ref[...], preferred_element_type=F32)
    cw = cw_ref[...]
    taps = cw.shape[0]
    hist = [c_ref[r] for r in range(taps - 1)] + [g[t * nb:(t + 1) * nb, :] for t in range(seq)]
    for t in range(seq):
        gc = hist[t] * cw[0:1, :]
        for k in range(1, taps):
            gc = gc + hist[t + k] * cw[k:k + 1, :]
        act_ref[t * nb:(t + 1) * nb, :] = (
            jax.nn.silu(gc) * up[t * nb:(t + 1) * nb, :]).astype(act_ref.dtype)
    for r in range(taps - 1):
        st_ref[r] = hist[seq + r]


def _ffn1_sample(h, wg, wu, cw, cache, layer, *, bn, seq, stacked=None):
    m, k = h.shape
    f = wg.shape[2]
    nb = m // seq
    hist = cw.shape[1] - 1
    wspec = pl.BlockSpec((None, k, bn), lambda j: (layer, 0, j))
    wout = pl.BlockSpec((k, bn), lambda j: (0, j))
    st_spec = pl.BlockSpec((None, hist, nb, bn), lambda j: (layer, 0, 0, j))
    in_specs = [pl.BlockSpec((m, k), lambda j: (0, 0)), wspec, wspec,
                pl.BlockSpec((None, cw.shape[1], bn), lambda j: (layer, 0, j)), st_spec]
    args = [h, wg, wu, cw, cache]
    aliases = {}
    if stacked is not None:
        in_specs.append(pl.BlockSpec(memory_space=pl.ANY))
        args.append(stacked)
        aliases = {len(args) - 1: 1}
    return pl.pallas_call(
        functools.partial(_ffn1_sample_body, nb=nb, seq=seq),
        out_shape=[jax.ShapeDtypeStruct((m, f), BF16),
                   jax.ShapeDtypeStruct(cache.shape, F32),
                   jax.ShapeDtypeStruct((k, f), BF16),
                   jax.ShapeDtypeStruct((k, f), BF16)],
        grid=(f // bn,),
        in_specs=in_specs,
        out_specs=[pl.BlockSpec((m, bn), lambda j: (0, j)), st_spec, wout, wout],
        input_output_aliases=aliases,
        compiler_params=_cparams(1),
        name="ffn_gate_up_s",
    )(*args)


CONV_ROWS = 32
HALO = 32


def _ln_silu(c, g, b):
    mu = jnp.mean(c, axis=-1, keepdims=True)
    d = c - mu
    var = jnp.mean(d * d, axis=-1, keepdims=True)
    y = d * lax.rsqrt(var + EPS)
    return jax.nn.silu(y * g + b)


def _conv_p_body(u_ref, dw_ref, db_ref, lg_ref, lb_ref, o_ref, ubuf, cbuf, *, bm, taps,
                 tiles_per_seq):
    m = pl.program_id(0)
    d = u_ref.shape[1]
    nc = d // LANES

    @pl.when(m % tiles_per_seq == 0)
    def _():
        ubuf[:, 0:HALO, :] = jnp.zeros((nc, HALO, LANES), F32)

    @pl.when(m % tiles_per_seq != 0)
    def _():
        ubuf[:, 0:HALO, :] = ubuf[:, bm:bm + HALO, :]

    for c in range(nc):
        ubuf[c, HALO:HALO + bm, :] = u_ref[:, c * LANES:(c + 1) * LANES]
    off = HALO - (taps - 1)

    def lane_chunk(c, carry):
        wv = dw_ref[c]
        bias = db_ref[c]
        for r0 in range(0, bm, CONV_ROWS):
            acc = jnp.zeros((CONV_ROWS, LANES), F32)
            for k in range(taps):
                acc = acc + ubuf[c, r0 + off + k:r0 + off + k + CONV_ROWS, :] * wv[k:k + 1, :]
            cbuf[c, r0:r0 + CONV_ROWS, :] = acc + bias
        return carry

    lax.fori_loop(0, nc, lane_chunk, 0)
    cfull = jnp.concatenate([cbuf[c] for c in range(nc)], axis=-1)
    o_ref[...] = _ln_silu(cfull, lg_ref[...], lb_ref[...]).astype(o_ref.dtype)


def _lane_chunked(x):
    l, r, d = x.shape
    return x.reshape(l, r, d // LANES, LANES).transpose(0, 2, 1, 3)


def _vec3(x):
    return x.reshape(x.shape[0], 1, x.shape[1])


def _conv_prompt(u, dw, db, lg, lb, layer, *, bm, seq_len):
    m, d = u.shape
    taps = dw.shape[1]
    nc = d // LANES
    vec = lambda: pl.BlockSpec((None, 1, d), lambda i: (layer, 0, 0))
    return pl.pallas_call(
        functools.partial(_conv_p_body, bm=bm, taps=taps, tiles_per_seq=seq_len // bm),
        out_shape=jax.ShapeDtypeStruct((m, d), BF16),
        grid=(m // bm,),
        in_specs=[pl.BlockSpec((bm, d), lambda i: (i, 0)),
                  pl.BlockSpec((None, nc, taps, LANES), lambda i: (layer, 0, 0, 0)),
                  pl.BlockSpec((None, nc, 1, LANES), lambda i: (layer, 0, 0, 0)),
                  vec(), vec()],
        out_specs=pl.BlockSpec((bm, d), lambda i: (i, 0)),
        scratch_shapes=[pltpu.VMEM((nc, HALO + bm, LANES), F32),
                        pltpu.VMEM((nc, bm, LANES), F32)],
        compiler_params=_cparams(1),
        name="conv_prompt",
    )(u, _lane_chunked(dw), _lane_chunked(_vec3(db)), _vec3(lg), _vec3(lb))


def _conv_s_body(cache_ref, u_ref, dw_ref, db_ref, lg_ref, lb_ref, *rest, taps, seq, nsteps):
    o_ref, st_ref, cfull = rest[-3:]
    c = pl.program_id(0)
    hist = taps - 1
    ext = lambda r: cache_ref[r] if r < hist else u_ref[r - hist]
    wv = dw_ref[...]
    bias = db_ref[...]
    for t in range(seq):
        acc = ext(t) * wv[0:1, :]
        for k in range(1, taps):
            acc = acc + ext(t + k) * wv[k:k + 1, :]
        cfull[c, t] = acc + bias
    for r in range(hist):
        st_ref[r] = ext(r + seq)

    @pl.when(c == nsteps - 1)
    def _():
        for t in range(seq):
            row = jnp.concatenate([cfull[cc, t] for cc in range(nsteps)], axis=-1)
            o_ref[t] = _ln_silu(row, lg_ref[...], lb_ref[...]).astype(o_ref.dtype)


def _conv_sample(cache, u, dw, db, lg, lb, layer, *, bc, stacked=None):
    seq, b, d = u.shape
    taps = dw.shape[1]
    hist = taps - 1
    nsteps = d // bc
    vec = lambda: pl.BlockSpec((None, 1, d), lambda i: (layer, 0, 0))
    in_specs = [pl.BlockSpec((None, hist, b, bc), lambda i: (layer, 0, 0, i)),
                pl.BlockSpec((seq, b, bc), lambda i: (0, 0, i)),
                pl.BlockSpec((None, taps, bc), lambda i: (layer, 0, i)),
                pl.BlockSpec((None, 1, bc), lambda i: (layer, 0, i)),
                vec(), vec()]
    args = [cache, u, dw, _vec3(db), _vec3(lg), _vec3(lb)]
    aliases = {}
    if stacked is not None:
        in_specs.append(pl.BlockSpec(memory_space=pl.ANY))
        args.append(stacked)
        aliases = {len(args) - 1: 1}
    return pl.pallas_call(
        functools.partial(_conv_s_body, taps=taps, seq=seq, nsteps=nsteps),
        out_shape=[jax.ShapeDtypeStruct((seq, b, d), BF16),
                   jax.ShapeDtypeStruct(cache.shape, F32)],
        grid=(nsteps,),
        in_specs=in_specs,
        out_specs=[pl.BlockSpec((seq, b, d), lambda i: (0, 0, 0)),
                   pl.BlockSpec((None, hist, b, bc), lambda i: (layer, 0, 0, i))],
        scratch_shapes=[pltpu.VMEM((nsteps, seq, b, bc), F32)],
        input_output_aliases=aliases,
        compiler_params=_cparams(1),
        name="conv_sample",
    )(*args)


def _s5_abar(lam_re, lam_im, dt):
    mag = jnp.exp(lam_re * dt)
    ang = lam_im * dt
    return mag * jnp.cos(ang), mag * jnp.sin(ang)


def _s5_disc_body(are_ref, aim_ref, ldt_ref, arep_ref, airep_ref, bre_ref, bim_ref,
                  abr_ref, abi_ref, bbr_ref, bbi_ref):
    dt = jnp.exp(ldt_ref[...])
    abar_re, abar_im = _s5_abar(are_ref[...], aim_ref[...], dt)
    abr_ref[...] = abar_re
    abi_ref[...] = abar_im
    lam_re = arep_ref[...]
    lam_im = airep_ref[...]
    rep_re, rep_im = _s5_abar(lam_re, lam_im, dt)
    nr = rep_re - 1.0
    ni = rep_im
    den = lam_re * lam_re + lam_im * lam_im
    q_re = (nr * lam_re + ni * lam_im) / den
    q_im = (ni * lam_re - nr * lam_im) / den
    br = bre_ref[...]
    bi = bim_ref[...]
    bbr_ref[...] = q_re * br - q_im * bi
    bbi_ref[...] = q_re * bi + q_im * br


def _s5_discretise(a_re, a_im, log_dt, b_re, b_im):
    ns, g, p = a_re.shape
    rows = ns * g
    two = lambda x: x.reshape(rows, p)
    rep = lambda x: jnp.repeat(x.reshape(rows, p), GROUP_SIZE, axis=1)
    wide = lambda x: x.reshape(rows, p * GROUP_SIZE)
    outs = pl.pallas_call(
        _s5_disc_body,
        out_shape=[jax.ShapeDtypeStruct((rows, p), F32)] * 2
        + [jax.ShapeDtypeStruct((rows, p * GROUP_SIZE), F32)] * 2,
        name="s5_discretise",
    )(two(a_re), two(a_im), log_dt.reshape(rows, 1), rep(a_re), rep(a_im), wide(b_re), wide(b_im))
    abr, abi, bbr, bbi = outs
    return (abr.reshape(ns, g, p), abi.reshape(ns, g, p),
            bbr.reshape(ns, g, p, GROUP_SIZE), bbi.reshape(ns, g, p, GROUP_SIZE))


SLAB_GROUPS = 16
SLAB = SLAB_GROUPS * GROUP_SIZE


def _s5_expand_body(xbr_ref, xbi_ref, xcr_ref, xci_ref, bm_ref, cm_ref, *, p):
    gpt = LANES // p
    tiles_slab = SLAB_GROUPS // gpt
    rg = lax.broadcasted_iota(jnp.int32, (SLAB, LANES), 0) // GROUP_SIZE
    lg = lax.broadcasted_iota(jnp.int32, (SLAB, LANES), 1) // p
    for half, x_ref in enumerate((xbr_ref, xbi_ref)):
        x = x_ref[...]
        for a in range(tiles_slab):
            col = (half * tiles_slab + a) * LANES
            bm_ref[:, col:col + LANES] = jnp.where(rg == gpt * a + lg, x, 0.0).astype(BF16)
    cg = lax.broadcasted_iota(jnp.int32, (LANES, SLAB), 1) // GROUP_SIZE
    rc = lax.broadcasted_iota(jnp.int32, (LANES, SLAB), 0) // p
    for half, x_ref in enumerate((xcr_ref, xci_ref)):
        x = x_ref[...]
        for a in range(tiles_slab):
            row = (half * tiles_slab + a) * LANES
            cm_ref[row:row + LANES, :] = jnp.where(cg == gpt * a + rc, x, 0.0).astype(BF16)


def _s5_block_weights(bb_re, bb_im, c_re, c_im):
    ns, g, p, gs = bb_re.shape
    s = g // SLAB_GROUPS
    n = ns * s
    gpt = LANES // p
    bcomp = lambda x: jnp.tile(
        x.reshape(n, SLAB_GROUPS, p, gs).transpose(0, 1, 3, 2).reshape(n, SLAB, p), (1, 1, gpt))
    ccomp = lambda x: jnp.tile(
        x.reshape(n, SLAB_GROUPS, gs, p).transpose(0, 3, 1, 2).reshape(n, p, SLAB), (1, gpt, 1))
    width = 2 * SLAB_GROUPS * p
    bspec = pl.BlockSpec((None, SLAB, LANES), lambda i: (i, 0, 0))
    cspec = pl.BlockSpec((None, LANES, SLAB), lambda i: (i, 0, 0))
    bmat, cmat = pl.pallas_call(
        functools.partial(_s5_expand_body, p=p),
        out_shape=[jax.ShapeDtypeStruct((n, SLAB, width), BF16),
                   jax.ShapeDtypeStruct((n, width, SLAB), BF16)],
        grid=(n,),
        in_specs=[bspec, bspec, cspec, cspec],
        out_specs=[pl.BlockSpec((None, SLAB, width), lambda i: (i, 0, 0)),
                   pl.BlockSpec((None, width, SLAB), lambda i: (i, 0, 0))],
        compiler_params=_cparams(1),
        name="s5_expand",
    )(bcomp(bb_re), bcomp(bb_im), ccomp(c_re), ccomp(-c_im))
    return bmat.reshape(ns, s, SLAB, width), cmat.reshape(ns, s, width, SLAB)


def _s5_pitch(rows):
    return rows + SUBLANES // 2


def _s5_project_in(u, bmat_ref, z, *, rows, pitch, n_slab, tiles_half):
    tiles_slab = tiles_half // n_slab
    for s in range(n_slab):
        bu = jnp.dot(u[:, s * SLAB:(s + 1) * SLAB].astype(BF16), bmat_ref[s],
                     preferred_element_type=F32)
        for half in range(2):
            for q in range(tiles_slab):
                j = half * tiles_half + s * tiles_slab + q
                col = (half * tiles_slab + q) * LANES
                z[j * pitch:j * pitch + rows, :] = bu[:, col:col + LANES]


def _s5_project_out(u, z, cmat_ref, d_ref, store, *, rows, pitch, n_slab, tiles_half):
    tiles_slab = tiles_half // n_slab
    for s in range(n_slab):
        parts = []
        for half in range(2):
            for q in range(tiles_slab):
                j = half * tiles_half + s * tiles_slab + q
                parts.append(z[j * pitch:j * pitch + rows, :].astype(BF16))
        st = jnp.concatenate(parts, axis=-1)
        y = jnp.dot(st, cmat_ref[s], preferred_element_type=F32)
        sl = slice(s * SLAB, (s + 1) * SLAB)
        y = y + d_ref[:, sl] * u[:, sl]
        store(sl, jax.nn.gelu(y))


def _s5_step(z, t, state, abar, *, pitch, tiles_half):
    nv = tiles_half // SUBLANES
    new = []
    for k in range(nv):
        idx_re = pl.ds(k * SUBLANES * pitch + t, SUBLANES, stride=pitch)
        idx_im = pl.ds((tiles_half + k * SUBLANES) * pitch + t, SUBLANES, stride=pitch)
        ar, ai = abar[k]
        sr, si = state[k]
        nr = ar * sr - ai * si + z[idx_re, :]
        ni = ar * si + ai * sr + z[idx_im, :]
        z[idx_re, :] = nr
        z[idx_im, :] = ni
        new.append((nr, ni))
    return tuple(new)


def _s5_p_body(u_ref, bmat_ref, cmat_ref, abr_ref, abi_ref, d_ref, o_ref, sre_ref, sim_ref,
               z, st_re, st_im, *, rows, tiles_per_seq, n_slab, tiles_half):
    m = pl.program_id(0)
    pitch = _s5_pitch(rows)
    nv = tiles_half // SUBLANES
    u = u_ref[...]
    _s5_project_in(u, bmat_ref, z, rows=rows, pitch=pitch, n_slab=n_slab, tiles_half=tiles_half)

    @pl.when(m % tiles_per_seq == 0)
    def _():
        st_re[...] = jnp.zeros(st_re.shape, F32)
        st_im[...] = jnp.zeros(st_im.shape, F32)

    vsl = lambda k: slice(k * SUBLANES, (k + 1) * SUBLANES)
    abar = tuple((abr_ref[vsl(k), :], abi_ref[vsl(k), :]) for k in range(nv))
    state0 = tuple((st_re[vsl(k), :], st_im[vsl(k), :]) for k in range(nv))

    def step(t, state):
        return _s5_step(z, t, state, abar, pitch=pitch, tiles_half=tiles_half)

    state = lax.fori_loop(0, rows, step, state0, unroll=8)
    for k in range(nv):
        st_re[vsl(k), :] = state[k][0]
        st_im[vsl(k), :] = state[k][1]
    sre_ref[...] = st_re[...]
    sim_ref[...] = st_im[...]

    def store(sl, val):
        o_ref[:, sl] = val.astype(o_ref.dtype)

    _s5_project_out(u, z, cmat_ref, d_ref, store, rows=rows, pitch=pitch, n_slab=n_slab,
                    tiles_half=tiles_half)


def _s5_prompt(u, bmat, cmat, abar_re, abar_im, dskip, *, layer, rows, seq_len):
    m, d = u.shape
    n_slab = bmat.shape[1]
    wsel = lambda a: pl.BlockSpec((None,) + a.shape[1:], lambda i: (layer, 0, 0, 0))
    tiles_half = abar_re.shape[0]
    pitch = _s5_pitch(rows)
    tiles_per_seq = seq_len // rows
    n_seq = m // seq_len
    st_spec = pl.BlockSpec((None, tiles_half, LANES), lambda i: (i // tiles_per_seq, 0, 0))
    full = lambda a: pl.BlockSpec(a.shape, lambda i: (0,) * a.ndim)
    return pl.pallas_call(
        functools.partial(_s5_p_body, rows=rows, tiles_per_seq=tiles_per_seq, n_slab=n_slab,
                          tiles_half=tiles_half),
        out_shape=[jax.ShapeDtypeStruct((m, d), BF16),
                   jax.ShapeDtypeStruct((n_seq, tiles_half, LANES), F32),
                   jax.ShapeDtypeStruct((n_seq, tiles_half, LANES), F32)],
        grid=(m // rows,),
        in_specs=[pl.BlockSpec((rows, d), lambda i: (i, 0)),
                  wsel(bmat), wsel(cmat), full(abar_re), full(abar_im), full(dskip)],
        out_specs=[pl.BlockSpec((rows, d), lambda i: (i, 0)), st_spec, st_spec],
        scratch_shapes=[pltpu.VMEM((2 * tiles_half * pitch, LANES), F32),
                        pltpu.VMEM((tiles_half, LANES), F32),
                        pltpu.VMEM((tiles_half, LANES), F32)],
        compiler_params=_cparams(1),
        name="s5_prompt",
    )(u, bmat, cmat, abar_re, abar_im, dskip)


def _s5_s_body(u_ref, h0r_ref, h0i_ref, bmat_ref, cmat_ref, abr_ref, abi_ref, d_ref,
               o_ref, sre_ref, sim_ref, z, zh, *, nb, seq, n_slab, tiles_half):
    rows = nb * seq
    pitch = _s5_pitch(rows)
    hp = _s5_pitch(nb)
    nv = tiles_half // SUBLANES
    d = u_ref.shape[2]
    u = u_ref[...].reshape(rows, d)
    _s5_project_in(u, bmat_ref, z, rows=rows, pitch=pitch, n_slab=n_slab, tiles_half=tiles_half)
    for j in range(tiles_half):
        cs = slice(j * LANES, (j + 1) * LANES)
        zh[j * hp:j * hp + nb, :] = h0r_ref[:, cs]
        zh[(tiles_half + j) * hp:(tiles_half + j) * hp + nb, :] = h0i_ref[:, cs]

    vsl = lambda k: slice(k * SUBLANES, (k + 1) * SUBLANES)
    abar = tuple((abr_ref[vsl(k), :], abi_ref[vsl(k), :]) for k in range(nv))

    def one_seq(b, carry):
        state = []
        for k in range(nv):
            idx_re = pl.ds(k * SUBLANES * hp + b, SUBLANES, stride=hp)
            idx_im = pl.ds((tiles_half + k * SUBLANES) * hp + b, SUBLANES, stride=hp)
            state.append((zh[idx_re, :], zh[idx_im, :]))
        state = tuple(state)
        for t in range(seq):
            state = _s5_step(z, t * nb + b, state, abar, pitch=pitch, tiles_half=tiles_half)
        for k in range(nv):
            idx_re = pl.ds(k * SUBLANES * hp + b, SUBLANES, stride=hp)
            idx_im = pl.ds((tiles_half + k * SUBLANES) * hp + b, SUBLANES, stride=hp)
            zh[idx_re, :] = state[k][0]
            zh[idx_im, :] = state[k][1]
        return carry

    lax.fori_loop(0, nb, one_seq, 0)
    for j in range(tiles_half):
        cs = slice(j * LANES, (j + 1) * LANES)
        sre_ref[:, cs] = zh[j * hp:j * hp + nb, :]
        sim_ref[:, cs] = zh[(tiles_half + j) * hp:(tiles_half + j) * hp + nb, :]

    def store(sl, val):
        o_ref[:, :, sl] = val.reshape(seq, nb, val.shape[1]).astype(o_ref.dtype)

    _s5_project_out(u, z, cmat_ref, d_ref, store, rows=rows, pitch=pitch, n_slab=n_slab,
                    tiles_half=tiles_half)


def _s5_sample(u, h0_re, h0_im, bmat, cmat, abar_re, abar_im, dskip, *, layer, nb):
    seq, b, d = u.shape
    nstate = h0_re.shape[1]
    n_slab = bmat.shape[1]
    wsel = lambda a: pl.BlockSpec((None,) + a.shape[1:], lambda i: (layer, 0, 0, 0))
    tiles_half = abar_re.shape[0]
    pitch = _s5_pitch(nb * seq)
    hp = _s5_pitch(nb)
    full = lambda a: pl.BlockSpec(a.shape, lambda i: (0,) * a.ndim)
    st_spec = pl.BlockSpec((nb, nstate), lambda i: (i, 0))
    u_spec = pl.BlockSpec((seq, nb, d), lambda i: (0, i, 0))
    return pl.pallas_call(
        functools.partial(_s5_s_body, nb=nb, seq=seq, n_slab=n_slab, tiles_half=tiles_half),
        out_shape=[jax.ShapeDtypeStruct((seq, b, d), BF16),
                   jax.ShapeDtypeStruct(h0_re.shape, F32),
                   jax.ShapeDtypeStruct(h0_im.shape, F32)],
        grid=(b // nb,),
        in_specs=[u_spec, st_spec, st_spec,
                  wsel(bmat), wsel(cmat), full(abar_re), full(abar_im), full(dskip)],
        out_specs=[u_spec, st_spec, st_spec],
        scratch_shapes=[pltpu.VMEM((2 * tiles_half * pitch, LANES), F32),
                        pltpu.VMEM((2 * tiles_half * hp, LANES), F32)],
        compiler_params=_cparams(1),
        name="s5_sample",
    )(u, h0_re, h0_im, bmat, cmat, abar_re, abar_im, dskip)


def _hn_dtype(i):
    return F32 if (i % N_MIXERS == 1) else BF16


def _run_sample(x, states, p):
    seq, nb, d = x.shape
    m = seq * nb
    depth = p["norm_mix"].shape[0]
    conv_st, ssm_re_st, ssm_im_st, ffn_st = states
    new_conv, new_re, new_im, new_ffn, wbf = None, [], [], None, []
    x = x.reshape(m, d)
    hn = _rmsnorm(x, p["norm_mix"][0], _hn_dtype(0), m)
    y = None
    for i in range(depth):
        j = i // N_MIXERS
        w = {}
        if i % N_MIXERS == 0:
            u, w["in_a"], w["in_b"] = _mm(hn, p["conv_w_in"], glu=True, bm=m, bn=512, layer=j)
            c, new_conv = _conv_sample(conv_st, u.reshape(seq, nb, d), p["conv_dw"],
                                       p["conv_dw_b"], p["conv_ln_g"], p["conv_ln_b"], j, bc=256,
                                       stacked=new_conv)
            x, hn, w["out"] = _mm(c.reshape(m, d), p["conv_w_out"], glu=False, bm=m, bn=512,
                                  layer=j, resid=x, gamma=p["norm_ffn"][i], hn_dtype=BF16)
        else:
            v, s_re, s_im = _s5_sample(hn.reshape(seq, nb, d), ssm_re_st[j].reshape(nb, -1),
                                       ssm_im_st[j].reshape(nb, -1), p["s5_bmat"],
                                       p["s5_cmat"], p["s5_abar_re"][j], p["s5_abar_im"][j],
                                       p["ssm_D"][j].reshape(1, d), layer=j, nb=32)
            new_re.append(s_re.reshape(ssm_re_st.shape[1:]))
            new_im.append(s_im.reshape(ssm_im_st.shape[1:]))
            x, hn, w["glu_a"], w["glu_b"] = _mm(
                v.reshape(m, d), p["ssm_w_glu"], glu=True, bm=m, bn=512, layer=j, resid=x,
                gamma=p["norm_ffn"][i], hn_dtype=BF16)
        act, new_ffn, w["gate"], w["up"] = _ffn1_sample(
            hn, p["ffn_w_gate"], p["ffn_w_up"], p["ffn_conv"], ffn_st, i, bn=512, seq=seq,
            stacked=new_ffn)
        last = i == depth - 1
        gamma = p["norm_final"] if last else p["norm_mix"][i + 1]
        outs = _mm(act, p["ffn_w_down"], glu=False, bm=m, bn=512, layer=i, resid=x, gamma=gamma,
                   hn_dtype=F32 if last else _hn_dtype(i + 1), write_x=not last)
        if last:
            y, w["down"] = outs
        else:
            x, hn, w["down"] = outs
        wbf.append(w)
    return (y.reshape(seq, nb, d), new_conv, jnp.stack(new_re), jnp.stack(new_im),
            new_ffn), wbf


def _run_prompt(x, p, wbf, *, seq_len, bm):
    m, d = x.shape
    n_seq = m // seq_len
    depth = p["norm_mix"].shape[0]
    new_conv, new_re, new_im, new_ffn = [], [], [], []
    hn = None
    y = None
    for i in range(depth):
        j = i // N_MIXERS
        w = wbf[i]
        if i % N_MIXERS == 0:
            if i == 0:
                (u,) = _mm(x, (w["in_a"], w["in_b"]), glu=True, bm=bm, bn=1024,
                           prenorm=p["norm_mix"][0])
            else:
                (u,) = _mm(hn, (w["in_a"], w["in_b"]), glu=True, bm=bm, bn=1024)
            cs = p["conv_dw"].shape[1] - 1
            new_conv.append(u.reshape(n_seq, seq_len, d)[:, -cs:])
            c = _conv_prompt(u, p["conv_dw"], p["conv_dw_b"], p["conv_ln_g"], p["conv_ln_b"], j,
                             bm=512, seq_len=seq_len)
            x, hn = _mm_resident(c, (w["out"],), glu=False, bm=512, bn=512, resid=x,
                                 gamma=p["norm_ffn"][i], hn_dtype=BF16)
        else:
            v, s_re, s_im = _s5_prompt(hn, p["s5_bmat"], p["s5_cmat"], p["s5_abar_re"][j],
                                       p["s5_abar_im"][j], p["ssm_D"][j].reshape(1, d), layer=j, rows=256,
                                       seq_len=seq_len)
            g, pdim = p["ssm_A_re"].shape[1:]
            new_re.append(s_re.reshape(n_seq, g, pdim))
            new_im.append(s_im.reshape(n_seq, g, pdim))
            x, hn = _mm_resident(v, (w["glu_a"], w["glu_b"]), glu=True, bm=512, bn=512, resid=x,
                        gamma=p["norm_ffn"][i], hn_dtype=BF16)
        act, tail = _ffn1_prompt(hn, w["gate"], w["up"], p["ffn_conv"], i, bm=seq_len, bn=512,
                                 seq_len=seq_len)
        fs = p["ffn_conv"].shape[1] - 1
        new_ffn.append(tail[:, -fs:])
        last = i == depth - 1
        gamma = p["norm_final"] if last else p["norm_mix"][i + 1]
        outs = _mm_resident(act, (w["down"],), glu=False, bm=256, bn=512, resid=x, gamma=gamma,
                   hn_dtype=F32 if last else _hn_dtype(i + 1), write_x=not last)
        if last:
            (y,) = outs
        else:
            x, hn = outs
    return (y, jnp.stack(new_conv), jnp.stack(new_re), jnp.stack(new_im), jnp.stack(new_ffn))


def kernel(x_prompt, x_sample, state_conv, state_ssm_re, state_ssm_im, state_ffn, norm_mix, norm_ffn, norm_final, conv_w_in, conv_dw, conv_dw_b, conv_ln_g, conv_ln_b, conv_w_out, ssm_A_re, ssm_A_im, ssm_log_dt, ssm_B_re, ssm_B_im, ssm_C_re, ssm_C_im, ssm_D, ssm_w_glu, ffn_w_gate, ffn_w_up, ffn_conv, ffn_w_down):
    bp, sp, d = x_prompt.shape
    abr, abi, bbr, bbi = _s5_discretise(ssm_A_re, ssm_A_im, ssm_log_dt, ssm_B_re, ssm_B_im)
    ns, g, pdim = ssm_A_re.shape
    bmat, cmat = _s5_block_weights(bbr, bbi, ssm_C_re, ssm_C_im)
    tiles_half = g * pdim // LANES
    p = dict(
        norm_mix=norm_mix, norm_ffn=norm_ffn, norm_final=norm_final,
        conv_w_in=conv_w_in, conv_dw=conv_dw, conv_dw_b=conv_dw_b,
        conv_ln_g=conv_ln_g, conv_ln_b=conv_ln_b, conv_w_out=conv_w_out,
        ssm_A_re=ssm_A_re, ssm_D=ssm_D, ssm_w_glu=ssm_w_glu,
        s5_bmat=bmat, s5_cmat=cmat,
        s5_abar_re=abr.reshape(ns, tiles_half, LANES), s5_abar_im=abi.reshape(ns, tiles_half, LANES),
        ffn_w_gate=ffn_w_gate, ffn_w_up=ffn_w_up, ffn_conv=ffn_conv, ffn_w_down=ffn_w_down,
    )
    tm = lambda a: jnp.swapaxes(a, -3, -2)
    (y_s, conv_s, re_s, im_s, ffn_s), wbf = _run_sample(
        tm(x_sample), (tm(state_conv), state_ssm_re, state_ssm_im, tm(state_ffn)), p)
    y_p, conv_p, re_p, im_p, ffn_p = _run_prompt(
        x_prompt.reshape(bp * sp, d), p, wbf, seq_len=sp, bm=1024)
    return (y_p.reshape(bp, sp, d), tm(y_s), conv_p, tm(conv_s), re_p, im_p, re_s, im_s, ffn_p,
            tm(ffn_s))
```
